```python
import jax
import jax.numpy as jnp
from jax import lax
import numpy as np

D_MODEL = 1024
BATCH = 8
SEQ = 2048
DEPTH = 4

GRID_W = 64
N_MIXERS = 4
GROUP_W = D_MODEL // N_MIXERS
D_MIX = N_MIXERS * GROUP_W
HEAD_DIM = 64
D_FF = 4 * D_MODEL
NORM_EPS = 1e-5

RWKV_HEADS = GROUP_W // HEAD_DIM
RWKV_W_RANK = 64
RWKV_A_RANK = 64
RWKV_G_RANK = 128
RWKV_DECAY_SCALE = 0.6065306597126334
RWKV_GN_EPS = 64e-5
RWKV_COLS = 3 * GROUP_W + 2 * RWKV_W_RANK + 2 * RWKV_A_RANK + RWKV_G_RANK

SSD_HEADS = GROUP_W // HEAD_DIM
SSD_BC_GROUPS = 2
SSD_D_STATE = 128
SSD_CONV = 4
SSD_CHUNK = 64
SSD_XBC = GROUP_W + 2 * SSD_BC_GROUPS * SSD_D_STATE
SSD_COLS = GROUP_W + SSD_XBC + 2 * SSD_HEADS

LRU_BLOCKS = 4
LRU_BLOCK_W = GROUP_W // LRU_BLOCKS
LRU_CONV = 4
LRU_C = 8.0
LRU_COLS = 2 * GROUP_W

NA_HEADS = GROUP_W // HEAD_DIM
NA_WIN_ROWS = 8
NA_WIN_COLS = 16
NA_QCOL_BLOCK = 16
NA_KCOL_BLOCK = NA_QCOL_BLOCK + NA_WIN_COLS
NA_COLS = 3 * GROUP_W

IN_COLS = RWKV_COLS + SSD_COLS + LRU_COLS + NA_COLS

kernel_name = 'hybrid_bidir_rwkv7_ssd_rglru_natten_encoder'


def split_cols(t, sizes):
    offs = np.cumsum(sizes)[:-1].tolist()
    return jnp.split(t, offs, axis=-1)


def rms_norm(x, w):
    xf = x.astype(jnp.float32)
    y = xf * lax.rsqrt(jnp.mean(xf * xf, axis=-1, keepdims=True) + NORM_EPS)
    return (y * w.astype(jnp.float32)).astype(x.dtype)


def centred_dwconv(u, w, b):
    k = w.shape[0]
    out = lax.conv_general_dilated(u, w[:, None, :].astype(u.dtype), window_strides=(1,),
                                   padding=[(k // 2, k - 1 - k // 2)],
                                   dimension_numbers=('NWC', 'WIO', 'NWC'),
                                   feature_group_count=u.shape[-1])
    return out + b.astype(u.dtype)


def bidir_token_shift(u, mu):
    prev = jnp.pad(u[:, :-1], ((0, 0), (1, 0), (0, 0)))
    nxt = jnp.pad(u[:, 1:], ((0, 0), (0, 1), (0, 0)))
    return u + mu[0] * (prev - u) + mu[1] * (nxt - u)


def time_major(t):
    t = jnp.stack([t[0], t[1][:, ::-1]])
    return jnp.moveaxis(t, 2, 0)


def rwkv7_mixer(u, shift_mu, w0, w_up, a0, a_up, g_up, k_k, k_a, r_k, gn_w, gn_b):
    f32 = jnp.float32
    bsz, s, _ = u.shape
    u = bidir_token_shift(u, shift_mu)
    r, k, v, wd, ad, gd = split_cols(u, [GROUP_W, GROUP_W, GROUP_W, 2 * RWKV_W_RANK,
                                         2 * RWKV_A_RANK, RWKV_G_RANK])
    wd = jnp.tanh(wd).reshape(bsz, s, 2, RWKV_W_RANK)
    ad = ad.reshape(bsz, s, 2, RWKV_A_RANK)
    z_w = w0[:, None, None, :] + jnp.einsum('bsdr,drc->dbsc', wd, w_up)
    decay = jnp.exp(-RWKV_DECAY_SCALE * jax.nn.sigmoid(z_w.astype(f32)))
    alpha = jax.nn.sigmoid((a0[:, None, None, :] + jnp.einsum('bsdr,drc->dbsc', ad, a_up)).astype(f32))
    g = jax.nn.sigmoid(gd) @ g_up
    heads = lambda t: t.reshape(t.shape[:-1] + (RWKV_HEADS, HEAD_DIM))
    rh = heads(r.astype(f32))
    vh = heads(v.astype(f32))
    kk = heads((k * k_k).astype(f32))
    kk = kk * lax.rsqrt(jnp.sum(kk * kk, axis=-1, keepdims=True) + 1e-12)
    kd = heads(k.astype(f32)[None] * (1.0 + (alpha - 1.0) * k_a))
    two = lambda t: jnp.broadcast_to(t[None], (2,) + t.shape)
    seq_in = (time_major(two(rh)), time_major(heads(decay)), time_major(kd),
              time_major(two(vh)), time_major(two(kk)), time_major(heads(alpha) * kk[None]))

    def step(state, inp):
        r_t, w_t, k_t, v_t, kk_t, b_t = inp
        s_kk = jnp.einsum('dbhij,dbhj->dbhi', state, kk_t)
        state = (state * w_t[..., None, :] - s_kk[..., :, None] * b_t[..., None, :]
                 + v_t[..., :, None] * k_t[..., None, :])
        return state, jnp.einsum('dbhij,dbhj->dbhi', state, r_t)

    state0 = jnp.zeros((2, bsz, RWKV_HEADS, HEAD_DIM, HEAD_DIM), f32)
    _, ys = lax.scan(step, state0, seq_in)
    ys = jnp.moveaxis(ys, 0, 2)
    y = ys[0] + ys[1][:, ::-1]
    mean = jnp.mean(y, axis=-1, keepdims=True)
    var = jnp.mean(jnp.square(y - mean), axis=-1, keepdims=True)
    y = ((y - mean) * lax.rsqrt(var + RWKV_GN_EPS)).reshape(bsz, s, GROUP_W) * gn_w + gn_b
    bonus = jnp.sum(rh[None] * kd * r_k, axis=(0, -1))[..., None] * vh
    y = (y + bonus.reshape(bsz, s, GROUP_W)) * g
    return y.astype(u.dtype)


def segsum(x):
    t = x.shape[-1]
    cs = jnp.cumsum(x, axis=-1)
    seg = cs[..., :, None] - cs[..., None, :]
    return jnp.where(jnp.tril(jnp.ones((t, t), dtype=bool)), seg, -jnp.inf)


def ssd_chunked(x, a, bm, cm):
    nb, s, h, p = x.shape
    n = bm.shape[-1]
    c, l = s // SSD_CHUNK, SSD_CHUNK
    x = x.reshape(nb, c, l, h, p)
    bm = bm.reshape(nb, c, l, h, n)
    cm = cm.reshape(nb, c, l, h, n)
    a = a.reshape(nb, c, l, h).transpose(0, 3, 1, 2)
    a_cs = jnp.cumsum(a, axis=-1)
    scores = jnp.einsum('bclhn,bcshn->bhcls', cm, bm) * jnp.exp(segsum(a))
    y_diag = jnp.einsum('bhcls,bcshp->bclhp', scores, x)
    decay_to_end = jnp.exp(a_cs[..., -1:] - a_cs)
    states = jnp.einsum('bclhn,bhcl,bclhp->bchpn', bm, decay_to_end, x)
    states = jnp.concatenate([jnp.zeros_like(states[:, :1]), states], axis=1)
    chunk_decay = jnp.exp(segsum(jnp.pad(a_cs[..., -1], ((0, 0), (0, 0), (1, 0)))))
    states = jnp.einsum('bhzc,bchpn->bzhpn', chunk_decay, states)[:, :-1]
    y_off = jnp.einsum('bclhn,bchpn,bhcl->bclhp', cm, states, jnp.exp(a_cs))
    return (y_diag + y_off).reshape(nb, s, h, p)


def ssd_mixer(u, conv_w, conv_b, dt_bias, a_log, d_skip, norm_w):
    f32 = jnp.float32
    bsz, s, _ = u.shape
    z, xbc, dt_raw = split_cols(u, [GROUP_W, SSD_XBC, 2 * SSD_HEADS])
    xbc = jax.nn.silu(centred_dwconv(xbc, conv_w, conv_b))
    xs, bm, cm = split_cols(xbc, [GROUP_W, SSD_BC_GROUPS * SSD_D_STATE, SSD_BC_GROUPS * SSD_D_STATE])
    rep = SSD_HEADS // SSD_BC_GROUPS
    xs = xs.reshape(bsz, s, SSD_HEADS, HEAD_DIM).astype(f32)
    bm = jnp.repeat(bm.reshape(bsz, s, SSD_BC_GROUPS, SSD_D_STATE), rep, axis=2).astype(f32)
    cm = jnp.repeat(cm.reshape(bsz, s, SSD_BC_GROUPS, SSD_D_STATE), rep, axis=2).astype(f32)
    dt = jax.nn.softplus(dt_raw.reshape(bsz, s, 2, SSD_HEADS).astype(f32) + dt_bias)
    a = -jnp.exp(a_log.astype(f32))
    dirs = lambda tf, tb: jnp.concatenate([tf, tb[:, ::-1]], axis=0)
    x_dt = dirs(xs * dt[:, :, 0, :, None], xs * dt[:, :, 1, :, None])
    a_dt = dirs(dt[:, :, 0] * a[0], dt[:, :, 1] * a[1])
    y = ssd_chunked(x_dt, a_dt, dirs(bm, bm), dirs(cm, cm))
    y = y[:bsz] + y[bsz:, ::-1] + d_skip[:, None] * xs
    y = y.reshape(bsz, s, GROUP_W) * jax.nn.silu(z.astype(f32))
    return rms_norm(y, norm_w).astype(u.dtype)


def rglru_mixer(u, conv_w, conv_b, gate_a_w, gate_a_b, gate_x_w, gate_x_b, lam):
    f32 = jnp.float32
    bsz, s, _ = u.shape
    gate_in, x_in = split_cols(u, [GROUP_W, GROUP_W])
    gate = jax.nn.gelu(gate_in.astype(f32), approximate=True)
    xf = centred_dwconv(x_in, conv_w, conv_b).astype(f32)
    xb = xf.reshape(bsz, s, LRU_BLOCKS, LRU_BLOCK_W)

    def block_gate(w, bias):
        z = jnp.einsum('bski,dkij->dbskj', xb, w.astype(f32)).reshape(2, bsz, s, GROUP_W)
        return jax.nn.sigmoid(z + bias[:, None, None, :])

    rec = block_gate(gate_a_w, gate_a_b)
    inp = block_gate(gate_x_w, gate_x_b)
    log_a = -LRU_C * rec * jax.nn.softplus(-lam.astype(f32))[:, None, None, :]
    a = jnp.exp(log_a)
    bterm = jnp.sqrt(-jnp.expm1(2.0 * log_a)) * inp * xf[None]
    a = jnp.stack([a[0], a[1][:, ::-1]])
    bterm = jnp.stack([bterm[0], bterm[1][:, ::-1]])

    def combine(lhs, rhs):
        return (lhs[0] * rhs[0], rhs[0] * lhs[1] + rhs[1])

    _, h = lax.associative_scan(combine, (a, bterm), axis=2)
    h = h[0] + h[1][:, ::-1]
    return (gate * h).astype(u.dtype)


def neighbourhood_attention(q, k, v, rel_bias):
    bsz, s, _ = q.shape
    rows = s // GRID_W
    kr = min(NA_WIN_ROWS, rows)
    ncb = GRID_W // NA_QCOL_BLOCK
    grid = lambda t: t.reshape(bsz, rows, GRID_W, NA_HEADS, HEAD_DIM).transpose(0, 3, 1, 2, 4)
    qg, kg, vg = grid(q), grid(k), grid(v)
    r_idx = jnp.arange(rows)
    key_rows = jnp.clip(r_idx - kr // 2, 0, rows - kr)[:, None] + jnp.arange(kr)[None, :]
    cb = jnp.arange(ncb)
    key_cols = (jnp.clip(cb * NA_QCOL_BLOCK - NA_WIN_COLS // 2, 0, GRID_W - NA_KCOL_BLOCK)[:, None]
                + jnp.arange(NA_KCOL_BLOCK)[None, :])
    ri = key_rows[:, None, :, None]
    ci = key_cols[None, :, None, :]
    k_blk = kg[:, :, ri, ci]
    v_blk = vg[:, :, ri, ci]
    q_blk = qg.reshape(bsz, NA_HEADS, rows, ncb, NA_QCOL_BLOCK, HEAD_DIM)
    scores = jnp.einsum('bhrcqd,bhrckwd->bhrcqkw', q_blk, k_blk).astype(jnp.float32) * (HEAD_DIM ** -0.5)
    q_cols = cb[:, None] * NA_QCOL_BLOCK + jnp.arange(NA_QCOL_BLOCK)[None, :]
    win_start = jnp.clip(q_cols - NA_WIN_COLS // 2, 0, GRID_W - NA_WIN_COLS)
    kc = key_cols[:, None, :]
    in_win = (kc >= win_start[..., None]) & (kc < win_start[..., None] + NA_WIN_COLS)
    row_i = key_rows - r_idx[:, None] + NA_WIN_ROWS - 1
    col_i = jnp.clip(kc - q_cols[..., None] + NA_WIN_COLS - 1, 0, 2 * NA_WIN_COLS - 2)
    bias = rel_bias[:, row_i[:, None, None, :, None], col_i[None, :, :, None, :]]
    scores = jnp.where(in_win[:, :, None, :], scores + bias.astype(jnp.float32)[None], -jnp.inf)
    shp = scores.shape
    p = jax.nn.softmax(scores.reshape(shp[:-2] + (kr * NA_KCOL_BLOCK,)), axis=-1).reshape(shp)
    out = jnp.einsum('bhrcqkw,bhrckwd->bhrcqd', p.astype(v.dtype), v_blk)
    return out.reshape(bsz, NA_HEADS, rows, GRID_W, HEAD_DIM).transpose(0, 2, 3, 1, 4).reshape(bsz, s, GROUP_W)


def setup_inputs(seed: int = 0) -> dict:
    key = jax.random.key(seed)
    ks = iter(jax.random.split(key, 40))
    f32 = jnp.float32
    nrm = lambda shape, scale: scale * jax.random.normal(next(ks), shape, f32)
    uni = lambda shape, lo, hi: jax.random.uniform(next(ks), shape, f32, lo, hi)
    L = DEPTH
    x = nrm((BATCH, SEQ, D_MODEL), 1.0)
    norm1_w = 1.0 + nrm((L, D_MODEL), 0.05)
    w_in = nrm((L, D_MODEL, IN_COLS), D_MODEL ** -0.5)
    rwkv_shift_mu = uni((L, 2, RWKV_COLS), 0.0, 0.5)
    rwkv_w0 = uni((L, 2, GROUP_W), -6.0, 1.0)
    rwkv_w_up = nrm((L, 2, RWKV_W_RANK, GROUP_W), 0.1)
    rwkv_a0 = nrm((L, 2, GROUP_W), 0.5)
    rwkv_a_up = nrm((L, 2, RWKV_A_RANK, GROUP_W), 0.1)
    rwkv_g_up = nrm((L, RWKV_G_RANK, GROUP_W), RWKV_G_RANK ** -0.5)
    rwkv_k_k = 0.85 + nrm((L, GROUP_W), 0.05)
    rwkv_k_a = 1.0 + nrm((L, GROUP_W), 0.05)
    rwkv_r_k = nrm((L, RWKV_HEADS, HEAD_DIM), 0.1)
    rwkv_gn_w = 1.0 + nrm((L, GROUP_W), 0.05)
    rwkv_gn_b = nrm((L, GROUP_W), 0.02)
    ssd_conv_w = nrm((L, SSD_CONV, SSD_XBC), 0.5)
    ssd_conv_b = nrm((L, SSD_XBC), 0.02)
    dt0 = jnp.exp(uni((L, 2, SSD_HEADS), float(np.log(1e-3)), float(np.log(1e-1))))
    ssd_dt_bias = dt0 + jnp.log(-jnp.expm1(-dt0))
    ssd_a_log = jnp.log(uni((L, 2, SSD_HEADS), 1.0, 16.0))
    ssd_d = 1.0 + nrm((L, SSD_HEADS), 0.1)
    ssd_norm_w = 1.0 + nrm((L, GROUP_W), 0.05)
    lru_conv_w = nrm((L, LRU_CONV, GROUP_W), 0.5)
    lru_conv_b = nrm((L, GROUP_W), 0.02)
    lru_gate_a_w = nrm((L, 2, LRU_BLOCKS, LRU_BLOCK_W, LRU_BLOCK_W), LRU_BLOCK_W ** -0.5)
    lru_gate_a_b = nrm((L, 2, GROUP_W), 0.1)
    lru_gate_x_w = nrm((L, 2, LRU_BLOCKS, LRU_BLOCK_W, LRU_BLOCK_W), LRU_BLOCK_W ** -0.5)
    lru_gate_x_b = nrm((L, 2, GROUP_W), 0.1)
    a_c = uni((L, 2, GROUP_W), 0.9, 0.999) ** (1.0 / LRU_C)
    lru_lambda = jnp.log(a_c) - jnp.log1p(-a_c)
    na_rel_bias = nrm((L, NA_HEADS, 2 * NA_WIN_ROWS - 1, 2 * NA_WIN_COLS - 1), 0.2)
    w_out = nrm((L, D_MIX, D_MODEL), D_MIX ** -0.5)
    norm2_w = 1.0 + nrm((L, D_MODEL), 0.05)
    w_mlp1 = nrm((L, D_MODEL, D_FF), D_MODEL ** -0.5)
    w_mlp2 = nrm((L, D_FF, D_MODEL), D_FF ** -0.5)
    final_norm_w = 1.0 + nrm((D_MODEL,), 0.05)
    return {'x': x, 'norm1_w': norm1_w, 'w_in': w_in,
            'rwkv_shift_mu': rwkv_shift_mu, 'rwkv_w0': rwkv_w0, 'rwkv_w_up': rwkv_w_up,
            'rwkv_a0': rwkv_a0, 'rwkv_a_up': rwkv_a_up, 'rwkv_g_up': rwkv_g_up,
            'rwkv_k_k': rwkv_k_k, 'rwkv_k_a': rwkv_k_a, 'rwkv_r_k': rwkv_r_k,
            'rwkv_gn_w': rwkv_gn_w, 'rwkv_gn_b': rwkv_gn_b,
            'ssd_conv_w': ssd_conv_w, 'ssd_conv_b': ssd_conv_b, 'ssd_dt_bias': ssd_dt_bias,
            'ssd_a_log': ssd_a_log, 'ssd_d': ssd_d, 'ssd_norm_w': ssd_norm_w,
            'lru_conv_w': lru_conv_w, 'lru_conv_b': lru_conv_b,
            'lru_gate_a_w': lru_gate_a_w, 'lru_gate_a_b': lru_gate_a_b,
            'lru_gate_x_w': lru_gate_x_w, 'lru_gate_x_b': lru_gate_x_b, 'lru_lambda': lru_lambda,
            'na_rel_bias': na_rel_bias, 'w_out': w_out, 'norm2_w': norm2_w,
            'w_mlp1': w_mlp1, 'w_mlp2': w_mlp2, 'final_norm_w': final_norm_w}


def reference(x, norm1_w, w_in, rwkv_shift_mu, rwkv_w0, rwkv_w_up, rwkv_a0, rwkv_a_up, rwkv_g_up,
              rwkv_k_k, rwkv_k_a, rwkv_r_k, rwkv_gn_w, rwkv_gn_b, ssd_conv_w, ssd_conv_b, ssd_dt_bias,
              ssd_a_log, ssd_d, ssd_norm_w, lru_conv_w, lru_conv_b, lru_gate_a_w, lru_gate_a_b,
              lru_gate_x_w, lru_gate_x_b, lru_lambda, na_rel_bias, w_out, norm2_w, w_mlp1, w_mlp2,
              final_norm_w):
    h = x
    for layer in range(DEPTH):
        u = rms_norm(h, norm1_w[layer]) @ w_in[layer]
        u_a, u_b, u_c, u_d = split_cols(u, [RWKV_COLS, SSD_COLS, LRU_COLS, NA_COLS])
        y_a = rwkv7_mixer(u_a, rwkv_shift_mu[layer], rwkv_w0[layer], rwkv_w_up[layer], rwkv_a0[layer],
                          rwkv_a_up[layer], rwkv_g_up[layer], rwkv_k_k[layer], rwkv_k_a[layer],
                          rwkv_r_k[layer], rwkv_gn_w[layer], rwkv_gn_b[layer])
        y_b = ssd_mixer(u_b, ssd_conv_w[layer], ssd_conv_b[layer], ssd_dt_bias[layer], ssd_a_log[layer],
                        ssd_d[layer], ssd_norm_w[layer])
        y_c = rglru_mixer(u_c, lru_conv_w[layer], lru_conv_b[layer], lru_gate_a_w[layer], lru_gate_a_b[layer],
                          lru_gate_x_w[layer], lru_gate_x_b[layer], lru_lambda[layer])
        q, k, v = split_cols(u_d, [GROUP_W, GROUP_W, GROUP_W])
        y_d = neighbourhood_attention(q, k, v, na_rel_bias[layer])
        h = h + jnp.concatenate([y_a, y_b, y_c, y_d], axis=-1) @ w_out[layer]
        m = rms_norm(h, norm2_w[layer]) @ w_mlp1[layer]
        h = h + jnp.square(jax.nn.relu(m)) @ w_mlp2[layer]
    return rms_norm(h, final_norm_w)
```

```python
import functools

import numpy as np
import jax
import jax.numpy as jnp
from jax import lax
from jax.experimental import pallas as pl
from jax.experimental.pallas import tpu as pltpu

F32 = jnp.float32
BF16 = jnp.bfloat16
HI = lax.Precision.HIGHEST

D_MODEL = 1024
GRID_W = 64
GROUP_W = 256
HEAD_DIM = 64
N_HEADS = GROUP_W // HEAD_DIM
D_FF = 4 * D_MODEL
NORM_EPS = 1e-5

RWKV_RANK2 = 128
RWKV_DECAY_SCALE = 0.6065306597126334
RWKV_GN_EPS = 64e-5
RWKV_COLS = 3 * GROUP_W + 3 * RWKV_RANK2

SSD_D_STATE = 128
SSD_XBC = GROUP_W + 4 * SSD_D_STATE
SSD_DT_PAD = 128
SSD_COLS_PAD = GROUP_W + SSD_XBC + SSD_DT_PAD

LRU_C = 8.0
LRU_COLS = 2 * GROUP_W
NA_COLS = 3 * GROUP_W
NA_WIN_ROWS = 8
NA_WIN_COLS = 16
NA_MASKED = -1e30

U_COLS = RWKV_COLS + SSD_COLS_PAD + NA_COLS + LRU_COLS
CHUNK = 64
ROWS = 256

V7X_VMEM_LIMIT = 60 * 1024 * 1024


def _dot(a, b, prec=None):
    return jnp.dot(a, b, preferred_element_type=F32, precision=prec)


def _dot_t(a, b, prec=None):
    return lax.dot_general(a, b, (((1,), (1,)), ((), ())), preferred_element_type=F32, precision=prec)


def _tdot(a, b, prec=None):
    return lax.dot_general(a, b, (((0,), (0,)), ((), ())), preferred_element_type=F32, precision=prec)


def _bdot(a, b):
    return _dot(a.astype(BF16), b.astype(BF16))


def _bdot_t(a, b):
    return _dot_t(a.astype(BF16), b.astype(BF16))


def _btdot(a, b):
    return _tdot(a.astype(BF16), b.astype(BF16))


def _iota(shape, dim):
    return lax.broadcasted_iota(jnp.int32, shape, dim)


def _head_mask(h, width=GROUP_W):
    lane = _iota((1, width), 1)
    return ((lane >= h * HEAD_DIM) & (lane < (h + 1) * HEAD_DIM)).astype(F32)


def _sigmoid(x):
    return 1.0 / (1.0 + jnp.exp(-x))


def _softplus(x):
    return jnp.maximum(x, 0.0) + jnp.log(1.0 + jnp.exp(-jnp.abs(x)))


def _shift_down(x, prev8, k):
    rx = pltpu.roll(x, k, 0)
    row8 = _iota((8, x.shape[1]), 0)
    head = jnp.where(row8 < k, pltpu.roll(prev8, k, 0), rx[0:8])
    return jnp.concatenate([head, rx[8:]], axis=0)


def _shift_up(x, next8, k):
    n = x.shape[0]
    rx = pltpu.roll(x, n - k, 0)
    row8 = _iota((8, x.shape[1]), 0)
    tail = jnp.where(row8 >= 8 - k, pltpu.roll(next8, 8 - k, 0), rx[n - 8:])
    return jnp.concatenate([rx[:n - 8], tail], axis=0)


def _halo_rows(ref, t0, nrows, seq, cols):
    pstart = pl.multiple_of(jnp.maximum(t0 - 8, 0), 8)
    nstart = pl.multiple_of(jnp.minimum(t0 + nrows, seq - 8), 8)
    prev8 = ref[pl.ds(pstart, 8), cols] * (t0 > 0).astype(F32)
    next8 = ref[pl.ds(nstart, 8), cols] * (t0 + nrows < seq).astype(F32)
    return prev8, next8


def _tri(n, upper):
    r = _iota((n, n), 0)
    c = _iota((n, n), 1)
    return ((c >= r) if upper else (c <= r)).astype(F32)


def _rmsnorm(x, w):
    ms = jnp.mean(x * x, axis=-1, keepdims=True)
    return x * lax.rsqrt(ms + NORM_EPS) * w


def _norm_inproj_kernel(x_ref, nw_ref, w_ref, o_ref, xn_ref):
    @pl.when(pl.program_id(1) == 0)
    def _():
        xn_ref[...] = _rmsnorm(x_ref[...], nw_ref[...]).astype(BF16)

    o_ref[...] = _dot(xn_ref[...], w_ref[...])


def _norm_inproj(h, norm_w, w_bf16, tm=1024, tn=512):
    t = h.shape[0]
    n = w_bf16.shape[1]
    return pl.pallas_call(
        _norm_inproj_kernel,
        grid=(t // tm, n // tn),
        in_specs=[pl.BlockSpec((tm, D_MODEL), lambda i, j: (i, 0)),
                  pl.BlockSpec((1, D_MODEL), lambda i, j: (0, 0)),
                  pl.BlockSpec((D_MODEL, tn), lambda i, j: (0, j))],
        out_specs=pl.BlockSpec((tm, tn), lambda i, j: (i, j)),
        out_shape=jax.ShapeDtypeStruct((t, n), F32),
        scratch_shapes=[pltpu.VMEM((tm, D_MODEL), BF16)],
        compiler_params=pltpu.CompilerParams(dimension_semantics=("parallel", "arbitrary")),
        name="norm_inproj",
    )(h, norm_w, w_bf16)


def _outproj_kernel(h_ref, ya_ref, yb_ref, yc_ref, yd_ref, w_ref, o_ref):
    acc = h_ref[...]
    for g, y_ref in enumerate((ya_ref, yb_ref, yc_ref, yd_ref)):
        acc = acc + _dot(y_ref[...].astype(BF16), w_ref[g * GROUP_W:(g + 1) * GROUP_W, :])
    o_ref[...] = acc


def _outproj(h, ys, w_bf16, tm=512):
    t = h.shape[0]
    yspec = pl.BlockSpec((tm, GROUP_W), lambda i: (i, 0))
    return pl.pallas_call(
        _outproj_kernel,
        grid=(t // tm,),
        in_specs=[pl.BlockSpec((tm, D_MODEL), lambda i: (i, 0)), yspec, yspec, yspec, yspec,
                  pl.BlockSpec((D_MODEL, D_MODEL), lambda i: (0, 0))],
        out_specs=pl.BlockSpec((tm, D_MODEL), lambda i: (i, 0)),
        out_shape=jax.ShapeDtypeStruct((t, D_MODEL), F32),
        compiler_params=pltpu.CompilerParams(dimension_semantics=("parallel",)),
        name="outproj",
    )(h, *ys, w_bf16)


def _mlp_kernel(h_ref, nw_ref, w1_ref, w2_ref, fw_ref, o_ref, xn_ref, *, final_norm):
    f = pl.program_id(1)

    @pl.when(f == 0)
    def _():
        hv = h_ref[...]
        xn_ref[...] = _rmsnorm(hv, nw_ref[...]).astype(BF16)
        o_ref[...] = hv

    m = _dot(xn_ref[...], w1_ref[...])
    a = jnp.square(jnp.maximum(m, 0.0))
    o_ref[...] += _dot(a.astype(BF16), w2_ref[...])

    if final_norm:
        @pl.when(f == pl.num_programs(1) - 1)
        def _():
            o_ref[...] = _rmsnorm(o_ref[...], fw_ref[...])


def _mlp(h, norm_w, w1_bf16, w2_bf16, final_w, final_norm, tm=1024, tf=512):
    t = h.shape[0]
    return pl.pallas_call(
        functools.partial(_mlp_kernel, final_norm=final_norm),
        grid=(t // tm, D_FF // tf),
        in_specs=[pl.BlockSpec((tm, D_MODEL), lambda i, f: (i, 0)),
                  pl.BlockSpec((1, D_MODEL), lambda i, f: (0, 0)),
                  pl.BlockSpec((D_MODEL, tf), lambda i, f: (0, f)),
                  pl.BlockSpec((tf, D_MODEL), lambda i, f: (f, 0)),
                  pl.BlockSpec((1, D_MODEL), lambda i, f: (0, 0))],
        out_specs=pl.BlockSpec((tm, D_MODEL), lambda i, f: (i, 0)),
        out_shape=jax.ShapeDtypeStruct((t, D_MODEL), F32),
        scratch_shapes=[pltpu.VMEM((tm, D_MODEL), BF16)],
        compiler_params=pltpu.CompilerParams(dimension_semantics=("parallel", "arbitrary")),
        name="mlp",
    )(h, norm_w, w1_bf16, w2_bf16, final_w)


def _lru_kernel(u_ref, cw_ref, cb_ref, wa_ref, ba_ref, wx_ref, bx_ref, lam_ref, o_ref,
                gate_s, a_s, b_s, *, seq):
    gcols = slice(0, GROUP_W)
    xcols = slice(GROUP_W, 2 * GROUP_W)

    def pre(i, carry):
        t0 = pl.multiple_of(i * ROWS, ROWS)
        rows = pl.ds(t0, ROWS)
        xi = u_ref[rows, xcols]
        prev8, next8 = _halo_rows(u_ref, t0, ROWS, seq, xcols)
        xf = (cw_ref[0:1, :] * _shift_down(xi, prev8, 2) + cw_ref[1:2, :] * _shift_down(xi, prev8, 1)
              + cw_ref[2:3, :] * xi + cw_ref[3:4, :] * _shift_up(xi, next8, 1) + cb_ref[...])
        gi = u_ref[rows, gcols]
        gate_s[rows, :] = 0.5 * gi * (1.0 + jnp.tanh(0.7978845608028654 * (gi + 0.044715 * gi * gi * gi)))
        xfb = xf.astype(BF16)
        for d in range(2):
            rec = _sigmoid(_dot(xfb, wa_ref[d]) + ba_ref[d:d + 1, :])
            inp = _sigmoid(_dot(xfb, wx_ref[d]) + bx_ref[d:d + 1, :])
            log_a = -LRU_C * rec * _softplus(-lam_ref[d:d + 1, :])
            a_s[d, rows, :] = jnp.exp(log_a)
            b_s[d, rows, :] = jnp.sqrt(1.0 - jnp.exp(2.0 * log_a)) * inp * xf
        return carry

    lax.fori_loop(0, seq // ROWS, pre, 0)

    row8 = _iota((8, GROUP_W), 0)

    def scan8(a, b, rev):
        for s in (1, 2, 4):
            if rev:
                keep = row8 < 8 - s
                a_sh = jnp.where(keep, pltpu.roll(a, 8 - s, 0), 1.0)
                b_sh = jnp.where(keep, pltpu.roll(b, 8 - s, 0), 0.0)
            else:
                keep = row8 >= s
                a_sh = jnp.where(keep, pltpu.roll(a, s, 0), 1.0)
                b_sh = jnp.where(keep, pltpu.roll(b, s, 0), 0.0)
            b = a * b_sh + b
            a = a * a_sh
        return a, b

    nblk = seq // 8

    def scan(i, carry):
        cf, cb = carry
        rf = pl.ds(pl.multiple_of(i * 8, 8), 8)
        rb = pl.ds(pl.multiple_of((nblk - 1 - i) * 8, 8), 8)
        af, bf = scan8(a_s[0, rf, :], b_s[0, rf, :], False)
        hf = bf + af * cf
        a_s[0, rf, :] = hf
        ab, bb = scan8(a_s[1, rb, :], b_s[1, rb, :], True)
        hb = bb + ab * cb
        a_s[1, rb, :] = hb
        return hf[7:8, :], hb[0:1, :]

    zero = jnp.zeros((1, GROUP_W), F32)
    lax.fori_loop(0, nblk, scan, (zero, zero))

    def post(i, carry):
        rows = pl.ds(pl.multiple_of(i * ROWS, ROWS), ROWS)
        o_ref[rows, :] = gate_s[rows, :] * (a_s[0, rows, :] + a_s[1, rows, :])
        return carry

    lax.fori_loop(0, seq // ROWS, post, 0)


def _lru(u, col_block, batch, seq, cw, cb, wa, ba, wx, bx, lam):
    full = lambda shape: pl.BlockSpec(shape, lambda b: (0,) * len(shape))
    return pl.pallas_call(
        functools.partial(_lru_kernel, seq=seq),
        grid=(batch,),
        in_specs=[pl.BlockSpec((seq, LRU_COLS), lambda b: (b, col_block)),
                  full((4, GROUP_W)), full((1, GROUP_W)),
                  full((2, GROUP_W, GROUP_W)), full((2, GROUP_W)),
                  full((2, GROUP_W, GROUP_W)), full((2, GROUP_W)), full((2, GROUP_W))],
        out_specs=pl.BlockSpec((seq, GROUP_W), lambda b: (b, 0)),
        out_shape=jax.ShapeDtypeStruct((batch * seq, GROUP_W), F32),
        scratch_shapes=[pltpu.VMEM((seq, GROUP_W), F32),
                        pltpu.VMEM((2, seq, GROUP_W), F32),
                        pltpu.VMEM((2, seq, GROUP_W), F32)],
        compiler_params=pltpu.CompilerParams(dimension_semantics=("parallel",),
                                             vmem_limit_bytes=V7X_VMEM_LIMIT),
        name="rglru",
    )(u, cw, cb, wa, ba, wx, bx, lam)


def _ssd_kernel(u_ref, cw_ref, cb_ref, dtb64_ref, dtb128_ref, alog64_ref, alog128_ref, e64_ref, e128_ref,
                dskip_ref, nw_ref, o_ref, xc_s, y_s, st_s, *, seq):
    zcols = slice(0, GROUP_W)
    xbc_cols = slice(GROUP_W, GROUP_W + SSD_XBC)
    dt_cols = slice(GROUP_W + SSD_XBC, SSD_COLS_PAD)

    def pre(i, carry):
        t0 = pl.multiple_of(i * ROWS, ROWS)
        rows = pl.ds(t0, ROWS)
        xi = u_ref[rows, xbc_cols]
        prev8, next8 = _halo_rows(u_ref, t0, ROWS, seq, xbc_cols)
        xc = (cw_ref[0:1, :] * _shift_down(xi, prev8, 2) + cw_ref[1:2, :] * _shift_down(xi, prev8, 1)
              + cw_ref[2:3, :] * xi + cw_ref[3:4, :] * _shift_up(xi, next8, 1) + cb_ref[...])
        xc = xc * _sigmoid(xc)
        xc_s[rows, :] = xc
        y_s[rows, :] = dskip_ref[...] * xc[:, 0:GROUP_W]
        return carry

    lax.fori_loop(0, seq // ROWS, pre, 0)

    st_s[...] = jnp.zeros_like(st_s)
    nchunk = seq // CHUNK
    rr = _iota((CHUNK, CHUNK), 0)
    cc = _iota((CHUNK, CHUNK), 1)
    hmasks = [_head_mask(h) for h in range(N_HEADS)]

    def chunk_step(d, t0):
        rows = pl.ds(t0, CHUNK)
        upper = d == 1
        tri = _tri(CHUNK, upper)
        strict = (rr < cc) if upper else (rr > cc)
        incl = (cc >= rr) if upper else (cc <= rr)
        edge_row = 0 if upper else CHUNK - 1

        xs = xc_s[rows, 0:GROUP_W]
        bm = xc_s[rows, GROUP_W:2 * GROUP_W].astype(BF16)
        cm = xc_s[rows, 2 * GROUP_W:3 * GROUP_W].astype(BF16)
        dtraw = u_ref[rows, dt_cols]
        dt64 = _softplus(_dot(dtraw, e64_ref[d], HI) + dtb64_ref[d:d + 1, :])
        dt128 = _softplus(_dot(dtraw, e128_ref[d], HI) + dtb128_ref[d:d + 1, :])
        adt64 = dt64 * (-jnp.exp(alog64_ref[d:d + 1, :]))
        adt128 = dt128 * (-jnp.exp(alog128_ref[d:d + 1, :]))
        xdt = xs * dt64
        xdt_b = xdt.astype(BF16)
        cs64 = _dot(tri, adt64, HI)

        y = jnp.zeros((CHUNK, GROUP_W), F32)
        for g in range(2):
            bg = bm[:, g * SSD_D_STATE:(g + 1) * SSD_D_STATE]
            cg = cm[:, g * SSD_D_STATE:(g + 1) * SSD_D_STATE]
            scores = _dot_t(cg, bg)
            for h in (2 * g, 2 * g + 1):
                a_cb = adt128[:, h * 128:h * 128 + CHUNK]
                seg = _dot(tri, jnp.where(strict, a_cb, 0.0), HI)
                lmat = jnp.where(incl, jnp.exp(seg), 0.0)
                y = y + _dot((scores * lmat).astype(BF16), xdt_b) * hmasks[h]
        y_off = jnp.concatenate([_dot(cm[:, g * SSD_D_STATE:(g + 1) * SSD_D_STATE], st_s[d, g].astype(BF16))
                                 for g in range(2)], axis=1)
        y = y + y_off * jnp.exp(cs64)
        y_s[rows, :] += y

        edge = cs64[edge_row:edge_row + 1, :]
        xd = (xdt * jnp.exp(edge - cs64)).astype(BF16)
        egrow = jnp.exp(edge)
        for g in range(2):
            gl = slice(g * 128, (g + 1) * 128)
            st_s[d, g] = st_s[d, g] * egrow[:, gl] + _tdot(bm[:, g * SSD_D_STATE:(g + 1) * SSD_D_STATE], xd[:, gl])

    def body(c, carry):
        chunk_step(0, pl.multiple_of(c * CHUNK, CHUNK))
        chunk_step(1, pl.multiple_of((nchunk - 1 - c) * CHUNK, CHUNK))
        return carry

    lax.fori_loop(0, nchunk, body, 0)

    def post(i, carry):
        rows = pl.ds(pl.multiple_of(i * ROWS, ROWS), ROWS)
        z = u_ref[rows, zcols]
        y = y_s[rows, :] * (z * _sigmoid(z))
        o_ref[rows, :] = _rmsnorm(y, nw_ref[...])
        return carry

    lax.fori_loop(0, seq // ROWS, post, 0)


def _ssd(u, col_block, batch, seq, cw, cb, dtb64, dtb128, alog64, alog128, e64, e128, dskip, nw):
    full = lambda shape: pl.BlockSpec(shape, lambda b: (0,) * len(shape))
    return pl.pallas_call(
        functools.partial(_ssd_kernel, seq=seq),
        grid=(batch,),
        in_specs=[pl.BlockSpec((seq, SSD_COLS_PAD), lambda b: (b, col_block)),
                  full((4, SSD_XBC)), full((1, SSD_XBC)),
                  full((2, GROUP_W)), full((2, 4 * 128)), full((2, GROUP_W)), full((2, 4 * 128)),
                  full((2, SSD_DT_PAD, GROUP_W)), full((2, SSD_DT_PAD, 4 * 128)),
                  full((1, GROUP_W)), full((1, GROUP_W))],
        out_specs=pl.BlockSpec((seq, GROUP_W), lambda b: (b, 0)),
        out_shape=jax.ShapeDtypeStruct((batch * seq, GROUP_W), F32),
        scratch_shapes=[pltpu.VMEM((seq, SSD_XBC), F32),
                        pltpu.VMEM((seq, GROUP_W), F32),
                        pltpu.VMEM((2, 2, SSD_D_STATE, 128), F32)],
        compiler_params=pltpu.CompilerParams(dimension_semantics=("parallel",),
                                             vmem_limit_bytes=V7X_VMEM_LIMIT),
        name="ssd",
    )(u, cw, cb, dtb64, dtb128, alog64, alog128, e64, e128, dskip, nw)


def _unit_lower_inverse(n_mat, eye, blk_mask):
    p = -(n_mat * blk_mask)
    t_d = eye + p
    for _ in range(3):
        p = _dot(p, p, HI)
        t_d = t_d + _dot(t_d, p, HI)
    m = -_dot(t_d, n_mat * (1.0 - blk_mask), HI)
    m2 = _dot(m, m, HI)
    t_o = eye + m
    t_o = t_o + _dot(t_o, m2, HI)
    return _dot(t_o, t_d, HI)


def _rwkv_kernel(u_ref, mu_ref, w0_ref, wup_ref, a0_ref, aup_ref, gup_ref, kk_ref, ka_ref, rk_ref,
                 gnw_ref, gnb_ref, o_ref,
                 r_s, v_s, n_s, g_s, bonus_s, lw_s, kd_s, b_s, y_s, st_s, *, seq):
    allc = slice(0, RWKV_COLS)
    lane_r = _iota((GROUP_W, GROUP_W), 0) >> 6
    lane_c = _iota((GROUP_W, GROUP_W), 1) >> 6
    blockdiag = (lane_r == lane_c).astype(F32)

    def pre(i, carry):
        t0 = pl.multiple_of(i * ROWS, ROWS)
        rows = pl.ds(t0, ROWS)
        x = u_ref[rows, allc]
        prev8, next8 = _halo_rows(u_ref, t0, ROWS, seq, allc)
        prev = _shift_down(x, prev8, 1)
        nxt = _shift_up(x, next8, 1)
        x = x + mu_ref[0:1, :] * (prev - x) + mu_ref[1:2, :] * (nxt - x)
        r = x[:, 0:GROUP_W]
        k = x[:, GROUP_W:2 * GROUP_W]
        v = x[:, 2 * GROUP_W:3 * GROUP_W]
        wd = jnp.tanh(x[:, 3 * GROUP_W:3 * GROUP_W + RWKV_RANK2]).astype(BF16)
        ad = x[:, 3 * GROUP_W + RWKV_RANK2:3 * GROUP_W + 2 * RWKV_RANK2].astype(BF16)
        gd = x[:, 3 * GROUP_W + 2 * RWKV_RANK2:RWKV_COLS]
        kk = k * kk_ref[...]
        kk = kk * lax.rsqrt(_dot(kk * kk, blockdiag, HI) + 1e-12)
        r_s[rows, :] = r
        v_s[rows, :] = v
        n_s[rows, :] = kk
        g_s[rows, :] = _dot(_sigmoid(gd).astype(BF16), gup_ref[...])
        kd_sum = jnp.zeros_like(k)
        for d in range(2):
            z_w = w0_ref[d:d + 1, :] + _dot(wd, wup_ref[d])
            lw_s[d, rows, :] = -RWKV_DECAY_SCALE * _sigmoid(z_w)
            alpha = _sigmoid(a0_ref[d:d + 1, :] + _dot(ad, aup_ref[d]))
            kd = k * (1.0 + (alpha - 1.0) * ka_ref[...])
            kd_s[d, rows, :] = kd
            b_s[d, rows, :] = alpha * kk
            kd_sum = kd_sum + kd
        bonus_s[rows, :] = _dot(r * kd_sum * rk_ref[...], blockdiag, HI) * v
        y_s[rows, :] = jnp.zeros((ROWS, GROUP_W), F32)
        return carry

    lax.fori_loop(0, seq // ROWS, pre, 0)

    st_s[...] = jnp.zeros_like(st_s)
    nchunk = seq // CHUNK
    rr = _iota((CHUNK, CHUNK), 0)
    cc = _iota((CHUNK, CHUNK), 1)
    eye = (rr == cc).astype(F32)
    blk16 = ((rr >> 4) == (cc >> 4)).astype(F32)
    hmasks = [_head_mask(h) for h in range(N_HEADS)]

    def chunk_step(d, t0):
        rows = pl.ds(t0, CHUNK)
        upper = d == 1
        tri = _tri(CHUNK, upper)
        strict = ((cc > rr) if upper else (cc < rr)).astype(F32)
        incl = ((cc >= rr) if upper else (cc <= rr)).astype(F32)
        edge_row = 0 if upper else CHUNK - 1

        lw = lw_s[d, rows, :]
        cs = _dot(tri, lw, HI)
        ginv = jnp.exp(-cs)
        kkt = n_s[rows, :] * jnp.exp(cs - lw)
        rt = r_s[rows, :] * jnp.exp(cs)
        kh = kd_s[d, rows, :] * ginv
        bh = b_s[d, rows, :] * ginv
        v = v_s[rows, :]
        s_mat = st_s[d]
        lhs = jnp.concatenate([kkt, rt], axis=0)
        p_all = _dot_t(lhs, s_mat, HI)
        p_kk = p_all[0:CHUNK]
        p_r = p_all[CHUNK:]

        a2s, a3s = [], []
        u_all = jnp.zeros((CHUNK, GROUP_W), F32)
        for h in range(N_HEADS):
            lm = lhs * hmasks[h]
            a_k = _dot_t(lm, kh, HI)
            a_b = _dot_t(lm, bh, HI)
            t_inv = _unit_lower_inverse(a_b[0:CHUNK] * strict, eye, blk16)
            rhs = p_kk + _dot(a_k[0:CHUNK] * strict, v, HI)
            u_all = u_all + _dot(t_inv, rhs, HI) * hmasks[h]
            a2s.append(a_k[CHUNK:] * incl)
            a3s.append(a_b[CHUNK:] * incl)
        y = p_r
        for h in range(N_HEADS):
            y = y + (_dot(a2s[h], v, HI) - _dot(a3s[h], u_all, HI)) * hmasks[h]
        y_s[rows, :] += y

        gl = jnp.exp(cs[edge_row:edge_row + 1, :])
        st_s[d] = (s_mat + _tdot(v, kh, HI) - _tdot(u_all, bh, HI)) * gl * blockdiag

    def body(c, carry):
        chunk_step(0, pl.multiple_of(c * CHUNK, CHUNK))
        chunk_step(1, pl.multiple_of((nchunk - 1 - c) * CHUNK, CHUNK))
        return carry

    lax.fori_loop(0, nchunk, body, 0)

    avg = blockdiag * (1.0 / HEAD_DIM)

    def post(i, carry):
        rows = pl.ds(pl.multiple_of(i * ROWS, ROWS), ROWS)
        y = y_s[rows, :]
        mean = _dot(y, avg, HI)
        yc = y - mean
        var = _dot(yc * yc, avg, HI)
        y = yc * lax.rsqrt(var + RWKV_GN_EPS) * gnw_ref[...] + gnb_ref[...]
        o_ref[rows, :] = (y + bonus_s[rows, :]) * g_s[rows, :]
        return carry

    lax.fori_loop(0, seq // ROWS, post, 0)


def _rwkv(u, col_block, batch, seq, mu, w0, wup, a0, aup, gup, k_k, k_a, r_k, gn_w, gn_b):
    full = lambda shape: pl.BlockSpec(shape, lambda b: (0,) * len(shape))
    tok = lambda: pltpu.VMEM((seq, GROUP_W), F32)
    tok2 = lambda: pltpu.VMEM((2, seq, GROUP_W), F32)
    return pl.pallas_call(
        functools.partial(_rwkv_kernel, seq=seq),
        grid=(batch,),
        in_specs=[pl.BlockSpec((seq, RWKV_COLS), lambda b: (b, col_block)),
                  full((2, RWKV_COLS)), full((2, GROUP_W)), full((2, RWKV_RANK2, GROUP_W)),
                  full((2, GROUP_W)), full((2, RWKV_RANK2, GROUP_W)), full((RWKV_RANK2, GROUP_W)),
                  full((1, GROUP_W)), full((1, GROUP_W)), full((1, GROUP_W)),
                  full((1, GROUP_W)), full((1, GROUP_W))],
        out_specs=pl.BlockSpec((seq, GROUP_W), lambda b: (b, 0)),
        out_shape=jax.ShapeDtypeStruct((batch * seq, GROUP_W), F32),
        scratch_shapes=[tok(), tok(), tok(), tok(), tok(), tok2(), tok2(), tok2(), tok(),
                        pltpu.VMEM((2, GROUP_W, GROUP_W), F32)],
        compiler_params=pltpu.CompilerParams(dimension_semantics=("parallel",),
                                             vmem_limit_bytes=V7X_VMEM_LIMIT),
        name="rwkv7",
    )(u, mu, w0, wup, a0, aup, gup, k_k, k_a, r_k, gn_w, gn_b)


def _natten_kernel(q_ref, k_ref, v_ref, bias_ref, o_ref, *, n_rows):
    r = pl.program_id(1)
    start = jnp.clip(r - NA_WIN_ROWS // 2, 0, n_rows - NA_WIN_ROWS)
    win = pl.ds(pl.multiple_of(start * GRID_W, GRID_W), NA_WIN_ROWS * GRID_W)
    kw = k_ref[win, :].astype(BF16)
    vw = v_ref[win, :].astype(BF16)
    q = q_ref[...] * (HEAD_DIM ** -0.5)
    hmasks = [_head_mask(h) for h in range(N_HEADS)]
    qs = jnp.concatenate([q * hmasks[h] for h in range(N_HEADS)], axis=0).astype(BF16)
    s = _dot_t(qs, kw) + bias_ref[0]
    m = jnp.max(s, axis=-1, keepdims=True)
    p = jnp.exp(s - m)
    p = p / jnp.sum(p, axis=-1, keepdims=True)
    o = _dot(p.astype(BF16), vw)
    out = o[0:GRID_W] * hmasks[0]
    for h in range(1, N_HEADS):
        out = out + o[h * GRID_W:(h + 1) * GRID_W] * hmasks[h]
    o_ref[...] = out


def _natten(u, q_block, batch, seq, bias_tab):
    n_rows = seq // GRID_W
    nkeys = NA_WIN_ROWS * GRID_W

    def bias_index(b, r):
        start = jnp.clip(r - NA_WIN_ROWS // 2, 0, n_rows - NA_WIN_ROWS)
        return (start - r + NA_WIN_ROWS - 1, 0, 0)

    return pl.pallas_call(
        functools.partial(_natten_kernel, n_rows=n_rows),
        grid=(batch, n_rows),
        in_specs=[pl.BlockSpec((GRID_W, GROUP_W), lambda b, r: (b * n_rows + r, q_block)),
                  pl.BlockSpec((seq, GROUP_W), lambda b, r: (b, q_block + 1)),
                  pl.BlockSpec((seq, GROUP_W), lambda b, r: (b, q_block + 2)),
                  pl.BlockSpec((1, N_HEADS * GRID_W, nkeys), bias_index)],
        out_specs=pl.BlockSpec((GRID_W, GROUP_W), lambda b, r: (b * n_rows + r, 0)),
        out_shape=jax.ShapeDtypeStruct((batch * seq, GROUP_W), F32),
        compiler_params=pltpu.CompilerParams(dimension_semantics=("parallel", "arbitrary")),
        name="natten",
    )(u, u, u, bias_tab)


def _natten_bias_table(rel_bias):
    d = np.arange(NA_WIN_ROWS)
    row_i = d[:, None] + d[None, :]
    qc = np.arange(GRID_W)
    kc = np.arange(GRID_W)
    ws = np.clip(qc - NA_WIN_COLS // 2, 0, GRID_W - NA_WIN_COLS)
    in_win = (kc[None, :] >= ws[:, None]) & (kc[None, :] < ws[:, None] + NA_WIN_COLS)
    col_i = np.clip(kc[None, :] - qc[:, None] + NA_WIN_COLS - 1, 0, 2 * NA_WIN_COLS - 2)
    tab = rel_bias[:, row_i[:, :, None, None], col_i[None, None, :, :]]
    tab = jnp.where(in_win[None, None, None], tab.astype(F32), NA_MASKED)
    return tab.transpose(1, 0, 3, 2, 4).reshape(NA_WIN_ROWS, N_HEADS * GRID_W, NA_WIN_ROWS * GRID_W)


def _pack_w_in(w_in):
    o_ssd = RWKV_COLS
    o_dt_end = o_ssd + GROUP_W + SSD_XBC + 8
    o_lru_end = o_dt_end + LRU_COLS
    pad = jnp.zeros(w_in.shape[:-1] + (SSD_DT_PAD - 8,), w_in.dtype)
    return jnp.concatenate([w_in[..., :o_dt_end], pad, w_in[..., o_lru_end:], w_in[..., o_dt_end:o_lru_end]],
                           axis=-1).astype(BF16)


def _pad_rank(w_up):
    z = jnp.zeros_like(w_up[0])
    return jnp.stack([jnp.concatenate([w_up[0], z], axis=0), jnp.concatenate([z, w_up[1]], axis=0)])


def _block_diag(w):
    _, nb, n, _ = w.shape
    eye = jnp.eye(nb, dtype=w.dtype)
    return jnp.einsum('dkij,kl->dkilj', w, eye).reshape(2, nb * n, nb * n)


def _dt_expanders():
    e64 = np.zeros((2, SSD_DT_PAD, GROUP_W), np.float32)
    e128 = np.zeros((2, SSD_DT_PAD, 4 * 128), np.float32)
    for d in range(2):
        for h in range(N_HEADS):
            e64[d, d * N_HEADS + h, h * HEAD_DIM:(h + 1) * HEAD_DIM] = 1.0
            e128[d, d * N_HEADS + h, h * 128:(h + 1) * 128] = 1.0
    return jnp.asarray(e64), jnp.asarray(e128)


def kernel(x, norm1_w, w_in, rwkv_shift_mu, rwkv_w0, rwkv_w_up, rwkv_a0, rwkv_a_up, rwkv_g_up, rwkv_k_k, rwkv_k_a, rwkv_r_k, rwkv_gn_w, rwkv_gn_b, ssd_conv_w, ssd_conv_b, ssd_dt_bias, ssd_a_log, ssd_d, ssd_norm_w, lru_conv_w, lru_conv_b, lru_gate_a_w, lru_gate_a_b, lru_gate_x_w, lru_gate_x_b, lru_lambda, na_rel_bias, w_out, norm2_w, w_mlp1, w_mlp2, final_norm_w):
    batch, seq, _ = x.shape
    depth = w_in.shape[0]
    h = x.reshape(batch * seq, D_MODEL)

    w_in_p = _pack_w_in(w_in)
    w_out_b = w_out.astype(BF16)
    w1_b = w_mlp1.astype(BF16)
    w2_b = w_mlp2.astype(BF16)
    e64, e128 = _dt_expanders()
    row = lambda a: a.reshape(1, -1)
    rep = lambda a, n: jnp.repeat(a, n, axis=-1)

    for l in range(depth):
        u = _norm_inproj(h, row(norm1_w[l]), w_in_p[l])
        y_a = _rwkv(u, 0, batch, seq, rwkv_shift_mu[l], rwkv_w0[l], _pad_rank(rwkv_w_up[l]).astype(BF16),
                    rwkv_a0[l], _pad_rank(rwkv_a_up[l]).astype(BF16), rwkv_g_up[l].astype(BF16),
                    row(rwkv_k_k[l]), row(rwkv_k_a[l]), row(rwkv_r_k[l]), row(rwkv_gn_w[l]), row(rwkv_gn_b[l]))
        y_b = _ssd(u, 1, batch, seq, ssd_conv_w[l], row(ssd_conv_b[l]),
                   rep(ssd_dt_bias[l], HEAD_DIM), rep(ssd_dt_bias[l], 128),
                   rep(ssd_a_log[l], HEAD_DIM), rep(ssd_a_log[l], 128), e64, e128,
                   row(rep(ssd_d[l], HEAD_DIM)), row(ssd_norm_w[l]))
        y_c = _lru(u, (RWKV_COLS + SSD_COLS_PAD + NA_COLS) // LRU_COLS, batch, seq,
                   lru_conv_w[l], row(lru_conv_b[l]),
                   _block_diag(lru_gate_a_w[l]).astype(BF16), lru_gate_a_b[l],
                   _block_diag(lru_gate_x_w[l]).astype(BF16), lru_gate_x_b[l], lru_lambda[l])
        y_d = _natten(u, (RWKV_COLS + SSD_COLS_PAD) // GROUP_W, batch, seq, _natten_bias_table(na_rel_bias[l]))
        h = _outproj(h, (y_a, y_b, y_c, y_d), w_out_b[l])
        h = _mlp(h, row(norm2_w[l]), w1_b[l], w2_b[l], row(final_norm_w), final_norm=(l == depth - 1))
    return h.reshape(batch, seq, D_MODEL)
```

```python
import functools

import numpy as np
import jax
import jax.numpy as jnp
from jax import lax
from jax.experimental import pallas as pl
from jax.experimental.pallas import tpu as pltpu

F32 = jnp.float32
BF16 = jnp.bfloat16
HI = lax.Precision.HIGHEST

D_MODEL = 1024
GRID_W = 64
GROUP_W = 256
HEAD_DIM = 64
N_HEADS = GROUP_W // HEAD_DIM
D_FF = 4 * D_MODEL
NORM_EPS = 1e-5

RWKV_RANK2 = 128
RWKV_DECAY_SCALE = 0.6065306597126334
RWKV_GN_EPS = 64e-5
RWKV_COLS = 3 * GROUP_W + 3 * RWKV_RANK2

SSD_D_STATE = 128
SSD_XBC = GROUP_W + 4 * SSD_D_STATE
SSD_DT_PAD = 128
SSD_COLS_PAD = GROUP_W + SSD_XBC + SSD_DT_PAD

LRU_C = 8.0
LRU_COLS = 2 * GROUP_W
NA_COLS = 3 * GROUP_W
NA_WIN_ROWS = 8
NA_WIN_COLS = 16
NA_MASKED = -1e30

U_COLS = RWKV_COLS + SSD_COLS_PAD + NA_COLS + LRU_COLS
CHUNK = 64
ROWS = 256

V7X_VMEM_LIMIT = 60 * 1024 * 1024

RWKV_PREC_GRAM = "bf16"
RWKV_PREC_APPLY = "bf16"
RWKV_PREC_INV = "bf16"
RWKV_PREC_STATE = "bf16"


def _dot(a, b, prec=None):
    return jnp.dot(a, b, preferred_element_type=F32, precision=prec)


def _dot_t(a, b, prec=None):
    return lax.dot_general(a, b, (((1,), (1,)), ((), ())), preferred_element_type=F32, precision=prec)


def _tdot(a, b, prec=None):
    return lax.dot_general(a, b, (((0,), (0,)), ((), ())), preferred_element_type=F32, precision=prec)


_NN = (((1,), (0,)), ((), ()))
_NT = (((1,), (1,)), ((), ()))
_TN = (((0,), (0,)), ((), ()))


def _mm(a, b, mode, dims=_NN):
    dg = lambda x, y: lax.dot_general(x, y, dims, preferred_element_type=F32)
    if mode == "hi":
        return lax.dot_general(a, b, dims, preferred_element_type=F32, precision=HI)
    ah = a.astype(BF16)
    bh = b.astype(BF16)
    if mode == "bf16":
        return dg(ah, bh)
    al = (a - ah.astype(F32)).astype(BF16)
    bl = (b - bh.astype(F32)).astype(BF16)
    return dg(ah, bh) + (dg(ah, bl) + dg(al, bh))


def _iota(shape, dim):
    return lax.broadcasted_iota(jnp.int32, shape, dim)


def _head_mask(h, width=GROUP_W):
    lane = _iota((1, width), 1)
    return ((lane >= h * HEAD_DIM) & (lane < (h + 1) * HEAD_DIM)).astype(F32)


def _sigmoid(x):
    return 1.0 / (1.0 + jnp.exp(-x))


def _softplus(x):
    return jnp.maximum(x, 0.0) + jnp.log(1.0 + jnp.exp(-jnp.abs(x)))


def _shift_down(x, prev8, k):
    rx = pltpu.roll(x, k, 0)
    row8 = _iota((8, x.shape[1]), 0)
    head = jnp.where(row8 < k, pltpu.roll(prev8, k, 0), rx[0:8])
    return jnp.concatenate([head, rx[8:]], axis=0)


def _shift_up(x, next8, k):
    n = x.shape[0]
    rx = pltpu.roll(x, n - k, 0)
    row8 = _iota((8, x.shape[1]), 0)
    tail = jnp.where(row8 >= 8 - k, pltpu.roll(next8, 8 - k, 0), rx[n - 8:])
    return jnp.concatenate([rx[:n - 8], tail], axis=0)


def _halo_rows(ref, t0, nrows, seq, cols):
    pstart = pl.multiple_of(jnp.maximum(t0 - 8, 0), 8)
    nstart = pl.multiple_of(jnp.minimum(t0 + nrows, seq - 8), 8)
    prev8 = ref[pl.ds(pstart, 8), cols] * (t0 > 0).astype(F32)
    next8 = ref[pl.ds(nstart, 8), cols] * (t0 + nrows < seq).astype(F32)
    return prev8, next8


def _tri(n, upper):
    r = _iota((n, n), 0)
    c = _iota((n, n), 1)
    return ((c >= r) if upper else (c <= r)).astype(F32)


def _rmsnorm(x, w):
    ms = jnp.mean(x * x, axis=-1, keepdims=True)
    return x * lax.rsqrt(ms + NORM_EPS) * w


def _norm_inproj_kernel(x_ref, nw_ref, w_ref, o_ref, xn_ref):
    @pl.when(pl.program_id(1) == 0)
    def _():
        xn_ref[...] = _rmsnorm(x_ref[...], nw_ref[...]).astype(BF16)

    o_ref[...] = _dot(xn_ref[...], w_ref[...])


def _norm_inproj(h, norm_w, w_bf16, tm=1024, tn=512):
    t = h.shape[0]
    n = w_bf16.shape[1]
    return pl.pallas_call(
        _norm_inproj_kernel,
        grid=(t // tm, n // tn),
        in_specs=[pl.BlockSpec((tm, D_MODEL), lambda i, j: (i, 0)),
                  pl.BlockSpec((1, D_MODEL), lambda i, j: (0, 0)),
                  pl.BlockSpec((D_MODEL, tn), lambda i, j: (0, j))],
        out_specs=pl.BlockSpec((tm, tn), lambda i, j: (i, j)),
        out_shape=jax.ShapeDtypeStruct((t, n), F32),
        scratch_shapes=[pltpu.VMEM((tm, D_MODEL), BF16)],
        compiler_params=pltpu.CompilerParams(dimension_semantics=("parallel", "arbitrary")),
        name="norm_inproj",
    )(h, norm_w, w_bf16)


def _outproj_kernel(h_ref, ya_ref, yb_ref, yc_ref, yd_ref, w_ref, o_ref):
    acc = h_ref[...]
    for g, y_ref in enumerate((ya_ref, yb_ref, yc_ref, yd_ref)):
        acc = acc + _dot(y_ref[...].astype(BF16), w_ref[g * GROUP_W:(g + 1) * GROUP_W, :])
    o_ref[...] = acc


def _outproj(h, ys, w_bf16, tm=512):
    t = h.shape[0]
    yspec = pl.BlockSpec((tm, GROUP_W), lambda i: (i, 0))
    return pl.pallas_call(
        _outproj_kernel,
        grid=(t // tm,),
        in_specs=[pl.BlockSpec((tm, D_MODEL), lambda i: (i, 0)), yspec, yspec, yspec, yspec,
                  pl.BlockSpec((D_MODEL, D_MODEL), lambda i: (0, 0))],
        out_specs=pl.BlockSpec((tm, D_MODEL), lambda i: (i, 0)),
        out_shape=jax.ShapeDtypeStruct((t, D_MODEL), F32),
        compiler_params=pltpu.CompilerParams(dimension_semantics=("parallel",)),
        name="outproj",
    )(h, *ys, w_bf16)


def _mlp_kernel(h_ref, nw_ref, w1_ref, w2_ref, fw_ref, o_ref, xn_ref, *, final_norm):
    f = pl.program_id(1)

    @pl.when(f == 0)
    def _():
        hv = h_ref[...]
        xn_ref[...] = _rmsnorm(hv, nw_ref[...]).astype(BF16)
        o_ref[...] = hv

    m = _dot(xn_ref[...], w1_ref[...])
    a = jnp.square(jnp.maximum(m, 0.0))
    o_ref[...] += _dot(a.astype(BF16), w2_ref[...])

    if final_norm:
        @pl.when(f == pl.num_programs(1) - 1)
        def _():
            o_ref[...] = _rmsnorm(o_ref[...], fw_ref[...])


def _mlp(h, norm_w, w1_bf16, w2_bf16, final_w, final_norm, tm=1024, tf=512):
    t = h.shape[0]
    return pl.pallas_call(
        functools.partial(_mlp_kernel, final_norm=final_norm),
        grid=(t // tm, D_FF // tf),
        in_specs=[pl.BlockSpec((tm, D_MODEL), lambda i, f: (i, 0)),
                  pl.BlockSpec((1, D_MODEL), lambda i, f: (0, 0)),
                  pl.BlockSpec((D_MODEL, tf), lambda i, f: (0, f)),
                  pl.BlockSpec((tf, D_MODEL), lambda i, f: (f, 0)),
                  pl.BlockSpec((1, D_MODEL), lambda i, f: (0, 0))],
        out_specs=pl.BlockSpec((tm, D_MODEL), lambda i, f: (i, 0)),
        out_shape=jax.ShapeDtypeStruct((t, D_MODEL), F32),
        scratch_shapes=[pltpu.VMEM((tm, D_MODEL), BF16)],
        compiler_params=pltpu.CompilerParams(dimension_semantics=("parallel", "arbitrary")),
        name="mlp",
    )(h, norm_w, w1_bf16, w2_bf16, final_w)


def _lru_kernel(u_ref, cw_ref, cb_ref, wa_ref, ba_ref, wx_ref, bx_ref, lam_ref, o_ref,
                gate_s, a_s, b_s, *, seq):
    gcols = slice(0, GROUP_W)
    xcols = slice(GROUP_W, 2 * GROUP_W)

    def pre(i, carry):
        t0 = pl.multiple_of(i * ROWS, ROWS)
        rows = pl.ds(t0, ROWS)
        xi = u_ref[rows, xcols]
        prev8, next8 = _halo_rows(u_ref, t0, ROWS, seq, xcols)
        xf = (cw_ref[0:1, :] * _shift_down(xi, prev8, 2) + cw_ref[1:2, :] * _shift_down(xi, prev8, 1)
              + cw_ref[2:3, :] * xi + cw_ref[3:4, :] * _shift_up(xi, next8, 1) + cb_ref[...])
        gi = u_ref[rows, gcols]
        gate_s[rows, :] = 0.5 * gi * (1.0 + jnp.tanh(0.7978845608028654 * (gi + 0.044715 * gi * gi * gi)))
        xfb = xf.astype(BF16)
        for d in range(2):
            rec = _sigmoid(_dot(xfb, wa_ref[d]) + ba_ref[d:d + 1, :])
            inp = _sigmoid(_dot(xfb, wx_ref[d]) + bx_ref[d:d + 1, :])
            log_a = -LRU_C * rec * _softplus(-lam_ref[d:d + 1, :])
            a_s[d, rows, :] = jnp.exp(log_a)
            b_s[d, rows, :] = jnp.sqrt(1.0 - jnp.exp(2.0 * log_a)) * inp * xf
        return carry

    lax.fori_loop(0, seq // ROWS, pre, 0)

    row8 = _iota((8, GROUP_W), 0)

    def scan8(a, b, rev):
        for s in (1, 2, 4):
            if rev:
                keep = row8 < 8 - s
                a_sh = jnp.where(keep, pltpu.roll(a, 8 - s, 0), 1.0)
                b_sh = jnp.where(keep, pltpu.roll(b, 8 - s, 0), 0.0)
            else:
                keep = row8 >= s
                a_sh = jnp.where(keep, pltpu.roll(a, s, 0), 1.0)
                b_sh = jnp.where(keep, pltpu.roll(b, s, 0), 0.0)
            b = a * b_sh + b
            a = a * a_sh
        return a, b

    nblk = seq // 8

    def scan(i, carry):
        cf, cb = carry
        rf = pl.ds(pl.multiple_of(i * 8, 8), 8)
        rb = pl.ds(pl.multiple_of((nblk - 1 - i) * 8, 8), 8)
        af, bf = scan8(a_s[0, rf, :], b_s[0, rf, :], False)
        hf = bf + af * cf
        a_s[0, rf, :] = hf
        ab, bb = scan8(a_s[1, rb, :], b_s[1, rb, :], True)
        hb = bb + ab * cb
        a_s[1, rb, :] = hb
        return hf[7:8, :], hb[0:1, :]

    zero = jnp.zeros((1, GROUP_W), F32)
    lax.fori_loop(0, nblk, scan, (zero, zero))

    def post(i, carry):
        rows = pl.ds(pl.multiple_of(i * ROWS, ROWS), ROWS)
        o_ref[rows, :] = gate_s[rows, :] * (a_s[0, rows, :] + a_s[1, rows, :])
        return carry

    lax.fori_loop(0, seq // ROWS, post, 0)


def _lru(u, col_block, batch, seq, cw, cb, wa, ba, wx, bx, lam):
    full = lambda shape: pl.BlockSpec(shape, lambda b: (0,) * len(shape))
    return pl.pallas_call(
        functools.partial(_lru_kernel, seq=seq),
        grid=(batch,),
        in_specs=[pl.BlockSpec((seq, LRU_COLS), lambda b: (b, col_block)),
                  full((4, GROUP_W)), full((1, GROUP_W)),
                  full((2, GROUP_W, GROUP_W)), full((2, GROUP_W)),
                  full((2, GROUP_W, GROUP_W)), full((2, GROUP_W)), full((2, GROUP_W))],
        out_specs=pl.BlockSpec((seq, GROUP_W), lambda b: (b, 0)),
        out_shape=jax.ShapeDtypeStruct((batch * seq, GROUP_W), F32),
        scratch_shapes=[pltpu.VMEM((seq, GROUP_W), F32),
                        pltpu.VMEM((2, seq, GROUP_W), F32),
                        pltpu.VMEM((2, seq, GROUP_W), F32)],
        compiler_params=pltpu.CompilerParams(dimension_semantics=("parallel",),
                                             vmem_limit_bytes=V7X_VMEM_LIMIT),
        name="rglru",
    )(u, cw, cb, wa, ba, wx, bx, lam)


def _ssd_kernel(u_ref, cw_ref, cb_ref, dtb64_ref, dtb128_ref, alog64_ref, alog128_ref, e64_ref, e128_ref,
                dskip_ref, nw_ref, o_ref, xc_s, y_s, st_s, *, seq):
    zcols = slice(0, GROUP_W)
    xbc_cols = slice(GROUP_W, GROUP_W + SSD_XBC)
    dt_cols = slice(GROUP_W + SSD_XBC, SSD_COLS_PAD)

    def pre(i, carry):
        t0 = pl.multiple_of(i * ROWS, ROWS)
        rows = pl.ds(t0, ROWS)
        xi = u_ref[rows, xbc_cols]
        prev8, next8 = _halo_rows(u_ref, t0, ROWS, seq, xbc_cols)
        xc = (cw_ref[0:1, :] * _shift_down(xi, prev8, 2) + cw_ref[1:2, :] * _shift_down(xi, prev8, 1)
              + cw_ref[2:3, :] * xi + cw_ref[3:4, :] * _shift_up(xi, next8, 1) + cb_ref[...])
        xc = xc * _sigmoid(xc)
        xc_s[rows, :] = xc
        y_s[rows, :] = dskip_ref[...] * xc[:, 0:GROUP_W]
        return carry

    lax.fori_loop(0, seq // ROWS, pre, 0)

    st_s[...] = jnp.zeros_like(st_s)
    nchunk = seq // CHUNK
    rr = _iota((CHUNK, CHUNK), 0)
    cc = _iota((CHUNK, CHUNK), 1)
    hmasks = [_head_mask(h) for h in range(N_HEADS)]

    def chunk_step(d, t0):
        rows = pl.ds(t0, CHUNK)
        upper = d == 1
        tri = _tri(CHUNK, upper)
        strict = (rr < cc) if upper else (rr > cc)
        incl = (cc >= rr) if upper else (cc <= rr)
        edge_row = 0 if upper else CHUNK - 1

        xs = xc_s[rows, 0:GROUP_W]
        bm = xc_s[rows, GROUP_W:2 * GROUP_W].astype(BF16)
        cm = xc_s[rows, 2 * GROUP_W:3 * GROUP_W].astype(BF16)
        dtraw = u_ref[rows, dt_cols]
        dt64 = _softplus(_dot(dtraw, e64_ref[d], HI) + dtb64_ref[d:d + 1, :])
        dt128 = _softplus(_dot(dtraw, e128_ref[d], HI) + dtb128_ref[d:d + 1, :])
        adt64 = dt64 * (-jnp.exp(alog64_ref[d:d + 1, :]))
        adt128 = dt128 * (-jnp.exp(alog128_ref[d:d + 1, :]))
        xdt = xs * dt64
        xdt_b = xdt.astype(BF16)
        cs64 = _dot(tri, adt64, HI)

        y = jnp.zeros((CHUNK, GROUP_W), F32)
        for g in range(2):
            bg = bm[:, g * SSD_D_STATE:(g + 1) * SSD_D_STATE]
            cg = cm[:, g * SSD_D_STATE:(g + 1) * SSD_D_STATE]
            scores = _dot_t(cg, bg)
            for h in (2 * g, 2 * g + 1):
                a_cb = adt128[:, h * 128:h * 128 + CHUNK]
                seg = _dot(tri, jnp.where(strict, a_cb, 0.0), HI)
                lmat = jnp.where(incl, jnp.exp(seg), 0.0)
                y = y + _dot((scores * lmat).astype(BF16), xdt_b) * hmasks[h]
        y_off = jnp.concatenate([_dot(cm[:, g * SSD_D_STATE:(g + 1) * SSD_D_STATE], st_s[d, g].astype(BF16))
                                 for g in range(2)], axis=1)
        y = y + y_off * jnp.exp(cs64)
        y_s[rows, :] += y

        edge = cs64[edge_row:edge_row + 1, :]
        xd = (xdt * jnp.exp(edge - cs64)).astype(BF16)
        egrow = jnp.exp(edge)
        for g in range(2):
            gl = slice(g * 128, (g + 1) * 128)
            st_s[d, g] = st_s[d, g] * egrow[:, gl] + _tdot(bm[:, g * SSD_D_STATE:(g + 1) * SSD_D_STATE], xd[:, gl])

    def body(c, carry):
        chunk_step(0, pl.multiple_of(c * CHUNK, CHUNK))
        chunk_step(1, pl.multiple_of((nchunk - 1 - c) * CHUNK, CHUNK))
        return carry

    lax.fori_loop(0, nchunk, body, 0)

    def post(i, carry):
        rows = pl.ds(pl.multiple_of(i * ROWS, ROWS), ROWS)
        z = u_ref[rows, zcols]
        y = y_s[rows, :] * (z * _sigmoid(z))
        o_ref[rows, :] = _rmsnorm(y, nw_ref[...])
        return carry

    lax.fori_loop(0, seq // ROWS, post, 0)


def _ssd(u, col_block, batch, seq, cw, cb, dtb64, dtb128, alog64, alog128, e64, e128, dskip, nw):
    full = lambda shape: pl.BlockSpec(shape, lambda b: (0,) * len(shape))
    return pl.pallas_call(
        functools.partial(_ssd_kernel, seq=seq),
        grid=(batch,),
        in_specs=[pl.BlockSpec((seq, SSD_COLS_PAD), lambda b: (b, col_block)),
                  full((4, SSD_XBC)), full((1, SSD_XBC)),
                  full((2, GROUP_W)), full((2, 4 * 128)), full((2, GROUP_W)), full((2, 4 * 128)),
                  full((2, SSD_DT_PAD, GROUP_W)), full((2, SSD_DT_PAD, 4 * 128)),
                  full((1, GROUP_W)), full((1, GROUP_W))],
        out_specs=pl.BlockSpec((seq, GROUP_W), lambda b: (b, 0)),
        out_shape=jax.ShapeDtypeStruct((batch * seq, GROUP_W), F32),
        scratch_shapes=[pltpu.VMEM((seq, SSD_XBC), F32),
                        pltpu.VMEM((seq, GROUP_W), F32),
                        pltpu.VMEM((2, 2, SSD_D_STATE, 128), F32)],
        compiler_params=pltpu.CompilerParams(dimension_semantics=("parallel",),
                                             vmem_limit_bytes=V7X_VMEM_LIMIT),
        name="ssd",
    )(u, cw, cb, dtb64, dtb128, alog64, alog128, e64, e128, dskip, nw)


def _unit_lower_inverses(n_mats, eye, blk_mask):
    mm = lambda a, b: _mm(a, b, RWKV_PREC_INV)
    ps = [-(n * blk_mask) for n in n_mats]
    tds = [eye + p for p in ps]
    for _ in range(3):
        ps = [mm(p, p) for p in ps]
        tds = [t + mm(t, p) for t, p in zip(tds, ps)]
    ms = [-mm(t, n * (1.0 - blk_mask)) for t, n in zip(tds, n_mats)]
    m2s = [mm(m, m) for m in ms]
    tos = [eye + m for m in ms]
    tos = [t + mm(t, m2) for t, m2 in zip(tos, m2s)]
    return [mm(to, td) for to, td in zip(tos, tds)]


def _rwkv_kernel(u_ref, mu_ref, w0_ref, wup_ref, a0_ref, aup_ref, gup_ref, kk_ref, ka_ref, rk_ref,
                 gnw_ref, gnb_ref, o_ref,
                 r_s, v_s, n_s, g_s, bonus_s, lw_s, kd_s, b_s, y_s, st_s, *, seq):
    allc = slice(0, RWKV_COLS)
    lane_r = _iota((GROUP_W, GROUP_W), 0) >> 6
    lane_c = _iota((GROUP_W, GROUP_W), 1) >> 6
    blockdiag = (lane_r == lane_c).astype(F32)

    def pre(i, carry):
        t0 = pl.multiple_of(i * ROWS, ROWS)
        rows = pl.ds(t0, ROWS)
        x = u_ref[rows, allc]
        prev8, next8 = _halo_rows(u_ref, t0, ROWS, seq, allc)
        prev = _shift_down(x, prev8, 1)
        nxt = _shift_up(x, next8, 1)
        x = x + mu_ref[0:1, :] * (prev - x) + mu_ref[1:2, :] * (nxt - x)
        r = x[:, 0:GROUP_W]
        k = x[:, GROUP_W:2 * GROUP_W]
        v = x[:, 2 * GROUP_W:3 * GROUP_W]
        wd = jnp.tanh(x[:, 3 * GROUP_W:3 * GROUP_W + RWKV_RANK2]).astype(BF16)
        ad = x[:, 3 * GROUP_W + RWKV_RANK2:3 * GROUP_W + 2 * RWKV_RANK2].astype(BF16)
        gd = x[:, 3 * GROUP_W + 2 * RWKV_RANK2:RWKV_COLS]
        kk = k * kk_ref[...]
        kk = kk * lax.rsqrt(_dot(kk * kk, blockdiag, HI) + 1e-12)
        r_s[rows, :] = r
        v_s[rows, :] = v
        n_s[rows, :] = kk
        g_s[rows, :] = _dot(_sigmoid(gd).astype(BF16), gup_ref[...])
        kd_sum = jnp.zeros_like(k)
        for d in range(2):
            z_w = w0_ref[d:d + 1, :] + _dot(wd, wup_ref[d])
            lw_s[d, rows, :] = -RWKV_DECAY_SCALE * _sigmoid(z_w)
            alpha = _sigmoid(a0_ref[d:d + 1, :] + _dot(ad, aup_ref[d]))
            kd = k * (1.0 + (alpha - 1.0) * ka_ref[...])
            kd_s[d, rows, :] = kd
            b_s[d, rows, :] = alpha * kk
            kd_sum = kd_sum + kd
        bonus_s[rows, :] = _dot(r * kd_sum * rk_ref[...], blockdiag, HI) * v
        y_s[rows, :] = jnp.zeros((ROWS, GROUP_W), F32)
        return carry

    lax.fori_loop(0, seq // ROWS, pre, 0)

    st_s[...] = jnp.zeros_like(st_s)
    nchunk = seq // CHUNK
    rr = _iota((CHUNK, CHUNK), 0)
    cc = _iota((CHUNK, CHUNK), 1)
    eye = (rr == cc).astype(F32)
    blk16 = ((rr >> 4) == (cc >> 4)).astype(F32)
    hmasks = [_head_mask(h) for h in range(N_HEADS)]

    tris = [_tri(CHUNK, False), _tri(CHUNK, True)]
    stricts = [(cc < rr).astype(F32), (cc > rr).astype(F32)]
    incls = [(cc <= rr).astype(F32), (cc >= rr).astype(F32)]
    edge_rows = [CHUNK - 1, 0]
    dirs = (0, 1)
    heads = range(N_HEADS)

    def chunk_pair(t0s):
        rows = [pl.ds(t0, CHUNK) for t0 in t0s]
        lw = [lw_s[d, rows[d], :] for d in dirs]
        cs = [_dot(tris[d], lw[d], HI) for d in dirs]
        ginv = [jnp.exp(-cs[d]) for d in dirs]
        kkt = [n_s[rows[d], :] * jnp.exp(cs[d] - lw[d]) for d in dirs]
        rt = [r_s[rows[d], :] * jnp.exp(cs[d]) for d in dirs]
        kh = [kd_s[d, rows[d], :] * ginv[d] for d in dirs]
        bh = [b_s[d, rows[d], :] * ginv[d] for d in dirs]
        v = [v_s[rows[d], :] for d in dirs]
        s_mat = [st_s[d] for d in dirs]
        lhs = [jnp.concatenate([kkt[d], rt[d]], axis=0) for d in dirs]
        p_all = [_mm(lhs[d], s_mat[d], RWKV_PREC_STATE, _NT) for d in dirs]

        lm = [[lhs[d] * hmasks[h] for h in heads] for d in dirs]
        a_k = [[_mm(lm[d][h], kh[d], RWKV_PREC_GRAM, _NT) for h in heads] for d in dirs]
        a_b = [[_mm(lm[d][h], bh[d], RWKV_PREC_GRAM, _NT) for h in heads] for d in dirs]
        t_inv = _unit_lower_inverses([a_b[d][h][0:CHUNK] * stricts[d] for d in dirs for h in heads], eye, blk16)
        rhs = [[p_all[d][0:CHUNK] + _mm(a_k[d][h][0:CHUNK] * stricts[d], v[d], RWKV_PREC_APPLY)
                for h in heads] for d in dirs]
        u_h = [[_mm(t_inv[d * N_HEADS + h], rhs[d][h], RWKV_PREC_APPLY) for h in heads] for d in dirs]
        u_all = [sum(u_h[d][h] * hmasks[h] for h in heads) for d in dirs]
        y_h = [[_mm(a_k[d][h][CHUNK:] * incls[d], v[d], RWKV_PREC_APPLY)
                - _mm(a_b[d][h][CHUNK:] * incls[d], u_all[d], RWKV_PREC_APPLY) for h in heads] for d in dirs]
        upd = [_mm(v[d], kh[d], RWKV_PREC_STATE, _TN) - _mm(u_all[d], bh[d], RWKV_PREC_STATE, _TN) for d in dirs]
        for d in dirs:
            y_s[rows[d], :] += p_all[d][CHUNK:] + sum(y_h[d][h] * hmasks[h] for h in heads)
            gl = jnp.exp(cs[d][edge_rows[d]:edge_rows[d] + 1, :])
            st_s[d] = (s_mat[d] + upd[d]) * gl * blockdiag

    def body(c, carry):
        chunk_pair([pl.multiple_of(c * CHUNK, CHUNK), pl.multiple_of((nchunk - 1 - c) * CHUNK, CHUNK)])
        return carry

    lax.fori_loop(0, nchunk, body, 0)

    avg = blockdiag * (1.0 / HEAD_DIM)

    def post(i, carry):
        rows = pl.ds(pl.multiple_of(i * ROWS, ROWS), ROWS)
        y = y_s[rows, :]
        mean = _dot(y, avg, HI)
        yc = y - mean
        var = _dot(yc * yc, avg, HI)
        y = yc * lax.rsqrt(var + RWKV_GN_EPS) * gnw_ref[...] + gnb_ref[...]
        o_ref[rows, :] = (y + bonus_s[rows, :]) * g_s[rows, :]
        return carry

    lax.fori_loop(0, seq // ROWS, post, 0)


def _rwkv(u, col_block, batch, seq, mu, w0, wup, a0, aup, gup, k_k, k_a, r_k, gn_w, gn_b):
    full = lambda shape: pl.BlockSpec(shape, lambda b: (0,) * len(shape))
    tok = lambda: pltpu.VMEM((seq, GROUP_W), F32)
    tok2 = lambda: pltpu.VMEM((2, seq, GROUP_W), F32)
    return pl.pallas_call(
        functools.partial(_rwkv_kernel, seq=seq),
        grid=(batch,),
        in_specs=[pl.BlockSpec((seq, RWKV_COLS), lambda b: (b, col_block)),
                  full((2, RWKV_COLS)), full((2, GROUP_W)), full((2, RWKV_RANK2, GROUP_W)),
                  full((2, GROUP_W)), full((2, RWKV_RANK2, GROUP_W)), full((RWKV_RANK2, GROUP_W)),
                  full((1, GROUP_W)), full((1, GROUP_W)), full((1, GROUP_W)),
                  full((1, GROUP_W)), full((1, GROUP_W))],
        out_specs=pl.BlockSpec((seq, GROUP_W), lambda b: (b, 0)),
        out_shape=jax.ShapeDtypeStruct((batch * seq, GROUP_W), F32),
        scratch_shapes=[tok(), tok(), tok(), tok(), tok(), tok2(), tok2(), tok2(), tok(),
                        pltpu.VMEM((2, GROUP_W, GROUP_W), F32)],
        compiler_params=pltpu.CompilerParams(dimension_semantics=("parallel",),
                                             vmem_limit_bytes=V7X_VMEM_LIMIT),
        name="rwkv7",
    )(u, mu, w0, wup, a0, aup, gup, k_k, k_a, r_k, gn_w, gn_b)


def _natten_kernel(q_ref, k_ref, v_ref, bias_ref, o_ref, *, n_rows):
    r = pl.program_id(1)
    start = jnp.clip(r - NA_WIN_ROWS // 2, 0, n_rows - NA_WIN_ROWS)
    win = pl.ds(pl.multiple_of(start * GRID_W, GRID_W), NA_WIN_ROWS * GRID_W)
    kw = k_ref[win, :].astype(BF16)
    vw = v_ref[win, :].astype(BF16)
    q = q_ref[...] * (HEAD_DIM ** -0.5)
    hmasks = [_head_mask(h) for h in range(N_HEADS)]
    qs = jnp.concatenate([q * hmasks[h] for h in range(N_HEADS)], axis=0).astype(BF16)
    s = _dot_t(qs, kw) + bias_ref[0]
    m = jnp.max(s, axis=-1, keepdims=True)
    p = jnp.exp(s - m)
    p = p / jnp.sum(p, axis=-1, keepdims=True)
    o = _dot(p.astype(BF16), vw)
    out = o[0:GRID_W] * hmasks[0]
    for h in range(1, N_HEADS):
        out = out + o[h * GRID_W:(h + 1) * GRID_W] * hmasks[h]
    o_ref[...] = out


def _natten(u, q_block, batch, seq, bias_tab):
    n_rows = seq // GRID_W
    nkeys = NA_WIN_ROWS * GRID_W

    def bias_index(b, r):
        start = jnp.clip(r - NA_WIN_ROWS // 2, 0, n_rows - NA_WIN_ROWS)
        return (start - r + NA_WIN_ROWS - 1, 0, 0)

    return pl.pallas_call(
        functools.partial(_natten_kernel, n_rows=n_rows),
        grid=(batch, n_rows),
        in_specs=[pl.BlockSpec((GRID_W, GROUP_W), lambda b, r: (b * n_rows + r, q_block)),
                  pl.BlockSpec((seq, GROUP_W), lambda b, r: (b, q_block + 1)),
                  pl.BlockSpec((seq, GROUP_W), lambda b, r: (b, q_block + 2)),
                  pl.BlockSpec((1, N_HEADS * GRID_W, nkeys), bias_index)],
        out_specs=pl.BlockSpec((GRID_W, GROUP_W), lambda b, r: (b * n_rows + r, 0)),
        out_shape=jax.ShapeDtypeStruct((batch * seq, GROUP_W), F32),
        compiler_params=pltpu.CompilerParams(dimension_semantics=("parallel", "arbitrary")),
        name="natten",
    )(u, u, u, bias_tab)


def _natten_bias_table(rel_bias):
    ncol = 2 * NA_WIN_COLS - 1
    qc = np.arange(GRID_W)
    kc = np.arange(GRID_W)
    ws = np.clip(qc - NA_WIN_COLS // 2, 0, GRID_W - NA_WIN_COLS)
    in_win = (kc[None, :] >= ws[:, None]) & (kc[None, :] < ws[:, None] + NA_WIN_COLS)
    col_i = np.clip(kc[None, :] - qc[:, None] + NA_WIN_COLS - 1, 0, ncol - 1)
    onehot = ((col_i[None] == np.arange(ncol)[:, None, None]) & in_win[None]).astype(np.float32)
    mask_add = np.where(in_win, 0.0, NA_MASKED).astype(np.float32)
    toep = jnp.einsum('hrc,cqk->hrqk', rel_bias.astype(F32), jnp.asarray(onehot), precision=HI) + mask_add
    tab = jnp.stack([toep[:, d:d + NA_WIN_ROWS] for d in range(NA_WIN_ROWS)])
    return tab.transpose(0, 1, 3, 2, 4).reshape(NA_WIN_ROWS, N_HEADS * GRID_W, NA_WIN_ROWS * GRID_W)


def _pack_w_in(w_in):
    o_ssd = RWKV_COLS
    o_dt_end = o_ssd + GROUP_W + SSD_XBC + 8
    o_lru_end = o_dt_end + LRU_COLS
    pad = jnp.zeros(w_in.shape[:-1] + (SSD_DT_PAD - 8,), w_in.dtype)
    return jnp.concatenate([w_in[..., :o_dt_end], pad, w_in[..., o_lru_end:], w_in[..., o_dt_end:o_lru_end]],
                           axis=-1).astype(BF16)


def _pad_rank(w_up):
    z = jnp.zeros_like(w_up[0])
    return jnp.stack([jnp.concatenate([w_up[0], z], axis=0), jnp.concatenate([z, w_up[1]], axis=0)])


def _block_diag(w):
    _, nb, n, _ = w.shape
    eye = jnp.eye(nb, dtype=w.dtype)
    return jnp.einsum('dkij,kl->dkilj', w, eye).reshape(2, nb * n, nb * n)


def _dt_expanders():
    e64 = np.zeros((2, SSD_DT_PAD, GROUP_W), np.float32)
    e128 = np.zeros((2, SSD_DT_PAD, 4 * 128), np.float32)
    for d in range(2):
        for h in range(N_HEADS):
            e64[d, d * N_HEADS + h, h * HEAD_DIM:(h + 1) * HEAD_DIM] = 1.0
            e128[d, d * N_HEADS + h, h * 128:(h + 1) * 128] = 1.0
    return jnp.asarray(e64), jnp.asarray(e128)


def kernel(x, norm1_w, w_in, rwkv_shift_mu, rwkv_w0, rwkv_w_up, rwkv_a0, rwkv_a_up, rwkv_g_up, rwkv_k_k, rwkv_k_a, rwkv_r_k, rwkv_gn_w, rwkv_gn_b, ssd_conv_w, ssd_conv_b, ssd_dt_bias, ssd_a_log, ssd_d, ssd_norm_w, lru_conv_w, lru_conv_b, lru_gate_a_w, lru_gate_a_b, lru_gate_x_w, lru_gate_x_b, lru_lambda, na_rel_bias, w_out, norm2_w, w_mlp1, w_mlp2, final_norm_w):
    batch, seq, _ = x.shape
    depth = w_in.shape[0]
    h = x.reshape(batch * seq, D_MODEL)

    w_in_p = _pack_w_in(w_in)
    w_out_b = w_out.astype(BF16)
    w1_b = w_mlp1.astype(BF16)
    w2_b = w_mlp2.astype(BF16)
    e64, e128 = _dt_expanders()
    row = lambda a: a.reshape(1, -1)
    rep = lambda a, n: jnp.repeat(a, n, axis=-1)

    for l in range(depth):
        u = _norm_inproj(h, row(norm1_w[l]), w_in_p[l])
        y_a = _rwkv(u, 0, batch, seq, rwkv_shift_mu[l], rwkv_w0[l], _pad_rank(rwkv_w_up[l]).astype(BF16),
                    rwkv_a0[l], _pad_rank(rwkv_a_up[l]).astype(BF16), rwkv_g_up[l].astype(BF16),
                    row(rwkv_k_k[l]), row(rwkv_k_a[l]), row(rwkv_r_k[l]), row(rwkv_gn_w[l]), row(rwkv_gn_b[l]))
        y_b = _ssd(u, 1, batch, seq, ssd_conv_w[l], row(ssd_conv_b[l]),
                   rep(ssd_dt_bias[l], HEAD_DIM), rep(ssd_dt_bias[l], 128),
                   rep(ssd_a_log[l], HEAD_DIM), rep(ssd_a_log[l], 128), e64, e128,
                   row(rep(ssd_d[l], HEAD_DIM)), row(ssd_norm_w[l]))
        y_c = _lru(u, (RWKV_COLS + SSD_COLS_PAD + NA_COLS) // LRU_COLS, batch, seq,
                   lru_conv_w[l], row(lru_conv_b[l]),
                   _block_diag(lru_gate_a_w[l]).astype(BF16), lru_gate_a_b[l],
                   _block_diag(lru_gate_x_w[l]).astype(BF16), lru_gate_x_b[l], lru_lambda[l])
        y_d = _natten(u, (RWKV_COLS + SSD_COLS_PAD) // GROUP_W, batch, seq, _natten_bias_table(na_rel_bias[l]))
        h = _outproj(h, (y_a, y_b, y_c, y_d), w_out_b[l])
        h = _mlp(h, row(norm2_w[l]), w1_b[l], w2_b[l], row(final_norm_w), final_norm=(l == depth - 1))
    return h.reshape(batch, seq, D_MODEL)
```

```python
import functools

import numpy as np
import jax
import jax.numpy as jnp
from jax import lax
from jax.experimental import pallas as pl
from jax.experimental.pallas import tpu as pltpu

F32 = jnp.float32
BF16 = jnp.bfloat16
HI = lax.Precision.HIGHEST

D_MODEL = 1024
GRID_W = 64
GROUP_W = 256
HEAD_DIM = 64
N_HEADS = GROUP_W // HEAD_DIM
D_FF = 4 * D_MODEL
NORM_EPS = 1e-5

RWKV_RANK2 = 128
RWKV_DECAY_SCALE = 0.6065306597126334
RWKV_GN_EPS = 64e-5
RWKV_COLS = 3 * GROUP_W + 3 * RWKV_RANK2

SSD_D_STATE = 128
SSD_XBC = GROUP_W + 4 * SSD_D_STATE
SSD_DT_PAD = 128
SSD_COLS_PAD = GROUP_W + SSD_XBC + SSD_DT_PAD

LRU_C = 8.0
LRU_COLS = 2 * GROUP_W
NA_COLS = 3 * GROUP_W
NA_WIN_ROWS = 8
NA_WIN_COLS = 16
NA_MASKED = -1e30
NA_ROWS_PER_STEP = 4

U_COLS = RWKV_COLS + SSD_COLS_PAD + NA_COLS + LRU_COLS
CHUNK = 64
RWKV_GROUP = 4
SSD_CHUNK = 256
ROWS = 256

V7X_VMEM_LIMIT = 60 * 1024 * 1024

RWKV_PREC_GRAM = "bf16"
RWKV_PREC_APPLY = "bf16"
RWKV_PREC_INV = "bf16"
RWKV_PREC_STATE = "bf16"


def _dot(a, b, prec=None):
    return jnp.dot(a, b, preferred_element_type=F32, precision=prec)


def _dot_t(a, b, prec=None):
    return lax.dot_general(a, b, (((1,), (1,)), ((), ())), preferred_element_type=F32, precision=prec)


def _tdot(a, b, prec=None):
    return lax.dot_general(a, b, (((0,), (0,)), ((), ())), preferred_element_type=F32, precision=prec)


_NN = (((1,), (0,)), ((), ()))
_NT = (((1,), (1,)), ((), ()))
_TN = (((0,), (0,)), ((), ()))


def _mm(a, b, mode, dims=_NN):
    dg = lambda x, y: lax.dot_general(x, y, dims, preferred_element_type=F32)
    if mode == "hi":
        return lax.dot_general(a, b, dims, preferred_element_type=F32, precision=HI)
    ah = a.astype(BF16)
    bh = b.astype(BF16)
    if mode == "bf16":
        return dg(ah, bh)
    al = (a - ah.astype(F32)).astype(BF16)
    bl = (b - bh.astype(F32)).astype(BF16)
    return dg(ah, bh) + (dg(ah, bl) + dg(al, bh))


def _split3(x):
    x1 = x.astype(BF16)
    r1 = x - x1.astype(F32)
    x2 = r1.astype(BF16)
    x3 = (r1 - x2.astype(F32)).astype(BF16)
    return x1, x2, x3


def _mm_exact_lhs(x, w01):
    x1, x2, x3 = _split3(x)
    return (_dot(x3, w01) + _dot(x2, w01)) + _dot(x1, w01)


def _mm_exact_rhs(w01, x):
    x1, x2, x3 = _split3(x)
    return (_dot(w01, x3) + _dot(w01, x2)) + _dot(w01, x1)


def _iota(shape, dim):
    return lax.broadcasted_iota(jnp.int32, shape, dim)


def _head_mask(h, width=GROUP_W):
    lane = _iota((1, width), 1)
    return ((lane >= h * HEAD_DIM) & (lane < (h + 1) * HEAD_DIM)).astype(F32)


def _sigmoid(x):
    return 1.0 / (1.0 + jnp.exp(-x))


def _softplus(x):
    return jnp.maximum(x, 0.0) + jnp.log(1.0 + jnp.exp(-jnp.abs(x)))


def _shift_down(x, prev8, k):
    rx = pltpu.roll(x, k, 0)
    row8 = _iota((8, x.shape[1]), 0)
    head = jnp.where(row8 < k, pltpu.roll(prev8, k, 0), rx[0:8])
    return jnp.concatenate([head, rx[8:]], axis=0)


def _shift_up(x, next8, k):
    n = x.shape[0]
    rx = pltpu.roll(x, n - k, 0)
    row8 = _iota((8, x.shape[1]), 0)
    tail = jnp.where(row8 >= 8 - k, pltpu.roll(next8, 8 - k, 0), rx[n - 8:])
    return jnp.concatenate([rx[:n - 8], tail], axis=0)


def _halo_rows(ref, t0, nrows, seq, cols):
    pstart = pl.multiple_of(jnp.maximum(t0 - 8, 0), 8)
    nstart = pl.multiple_of(jnp.minimum(t0 + nrows, seq - 8), 8)
    prev8 = ref[pl.ds(pstart, 8), cols] * (t0 > 0).astype(F32)
    next8 = ref[pl.ds(nstart, 8), cols] * (t0 + nrows < seq).astype(F32)
    return prev8, next8


def _tri(n, upper):
    r = _iota((n, n), 0)
    c = _iota((n, n), 1)
    return ((c >= r) if upper else (c <= r)).astype(F32)


def _rmsnorm(x, w):
    ms = jnp.mean(x * x, axis=-1, keepdims=True)
    return x * lax.rsqrt(ms + NORM_EPS) * w


def _norm_inproj_kernel(x_ref, nw_ref, w_ref, o_ref, xn_ref):
    @pl.when(pl.program_id(1) == 0)
    def _():
        xn_ref[...] = _rmsnorm(x_ref[...], nw_ref[...]).astype(BF16)

    o_ref[...] = _dot(xn_ref[...], w_ref[...])


def _norm_inproj(h, norm_w, w_bf16, layer, tm=2048, tn=896):
    t = h.shape[0]
    n = w_bf16.shape[2]
    return pl.pallas_call(
        _norm_inproj_kernel,
        grid=(t // tm, n // tn),
        in_specs=[pl.BlockSpec((tm, D_MODEL), lambda i, j: (i, 0)),
                  pl.BlockSpec((1, D_MODEL), lambda i, j: (0, 0)),
                  pl.BlockSpec((None, D_MODEL, tn), lambda i, j: (layer, 0, j))],
        out_specs=pl.BlockSpec((tm, tn), lambda i, j: (i, j)),
        out_shape=jax.ShapeDtypeStruct((t, n), F32),
        scratch_shapes=[pltpu.VMEM((tm, D_MODEL), BF16)],
        compiler_params=pltpu.CompilerParams(dimension_semantics=("parallel", "arbitrary"),
                                             vmem_limit_bytes=V7X_VMEM_LIMIT),
        name="norm_inproj",
    )(h, norm_w, w_bf16)


def _outproj_kernel(h_ref, ya_ref, yb_ref, yc_ref, yd_ref, w_ref, o_ref):
    acc = h_ref[...]
    for g, y_ref in enumerate((ya_ref, yb_ref, yc_ref, yd_ref)):
        acc = acc + _dot(y_ref[...].astype(BF16), w_ref[g * GROUP_W:(g + 1) * GROUP_W, :])
    o_ref[...] = acc


def _outproj(h, ys, w_bf16, layer, tm=512):
    t = h.shape[0]
    yspec = pl.BlockSpec((tm, GROUP_W), lambda i: (i, 0))
    return pl.pallas_call(
        _outproj_kernel,
        grid=(t // tm,),
        in_specs=[pl.BlockSpec((tm, D_MODEL), lambda i: (i, 0)), yspec, yspec, yspec, yspec,
                  pl.BlockSpec((None, D_MODEL, D_MODEL), lambda i: (layer, 0, 0))],
        out_specs=pl.BlockSpec((tm, D_MODEL), lambda i: (i, 0)),
        out_shape=jax.ShapeDtypeStruct((t, D_MODEL), F32),
        compiler_params=pltpu.CompilerParams(dimension_semantics=("parallel",)),
        name="outproj",
    )(h, *ys, w_bf16)


def _mlp_kernel(h_ref, nw_ref, w1_ref, w2_ref, fw_ref, o_ref, xn_ref, *, final_norm):
    f = pl.program_id(1)

    @pl.when(f == 0)
    def _():
        hv = h_ref[...]
        xn_ref[...] = _rmsnorm(hv, nw_ref[...]).astype(BF16)
        o_ref[...] = hv

    m = _dot(xn_ref[...], w1_ref[...])
    a = jnp.square(jnp.maximum(m, 0.0))
    o_ref[...] += _dot(a.astype(BF16), w2_ref[...])

    if final_norm:
        @pl.when(f == pl.num_programs(1) - 1)
        def _():
            o_ref[...] = _rmsnorm(o_ref[...], fw_ref[...])


def _mlp(h, norm_w, w1_bf16, w2_bf16, final_w, layer, final_norm, tm=1024, tf=512):
    t = h.shape[0]
    return pl.pallas_call(
        functools.partial(_mlp_kernel, final_norm=final_norm),
        grid=(t // tm, D_FF // tf),
        in_specs=[pl.BlockSpec((tm, D_MODEL), lambda i, f: (i, 0)),
                  pl.BlockSpec((1, D_MODEL), lambda i, f: (0, 0)),
                  pl.BlockSpec((None, D_MODEL, tf), lambda i, f: (layer, 0, f)),
                  pl.BlockSpec((None, tf, D_MODEL), lambda i, f: (layer, f, 0)),
                  pl.BlockSpec((1, D_MODEL), lambda i, f: (0, 0))],
        out_specs=pl.BlockSpec((tm, D_MODEL), lambda i, f: (i, 0)),
        out_shape=jax.ShapeDtypeStruct((t, D_MODEL), F32),
        scratch_shapes=[pltpu.VMEM((tm, D_MODEL), BF16)],
        compiler_params=pltpu.CompilerParams(dimension_semantics=("parallel", "arbitrary")),
        name="mlp",
    )(h, norm_w, w1_bf16, w2_bf16, final_w)


def _lru_kernel(u_ref, cw_ref, cb_ref, wa_ref, ba_ref, wx_ref, bx_ref, lam_ref, o_ref,
                gate_s, a_s, b_s, *, seq):
    gcols = slice(0, GROUP_W)
    xcols = slice(GROUP_W, 2 * GROUP_W)

    def pre(i, carry):
        t0 = pl.multiple_of(i * ROWS, ROWS)
        rows = pl.ds(t0, ROWS)
        xi = u_ref[rows, xcols]
        prev8, next8 = _halo_rows(u_ref, t0, ROWS, seq, xcols)
        xf = (cw_ref[0:1, :] * _shift_down(xi, prev8, 2) + cw_ref[1:2, :] * _shift_down(xi, prev8, 1)
              + cw_ref[2:3, :] * xi + cw_ref[3:4, :] * _shift_up(xi, next8, 1) + cb_ref[...])
        gi = u_ref[rows, gcols]
        gate_s[rows, :] = 0.5 * gi * (1.0 + jnp.tanh(0.7978845608028654 * (gi + 0.044715 * gi * gi * gi)))
        xfb = xf.astype(BF16)
        for d in range(2):
            rec = _sigmoid(_dot(xfb, wa_ref[d]) + ba_ref[d:d + 1, :])
            inp = _sigmoid(_dot(xfb, wx_ref[d]) + bx_ref[d:d + 1, :])
            log_a = -LRU_C * rec * _softplus(-lam_ref[d:d + 1, :])
            a_s[d, rows, :] = jnp.exp(log_a)
            b_s[d, rows, :] = jnp.sqrt(1.0 - jnp.exp(2.0 * log_a)) * inp * xf
        return carry

    lax.fori_loop(0, seq // ROWS, pre, 0)

    row8 = _iota((8, GROUP_W), 0)

    def scan8(a, b, rev):
        for s in (1, 2, 4):
            if rev:
                keep = row8 < 8 - s
                a_sh = jnp.where(keep, pltpu.roll(a, 8 - s, 0), 1.0)
                b_sh = jnp.where(keep, pltpu.roll(b, 8 - s, 0), 0.0)
            else:
                keep = row8 >= s
                a_sh = jnp.where(keep, pltpu.roll(a, s, 0), 1.0)
                b_sh = jnp.where(keep, pltpu.roll(b, s, 0), 0.0)
            b = a * b_sh + b
            a = a * a_sh
        return a, b

    nblk = seq // 8

    def scan(i, carry):
        cf, cb = carry
        rf = pl.ds(pl.multiple_of(i * 8, 8), 8)
        rb = pl.ds(pl.multiple_of((nblk - 1 - i) * 8, 8), 8)
        af, bf = scan8(a_s[0, rf, :], b_s[0, rf, :], False)
        hf = bf + af * cf
        a_s[0, rf, :] = hf
        ab, bb = scan8(a_s[1, rb, :], b_s[1, rb, :], True)
        hb = bb + ab * cb
        a_s[1, rb, :] = hb
        return hf[7:8, :], hb[0:1, :]

    zero = jnp.zeros((1, GROUP_W), F32)
    lax.fori_loop(0, nblk, scan, (zero, zero))

    def post(i, carry):
        rows = pl.ds(pl.multiple_of(i * ROWS, ROWS), ROWS)
        o_ref[rows, :] = gate_s[rows, :] * (a_s[0, rows, :] + a_s[1, rows, :])
        return carry

    lax.fori_loop(0, seq // ROWS, post, 0)


def _lru(u, col_block, batch, seq, cw, cb, wa, ba, wx, bx, lam):
    full = lambda shape: pl.BlockSpec(shape, lambda b: (0,) * len(shape))
    return pl.pallas_call(
        functools.partial(_lru_kernel, seq=seq),
        grid=(batch,),
        in_specs=[pl.BlockSpec((seq, LRU_COLS), lambda b: (b, col_block)),
                  full((4, GROUP_W)), full((1, GROUP_W)),
                  full((2, GROUP_W, GROUP_W)), full((2, GROUP_W)),
                  full((2, GROUP_W, GROUP_W)), full((2, GROUP_W)), full((2, GROUP_W))],
        out_specs=pl.BlockSpec((seq, GROUP_W), lambda b: (b, 0)),
        out_shape=jax.ShapeDtypeStruct((batch * seq, GROUP_W), F32),
        scratch_shapes=[pltpu.VMEM((seq, GROUP_W), F32),
                        pltpu.VMEM((2, seq, GROUP_W), F32),
                        pltpu.VMEM((2, seq, GROUP_W), F32)],
        compiler_params=pltpu.CompilerParams(dimension_semantics=("parallel",),
                                             vmem_limit_bytes=V7X_VMEM_LIMIT),
        name="rglru",
    )(u, cw, cb, wa, ba, wx, bx, lam)


def _ssd_kernel(u_ref, cw_ref, cb_ref, dtb64_ref, dtb128_ref, alog64_ref, alog128_ref, e64_ref, e128_ref,
                dskip_ref, nw_ref, o_ref, xc_s, y_s, st_s, *, seq):
    zcols = slice(0, GROUP_W)
    xbc_cols = slice(GROUP_W, GROUP_W + SSD_XBC)
    dt_cols = slice(GROUP_W + SSD_XBC, SSD_COLS_PAD)

    def pre(i, carry):
        t0 = pl.multiple_of(i * ROWS, ROWS)
        rows = pl.ds(t0, ROWS)
        xi = u_ref[rows, xbc_cols]
        prev8, next8 = _halo_rows(u_ref, t0, ROWS, seq, xbc_cols)
        xc = (cw_ref[0:1, :] * _shift_down(xi, prev8, 2) + cw_ref[1:2, :] * _shift_down(xi, prev8, 1)
              + cw_ref[2:3, :] * xi + cw_ref[3:4, :] * _shift_up(xi, next8, 1) + cb_ref[...])
        xc = xc * _sigmoid(xc)
        xc_s[rows, :] = xc
        y_s[rows, :] = dskip_ref[...] * xc[:, 0:GROUP_W]
        return carry

    lax.fori_loop(0, seq // ROWS, pre, 0)

    st_s[...] = jnp.zeros_like(st_s)
    n = SSD_CHUNK
    nchunk = seq // n
    rr = _iota((n, n), 0)
    cc = _iota((n, n), 1)
    incls = [cc <= rr, cc >= rr]
    tris = [m.astype(BF16) for m in incls]
    edge_rows = [n - 1, 0]
    hmasks = [_head_mask(h) for h in range(N_HEADS)]
    dirs = (0, 1)
    groups = (0, 1)
    heads = range(N_HEADS)
    gsl = [slice(g * SSD_D_STATE, (g + 1) * SSD_D_STATE) for g in groups]

    def chunk_pair(t0s):
        rows = [pl.ds(t0, n) for t0 in t0s]
        xs = [xc_s[rows[d], 0:GROUP_W] for d in dirs]
        bm = [xc_s[rows[d], GROUP_W:2 * GROUP_W].astype(BF16) for d in dirs]
        cm = [xc_s[rows[d], 2 * GROUP_W:3 * GROUP_W].astype(BF16) for d in dirs]
        dtraw = [u_ref[rows[d], dt_cols] for d in dirs]
        dt64 = [_softplus(_mm_exact_lhs(dtraw[d], e64_ref[d]) + dtb64_ref[d:d + 1, :]) for d in dirs]
        dt128 = [_softplus(_mm_exact_lhs(dtraw[d], e128_ref[d]) + dtb128_ref[d:d + 1, :]) for d in dirs]
        adt64 = [dt64[d] * (-jnp.exp(alog64_ref[d:d + 1, :])) for d in dirs]
        adt128 = [dt128[d] * (-jnp.exp(alog128_ref[d:d + 1, :])) for d in dirs]
        xdt = [xs[d] * dt64[d] for d in dirs]
        xdt_b = [x.astype(BF16) for x in xdt]
        cs64 = [_mm_exact_rhs(tris[d], adt64[d]) for d in dirs]
        cs128 = [_mm_exact_rhs(tris[d], adt128[d]) for d in dirs]
        scores = [[_dot_t(cm[d][:, gsl[g]], bm[d][:, gsl[g]]) for g in groups] for d in dirs]
        y_off = [jnp.concatenate([_dot(cm[d][:, gsl[g]], st_s[d, g].astype(BF16)) for g in groups], axis=1)
                 for d in dirs]
        y = [y_off[d] * jnp.exp(cs64[d]) for d in dirs]
        for d in dirs:
            for h in heads:
                cb = cs128[d][:, h * 128:(h + 1) * 128]
                cb = jnp.concatenate([cb] * (n // 128), axis=1)
                seg = jnp.where(incls[d], cb - cb.T, NA_MASKED)
                m = (scores[d][h // 2] * jnp.exp(seg)).astype(BF16)
                y[d] = y[d] + _dot(m, xdt_b[d]) * hmasks[h]
        for d in dirs:
            y_s[rows[d], :] += y[d]
            edge = cs64[d][edge_rows[d]:edge_rows[d] + 1, :]
            xd = (xdt[d] * jnp.exp(edge - cs64[d])).astype(BF16)
            egrow = jnp.exp(edge)
            for g in groups:
                st_s[d, g] = st_s[d, g] * egrow[:, gsl[g]] + _tdot(bm[d][:, gsl[g]], xd[:, gsl[g]])

    def body(c, carry):
        chunk_pair([pl.multiple_of(c * n, n), pl.multiple_of((nchunk - 1 - c) * n, n)])
        return carry

    lax.fori_loop(0, nchunk, body, 0)

    def post(i, carry):
        rows = pl.ds(pl.multiple_of(i * ROWS, ROWS), ROWS)
        z = u_ref[rows, zcols]
        y = y_s[rows, :] * (z * _sigmoid(z))
        o_ref[rows, :] = _rmsnorm(y, nw_ref[...])
        return carry

    lax.fori_loop(0, seq // ROWS, post, 0)


def _ssd(u, col_block, batch, seq, cw, cb, dtb64, dtb128, alog64, alog128, e64, e128, dskip, nw):
    full = lambda shape: pl.BlockSpec(shape, lambda b: (0,) * len(shape))
    return pl.pallas_call(
        functools.partial(_ssd_kernel, seq=seq),
        grid=(batch,),
        in_specs=[pl.BlockSpec((seq, SSD_COLS_PAD), lambda b: (b, col_block)),
                  full((4, SSD_XBC)), full((1, SSD_XBC)),
                  full((2, GROUP_W)), full((2, 4 * 128)), full((2, GROUP_W)), full((2, 4 * 128)),
                  full((2, SSD_DT_PAD, GROUP_W)), full((2, SSD_DT_PAD, 4 * 128)),
                  full((1, GROUP_W)), full((1, GROUP_W))],
        out_specs=pl.BlockSpec((seq, GROUP_W), lambda b: (b, 0)),
        out_shape=jax.ShapeDtypeStruct((batch * seq, GROUP_W), F32),
        scratch_shapes=[pltpu.VMEM((seq, SSD_XBC), F32),
                        pltpu.VMEM((seq, GROUP_W), F32),
                        pltpu.VMEM((2, 2, SSD_D_STATE, 128), F32)],
        compiler_params=pltpu.CompilerParams(dimension_semantics=("parallel",),
                                             vmem_limit_bytes=V7X_VMEM_LIMIT),
        name="ssd",
    )(u, cw, cb, dtb64, dtb128, alog64, alog128, e64, e128, dskip, nw)


def _unit_lower_inverses(n_mats, eye, blk_mask):
    mm = lambda a, b: _mm(a, b, RWKV_PREC_INV)
    ps = [-(n * blk_mask) for n in n_mats]
    tds = [eye + p for p in ps]
    for _ in range(3):
        ps = [mm(p, p) for p in ps]
        tds = [t + mm(t, p) for t, p in zip(tds, ps)]
    ms = [-mm(t, n * (1.0 - blk_mask)) for t, n in zip(tds, n_mats)]
    m2s = [mm(m, m) for m in ms]
    tos = [eye + m for m in ms]
    tos = [t + mm(t, m2) for t, m2 in zip(tos, m2s)]
    return [mm(to, td) for to, td in zip(tos, tds)]


def _rwkv_kernel(u_ref, mu_ref, w0_ref, wup_ref, a0_ref, aup_ref, gup_ref, kk_ref, ka_ref, rk_ref,
                 gnw_ref, gnb_ref, o_ref,
                 r_s, v_s, n_s, g_s, bonus_s, lw_s, kd_s, b_s, y_s, st_s, *, seq):
    allc = slice(0, RWKV_COLS)
    lane_r = _iota((GROUP_W, GROUP_W), 0) >> 6
    lane_c = _iota((GROUP_W, GROUP_W), 1) >> 6
    blockdiag = (lane_r == lane_c).astype(F32)
    blockdiag_b = blockdiag.astype(BF16)

    def pre(i, carry):
        t0 = pl.multiple_of(i * ROWS, ROWS)
        rows = pl.ds(t0, ROWS)
        x = u_ref[rows, allc]
        prev8, next8 = _halo_rows(u_ref, t0, ROWS, seq, allc)
        prev = _shift_down(x, prev8, 1)
        nxt = _shift_up(x, next8, 1)
        x = x + mu_ref[0:1, :] * (prev - x) + mu_ref[1:2, :] * (nxt - x)
        r = x[:, 0:GROUP_W]
        k = x[:, GROUP_W:2 * GROUP_W]
        v = x[:, 2 * GROUP_W:3 * GROUP_W]
        wd = jnp.tanh(x[:, 3 * GROUP_W:3 * GROUP_W + RWKV_RANK2]).astype(BF16)
        ad = x[:, 3 * GROUP_W + RWKV_RANK2:3 * GROUP_W + 2 * RWKV_RANK2].astype(BF16)
        gd = x[:, 3 * GROUP_W + 2 * RWKV_RANK2:RWKV_COLS]
        kk = k * kk_ref[...]
        kk = kk * lax.rsqrt(_mm_exact_lhs(kk * kk, blockdiag_b) + 1e-12)
        r_s[rows, :] = r
        v_s[rows, :] = v
        n_s[rows, :] = kk
        g_s[rows, :] = _dot(_sigmoid(gd).astype(BF16), gup_ref[...])
        kd_sum = jnp.zeros_like(k)
        for d in range(2):
            z_w = w0_ref[d:d + 1, :] + _dot(wd, wup_ref[d])
            lw_s[d, rows, :] = -RWKV_DECAY_SCALE * _sigmoid(z_w)
            alpha = _sigmoid(a0_ref[d:d + 1, :] + _dot(ad, aup_ref[d]))
            kd = k * (1.0 + (alpha - 1.0) * ka_ref[...])
            kd_s[d, rows, :] = kd
            b_s[d, rows, :] = alpha * kk
            kd_sum = kd_sum + kd
        bonus_s[rows, :] = _mm_exact_lhs(r * kd_sum * rk_ref[...], blockdiag_b) * v
        y_s[rows, :] = jnp.zeros((ROWS, GROUP_W), F32)
        return carry

    lax.fori_loop(0, seq // ROWS, pre, 0)

    st_s[...] = jnp.zeros_like(st_s)
    nchunk = seq // CHUNK
    rr = _iota((CHUNK, CHUNK), 0)
    cc = _iota((CHUNK, CHUNK), 1)
    eye = (rr == cc).astype(F32)
    blk16 = ((rr >> 4) == (cc >> 4)).astype(F32)
    hmasks = [_head_mask(h) for h in range(N_HEADS)]

    tris = [_tri(CHUNK, False).astype(BF16), _tri(CHUNK, True).astype(BF16)]
    stricts = [(cc < rr).astype(F32), (cc > rr).astype(F32)]
    incls = [(cc <= rr).astype(F32), (cc >= rr).astype(F32)]
    edge_rows = [CHUNK - 1, 0]
    dirs = (0, 1)
    heads = range(N_HEADS)

    def chunk_group(t0s):
        units = [(d, k) for d in dirs for k in range(len(t0s[d]))]
        rows = {u: pl.ds(t0s[u[0]][u[1]], CHUNK) for u in units}
        lw = {u: lw_s[u[0], rows[u], :] for u in units}
        cs = {u: _mm_exact_rhs(tris[u[0]], lw[u]) for u in units}
        ginv = {u: jnp.exp(-cs[u]) for u in units}
        kkt = {u: n_s[rows[u], :] * jnp.exp(cs[u] - lw[u]) for u in units}
        rt = {u: r_s[rows[u], :] * jnp.exp(cs[u]) for u in units}
        kh = {u: kd_s[u[0], rows[u], :] * ginv[u] for u in units}
        bh = {u: b_s[u[0], rows[u], :] * ginv[u] for u in units}
        v = {u: v_s[rows[u], :] for u in units}
        lhs = {u: jnp.concatenate([kkt[u], rt[u]], axis=0) for u in units}
        uh = [(u, h) for u in units for h in heads]
        lm = {x: lhs[x[0]] * hmasks[x[1]] for x in uh}
        a_k = {x: _mm(lm[x], kh[x[0]], RWKV_PREC_GRAM, _NT) for x in uh}
        a_b = {x: _mm(lm[x], bh[x[0]], RWKV_PREC_GRAM, _NT) for x in uh}
        t_inv = dict(zip(uh, _unit_lower_inverses([a_b[x][0:CHUNK] * stricts[x[0][0]] for x in uh], eye, blk16)))
        a1v = {x: _mm(a_k[x][0:CHUNK] * stricts[x[0][0]], v[x[0]], RWKV_PREC_APPLY) for x in uh}
        a2v = {x: _mm(a_k[x][CHUNK:] * incls[x[0][0]], v[x[0]], RWKV_PREC_APPLY) for x in uh}
        a3 = {x: a_b[x][CHUNK:] * incls[x[0][0]] for x in uh}
        gl = {u: jnp.exp(cs[u][edge_rows[u[0]]:edge_rows[u[0]] + 1, :]) for u in units}

        s_mat = [st_s[d] for d in dirs]
        for k in range(max(len(t) for t in t0s)):
            us = [(d, k) for d in dirs if k < len(t0s[d])]
            p_all = {u: _mm(lhs[u], s_mat[u[0]], RWKV_PREC_STATE, _NT) for u in us}
            u_h = {(u, h): _mm(t_inv[(u, h)], p_all[u][0:CHUNK] + a1v[(u, h)], RWKV_PREC_APPLY)
                   for u in us for h in heads}
            u_all = {u: sum(u_h[(u, h)] * hmasks[h] for h in heads) for u in us}
            y_h = {(u, h): a2v[(u, h)] - _mm(a3[(u, h)], u_all[u], RWKV_PREC_APPLY) for u in us for h in heads}
            upd = {u: _mm(v[u], kh[u], RWKV_PREC_STATE, _TN) - _mm(u_all[u], bh[u], RWKV_PREC_STATE, _TN)
                   for u in us}
            for u in us:
                y_s[rows[u], :] += p_all[u][CHUNK:] + sum(y_h[(u, h)] * hmasks[h] for h in heads)
                s_mat[u[0]] = (s_mat[u[0]] + upd[u]) * gl[u] * blockdiag
        for d in dirs:
            st_s[d] = s_mat[d]

    def body(c, carry):
        fwd = [pl.multiple_of((c * RWKV_GROUP + k) * CHUNK, CHUNK) for k in range(RWKV_GROUP)]
        bwd = [pl.multiple_of((nchunk - 1 - c * RWKV_GROUP - k) * CHUNK, CHUNK) for k in range(RWKV_GROUP)]
        chunk_group([fwd, bwd])
        return carry

    lax.fori_loop(0, nchunk // RWKV_GROUP, body, 0)

    def post(i, carry):
        rows = pl.ds(pl.multiple_of(i * ROWS, ROWS), ROWS)
        y = y_s[rows, :]
        mean = _mm_exact_lhs(y, blockdiag_b) * (1.0 / HEAD_DIM)
        yc = y - mean
        var = _mm_exact_lhs(yc * yc, blockdiag_b) * (1.0 / HEAD_DIM)
        y = yc * lax.rsqrt(var + RWKV_GN_EPS) * gnw_ref[...] + gnb_ref[...]
        o_ref[rows, :] = (y + bonus_s[rows, :]) * g_s[rows, :]
        return carry

    lax.fori_loop(0, seq // ROWS, post, 0)


def _rwkv(u, col_block, batch, seq, mu, w0, wup, a0, aup, gup, k_k, k_a, r_k, gn_w, gn_b):
    full = lambda shape: pl.BlockSpec(shape, lambda b: (0,) * len(shape))
    tok = lambda: pltpu.VMEM((seq, GROUP_W), F32)
    tok2 = lambda: pltpu.VMEM((2, seq, GROUP_W), F32)
    return pl.pallas_call(
        functools.partial(_rwkv_kernel, seq=seq),
        grid=(batch,),
        in_specs=[pl.BlockSpec((seq, RWKV_COLS), lambda b: (b, col_block)),
                  full((2, RWKV_COLS)), full((2, GROUP_W)), full((2, RWKV_RANK2, GROUP_W)),
                  full((2, GROUP_W)), full((2, RWKV_RANK2, GROUP_W)), full((RWKV_RANK2, GROUP_W)),
                  full((1, GROUP_W)), full((1, GROUP_W)), full((1, GROUP_W)),
                  full((1, GROUP_W)), full((1, GROUP_W))],
        out_specs=pl.BlockSpec((seq, GROUP_W), lambda b: (b, 0)),
        out_shape=jax.ShapeDtypeStruct((batch * seq, GROUP_W), F32),
        scratch_shapes=[tok(), tok(), tok(), tok(), tok(), tok2(), tok2(), tok2(), tok(),
                        pltpu.VMEM((2, GROUP_W, GROUP_W), F32)],
        compiler_params=pltpu.CompilerParams(dimension_semantics=("parallel",),
                                             vmem_limit_bytes=V7X_VMEM_LIMIT),
        name="rwkv7",
    )(u, mu, w0, wup, a0, aup, gup, k_k, k_a, r_k, gn_w, gn_b)


def _natten_kernel(q_ref, k_ref, v_ref, *rest, n_rows):
    bias_refs, o_ref = rest[:-1], rest[-1]
    steps = range(NA_ROWS_PER_STEP)
    hmasks = [_head_mask(h) for h in range(N_HEADS)]
    r = [pl.program_id(1) * NA_ROWS_PER_STEP + j for j in steps]
    start = [jnp.clip(r[j] - NA_WIN_ROWS // 2, 0, n_rows - NA_WIN_ROWS) for j in steps]
    win = [pl.ds(pl.multiple_of(start[j] * GRID_W, GRID_W), NA_WIN_ROWS * GRID_W) for j in steps]
    kw = [k_ref[win[j], :].astype(BF16) for j in steps]
    vw = [v_ref[win[j], :].astype(BF16) for j in steps]
    q = [q_ref[j * GRID_W:(j + 1) * GRID_W, :] * (HEAD_DIM ** -0.5) for j in steps]
    qs = [jnp.concatenate([q[j] * hmasks[h] for h in range(N_HEADS)], axis=0).astype(BF16) for j in steps]
    s = [_dot_t(qs[j], kw[j]) + bias_refs[j][0] for j in steps]
    m = [jnp.max(s[j], axis=-1, keepdims=True) for j in steps]
    p = [jnp.exp(s[j] - m[j]) for j in steps]
    p = [p[j] / jnp.sum(p[j], axis=-1, keepdims=True) for j in steps]
    o = [_dot(p[j].astype(BF16), vw[j]) for j in steps]
    for j in steps:
        o_ref[j * GRID_W:(j + 1) * GRID_W, :] = sum(o[j][h * GRID_W:(h + 1) * GRID_W] * hmasks[h]
                                                    for h in range(N_HEADS))


def _natten(u, q_block, batch, seq, bias_tab):
    n_rows = seq // GRID_W
    nkeys = NA_WIN_ROWS * GRID_W
    nstep = n_rows // NA_ROWS_PER_STEP
    qrows = NA_ROWS_PER_STEP * GRID_W

    def bias_index(j):
        def index(b, i):
            r = i * NA_ROWS_PER_STEP + j
            start = jnp.clip(r - NA_WIN_ROWS // 2, 0, n_rows - NA_WIN_ROWS)
            return (start - r + NA_WIN_ROWS - 1, 0, 0)
        return index

    bias_specs = [pl.BlockSpec((1, N_HEADS * GRID_W, nkeys), bias_index(j)) for j in range(NA_ROWS_PER_STEP)]
    return pl.pallas_call(
        functools.partial(_natten_kernel, n_rows=n_rows),
        grid=(batch, nstep),
        in_specs=[pl.BlockSpec((qrows, GROUP_W), lambda b, i: (b * nstep + i, q_block)),
                  pl.BlockSpec((seq, GROUP_W), lambda b, i: (b, q_block + 1)),
                  pl.BlockSpec((seq, GROUP_W), lambda b, i: (b, q_block + 2))] + bias_specs,
        out_specs=pl.BlockSpec((qrows, GROUP_W), lambda b, i: (b * nstep + i, 0)),
        out_shape=jax.ShapeDtypeStruct((batch * seq, GROUP_W), F32),
        compiler_params=pltpu.CompilerParams(dimension_semantics=("parallel", "arbitrary")),
        name="natten",
    )(u, u, u, *([bias_tab] * NA_ROWS_PER_STEP))


def _natten_bias_table(rel_bias):
    ncol = 2 * NA_WIN_COLS - 1
    qc = np.arange(GRID_W)
    kc = np.arange(GRID_W)
    ws = np.clip(qc - NA_WIN_COLS // 2, 0, GRID_W - NA_WIN_COLS)
    in_win = (kc[None, :] >= ws[:, None]) & (kc[None, :] < ws[:, None] + NA_WIN_COLS)
    col_i = np.clip(kc[None, :] - qc[:, None] + NA_WIN_COLS - 1, 0, ncol - 1)
    onehot = ((col_i[None] == np.arange(ncol)[:, None, None]) & in_win[None]).astype(np.float32)
    mask_add = np.where(in_win, 0.0, NA_MASKED).astype(np.float32)
    toep = jnp.einsum('hrc,cqk->hrqk', rel_bias.astype(F32), jnp.asarray(onehot), precision=HI) + mask_add
    tab = jnp.stack([toep[:, d:d + NA_WIN_ROWS] for d in range(NA_WIN_ROWS)])
    return tab.transpose(0, 1, 3, 2, 4).reshape(NA_WIN_ROWS, N_HEADS * GRID_W, NA_WIN_ROWS * GRID_W)


def _pack_w_in(w_in):
    o_ssd = RWKV_COLS
    o_dt_end = o_ssd + GROUP_W + SSD_XBC + 8
    o_lru_end = o_dt_end + LRU_COLS
    pad = jnp.zeros(w_in.shape[:-1] + (SSD_DT_PAD - 8,), w_in.dtype)
    return jnp.concatenate([w_in[..., :o_dt_end], pad, w_in[..., o_lru_end:], w_in[..., o_dt_end:o_lru_end]],
                           axis=-1).astype(BF16)


def _pad_rank(w_up):
    z = jnp.zeros_like(w_up[0])
    return jnp.stack([jnp.concatenate([w_up[0], z], axis=0), jnp.concatenate([z, w_up[1]], axis=0)])


def _block_diag(w):
    _, nb, n, _ = w.shape
    eye = jnp.eye(nb, dtype=w.dtype)
    return jnp.einsum('dkij,kl->dkilj', w, eye).reshape(2, nb * n, nb * n)


def _dt_expanders():
    e64 = np.zeros((2, SSD_DT_PAD, GROUP_W), np.float32)
    e128 = np.zeros((2, SSD_DT_PAD, 4 * 128), np.float32)
    for d in range(2):
        for h in range(N_HEADS):
            e64[d, d * N_HEADS + h, h * HEAD_DIM:(h + 1) * HEAD_DIM] = 1.0
            e128[d, d * N_HEADS + h, h * 128:(h + 1) * 128] = 1.0
    return jnp.asarray(e64, BF16), jnp.asarray(e128, BF16)


def kernel(x, norm1_w, w_in, rwkv_shift_mu, rwkv_w0, rwkv_w_up, rwkv_a0, rwkv_a_up, rwkv_g_up, rwkv_k_k, rwkv_k_a, rwkv_r_k, rwkv_gn_w, rwkv_gn_b, ssd_conv_w, ssd_conv_b, ssd_dt_bias, ssd_a_log, ssd_d, ssd_norm_w, lru_conv_w, lru_conv_b, lru_gate_a_w, lru_gate_a_b, lru_gate_x_w, lru_gate_x_b, lru_lambda, na_rel_bias, w_out, norm2_w, w_mlp1, w_mlp2, final_norm_w):
    batch, seq, _ = x.shape
    depth = w_in.shape[0]
    h = x.reshape(batch * seq, D_MODEL)

    w_in_p = _pack_w_in(w_in)
    w_out_b = w_out.astype(BF16)
    w1_b = w_mlp1.astype(BF16)
    w2_b = w_mlp2.astype(BF16)
    e64, e128 = _dt_expanders()
    row = lambda a: a.reshape(1, -1)
    rep = lambda a, n: jnp.repeat(a, n, axis=-1)

    for l in range(depth):
        u = _norm_inproj(h, row(norm1_w[l]), w_in_p, l)
        y_a = _rwkv(u, 0, batch, seq, rwkv_shift_mu[l], rwkv_w0[l], _pad_rank(rwkv_w_up[l]).astype(BF16),
                    rwkv_a0[l], _pad_rank(rwkv_a_up[l]).astype(BF16), rwkv_g_up[l].astype(BF16),
                    row(rwkv_k_k[l]), row(rwkv_k_a[l]), row(rwkv_r_k[l]), row(rwkv_gn_w[l]), row(rwkv_gn_b[l]))
        y_b = _ssd(u, 1, batch, seq, ssd_conv_w[l], row(ssd_conv_b[l]),
                   rep(ssd_dt_bias[l], HEAD_DIM), rep(ssd_dt_bias[l], 128),
                   rep(ssd_a_log[l], HEAD_DIM), rep(ssd_a_log[l], 128), e64, e128,
                   row(rep(ssd_d[l], HEAD_DIM)), row(ssd_norm_w[l]))
        y_c = _lru(u, (RWKV_COLS + SSD_COLS_PAD + NA_COLS) // LRU_COLS, batch, seq,
                   lru_conv_w[l], row(lru_conv_b[l]),
                   _block_diag(lru_gate_a_w[l]).astype(BF16), lru_gate_a_b[l],
                   _block_diag(lru_gate_x_w[l]).astype(BF16), lru_gate_x_b[l], lru_lambda[l])
        y_d = _natten(u, (RWKV_COLS + SSD_COLS_PAD) // GROUP_W, batch, seq, _natten_bias_table(na_rel_bias[l]))
        h = _outproj(h, (y_a, y_b, y_c, y_d), w_out_b, l)
        h = _mlp(h, row(norm2_w[l]), w1_b, w2_b, row(final_norm_w), l, final_norm=(l == depth - 1))
    return h.reshape(batch, seq, D_MODEL)
```

```python
import functools

import numpy as np
import jax
import jax.numpy as jnp
from jax import lax
from jax.experimental import pallas as pl
from jax.experimental.pallas import tpu as pltpu

F32 = jnp.float32
BF16 = jnp.bfloat16
HI = lax.Precision.HIGHEST

D_MODEL = 1024
GRID_W = 64
GROUP_W = 256
HEAD_DIM = 64
N_HEADS = GROUP_W // HEAD_DIM
D_FF = 4 * D_MODEL
NORM_EPS = 1e-5

RWKV_RANK2 = 128
RWKV_DECAY_SCALE = 0.6065306597126334
RWKV_GN_EPS = 64e-5
RWKV_COLS = 3 * GROUP_W + 3 * RWKV_RANK2

SSD_D_STATE = 128
SSD_XBC = GROUP_W + 4 * SSD_D_STATE
SSD_DT_PAD = 128
SSD_COLS_PAD = GROUP_W + SSD_XBC + SSD_DT_PAD

LRU_C = 8.0
LRU_COLS = 2 * GROUP_W
NA_COLS = 3 * GROUP_W
NA_WIN_ROWS = 8
NA_WIN_COLS = 16
NA_MASKED = -1e30
NA_ROWS_PER_STEP = 4

U_COLS = RWKV_COLS + SSD_COLS_PAD + NA_COLS + LRU_COLS
CHUNK = 64
RWKV_GROUP = 4
SSD_CHUNK = 256
ROWS = 256

V7X_VMEM_LIMIT = 60 * 1024 * 1024

RWKV_PREC_GRAM = "bf16"
RWKV_PREC_APPLY = "bf16"
RWKV_PREC_INV = "bf16"
RWKV_PREC_STATE = "bf16"


def _dot(a, b, prec=None):
    return jnp.dot(a, b, preferred_element_type=F32, precision=prec)


def _dot_t(a, b, prec=None):
    return lax.dot_general(a, b, (((1,), (1,)), ((), ())), preferred_element_type=F32, precision=prec)


def _tdot(a, b, prec=None):
    return lax.dot_general(a, b, (((0,), (0,)), ((), ())), preferred_element_type=F32, precision=prec)


_NN = (((1,), (0,)), ((), ()))
_NT = (((1,), (1,)), ((), ()))
_TN = (((0,), (0,)), ((), ()))


def _mm(a, b, mode, dims=_NN):
    dg = lambda x, y: lax.dot_general(x, y, dims, preferred_element_type=F32)
    if mode == "hi":
        return lax.dot_general(a, b, dims, preferred_element_type=F32, precision=HI)
    ah = a.astype(BF16)
    bh = b.astype(BF16)
    if mode == "bf16":
        return dg(ah, bh)
    al = (a - ah.astype(F32)).astype(BF16)
    bl = (b - bh.astype(F32)).astype(BF16)
    return dg(ah, bh) + (dg(ah, bl) + dg(al, bh))


def _split3(x):
    x1 = x.astype(BF16)
    r1 = x - x1.astype(F32)
    x2 = r1.astype(BF16)
    x3 = (r1 - x2.astype(F32)).astype(BF16)
    return x1, x2, x3


def _mm_exact_lhs(x, w01):
    x1, x2, x3 = _split3(x)
    return (_dot(x3, w01) + _dot(x2, w01)) + _dot(x1, w01)


def _mm_exact_rhs(w01, x):
    x1, x2, x3 = _split3(x)
    return (_dot(w01, x3) + _dot(w01, x2)) + _dot(w01, x1)


def _iota(shape, dim):
    return lax.broadcasted_iota(jnp.int32, shape, dim)


def _head_mask(h, width=GROUP_W):
    lane = _iota((1, width), 1)
    return ((lane >= h * HEAD_DIM) & (lane < (h + 1) * HEAD_DIM)).astype(F32)


def _sigmoid(x):
    return 0.5 * jnp.tanh(0.5 * x) + 0.5


def _softplus(x):
    return jnp.maximum(x, 0.0) + jnp.log(1.0 + jnp.exp(-jnp.abs(x)))


def _shift_down(x, prev8, k):
    rx = pltpu.roll(x, k, 0)
    row8 = _iota((8, x.shape[1]), 0)
    head = jnp.where(row8 < k, pltpu.roll(prev8, k, 0), rx[0:8])
    return jnp.concatenate([head, rx[8:]], axis=0)


def _shift_up(x, next8, k):
    n = x.shape[0]
    rx = pltpu.roll(x, n - k, 0)
    row8 = _iota((8, x.shape[1]), 0)
    tail = jnp.where(row8 >= 8 - k, pltpu.roll(next8, 8 - k, 0), rx[n - 8:])
    return jnp.concatenate([rx[:n - 8], tail], axis=0)


def _halo_rows(ref, t0, nrows, seq, cols):
    pstart = pl.multiple_of(jnp.maximum(t0 - 8, 0), 8)
    nstart = pl.multiple_of(jnp.minimum(t0 + nrows, seq - 8), 8)
    prev8 = ref[pl.ds(pstart, 8), cols] * (t0 > 0).astype(F32)
    next8 = ref[pl.ds(nstart, 8), cols] * (t0 + nrows < seq).astype(F32)
    return prev8, next8


def _tri(n, upper):
    r = _iota((n, n), 0)
    c = _iota((n, n), 1)
    return ((c >= r) if upper else (c <= r)).astype(F32)


def _rmsnorm(x, w):
    ms = jnp.mean(x * x, axis=-1, keepdims=True)
    return x * lax.rsqrt(ms + NORM_EPS) * w


def _norm_inproj_kernel(x_ref, nw_ref, w_ref, o_ref, xn_ref):
    @pl.when(pl.program_id(1) == 0)
    def _():
        xn_ref[...] = _rmsnorm(x_ref[...], nw_ref[...]).astype(BF16)

    o_ref[...] = _dot(xn_ref[...], w_ref[...])


def _norm_inproj(h, norm_w, w_bf16, layer, tm=2048, tn=896):
    t = h.shape[0]
    n = w_bf16.shape[2]
    return pl.pallas_call(
        _norm_inproj_kernel,
        grid=(t // tm, n // tn),
        in_specs=[pl.BlockSpec((tm, D_MODEL), lambda i, j: (i, 0)),
                  pl.BlockSpec((1, D_MODEL), lambda i, j: (0, 0)),
                  pl.BlockSpec((None, D_MODEL, tn), lambda i, j: (layer, 0, j))],
        out_specs=pl.BlockSpec((tm, tn), lambda i, j: (i, j)),
        out_shape=jax.ShapeDtypeStruct((t, n), F32),
        scratch_shapes=[pltpu.VMEM((tm, D_MODEL), BF16)],
        compiler_params=pltpu.CompilerParams(dimension_semantics=("parallel", "arbitrary"),
                                             vmem_limit_bytes=V7X_VMEM_LIMIT),
        name="norm_inproj",
    )(h, norm_w, w_bf16)


def _outproj_kernel(h_ref, ya_ref, yb_ref, yc_ref, yd_ref, w_ref, o_ref):
    acc = h_ref[...]
    for g, y_ref in enumerate((ya_ref, yb_ref, yc_ref, yd_ref)):
        acc = acc + _dot(y_ref[...].astype(BF16), w_ref[g * GROUP_W:(g + 1) * GROUP_W, :])
    o_ref[...] = acc


def _outproj(h, ys, w_bf16, layer, tm=512):
    t = h.shape[0]
    yspec = pl.BlockSpec((tm, GROUP_W), lambda i: (i, 0))
    return pl.pallas_call(
        _outproj_kernel,
        grid=(t // tm,),
        in_specs=[pl.BlockSpec((tm, D_MODEL), lambda i: (i, 0)), yspec, yspec, yspec, yspec,
                  pl.BlockSpec((None, D_MODEL, D_MODEL), lambda i: (layer, 0, 0))],
        out_specs=pl.BlockSpec((tm, D_MODEL), lambda i: (i, 0)),
        out_shape=jax.ShapeDtypeStruct((t, D_MODEL), F32),
        compiler_params=pltpu.CompilerParams(dimension_semantics=("parallel",)),
        name="outproj",
    )(h, *ys, w_bf16)


def _mlp_kernel(h_ref, nw_ref, w1_ref, w2_ref, fw_ref, o_ref, xn_ref, *, final_norm):
    f = pl.program_id(1)

    @pl.when(f == 0)
    def _():
        hv = h_ref[...]
        xn_ref[...] = _rmsnorm(hv, nw_ref[...]).astype(BF16)
        o_ref[...] = hv

    m = _dot(xn_ref[...], w1_ref[...])
    a = jnp.square(jnp.maximum(m, 0.0))
    o_ref[...] += _dot(a.astype(BF16), w2_ref[...])

    if final_norm:
        @pl.when(f == pl.num_programs(1) - 1)
        def _():
            o_ref[...] = _rmsnorm(o_ref[...], fw_ref[...])


def _mlp(h, norm_w, w1_bf16, w2_bf16, final_w, layer, final_norm, tm=1024, tf=512):
    t = h.shape[0]
    return pl.pallas_call(
        functools.partial(_mlp_kernel, final_norm=final_norm),
        grid=(t // tm, D_FF // tf),
        in_specs=[pl.BlockSpec((tm, D_MODEL), lambda i, f: (i, 0)),
                  pl.BlockSpec((1, D_MODEL), lambda i, f: (0, 0)),
                  pl.BlockSpec((None, D_MODEL, tf), lambda i, f: (layer, 0, f)),
                  pl.BlockSpec((None, tf, D_MODEL), lambda i, f: (layer, f, 0)),
                  pl.BlockSpec((1, D_MODEL), lambda i, f: (0, 0))],
        out_specs=pl.BlockSpec((tm, D_MODEL), lambda i, f: (i, 0)),
        out_shape=jax.ShapeDtypeStruct((t, D_MODEL), F32),
        scratch_shapes=[pltpu.VMEM((tm, D_MODEL), BF16)],
        compiler_params=pltpu.CompilerParams(dimension_semantics=("parallel", "arbitrary")),
        name="mlp",
    )(h, norm_w, w1_bf16, w2_bf16, final_w)


def _lru_kernel(u_ref, cw_ref, cb_ref, wa_ref, ba_ref, wx_ref, bx_ref, lam_ref, o_ref,
                gate_s, a_s, b_s, *, seq):
    gcols = slice(0, GROUP_W)
    xcols = slice(GROUP_W, 2 * GROUP_W)

    def pre(i, carry):
        t0 = pl.multiple_of(i * ROWS, ROWS)
        rows = pl.ds(t0, ROWS)
        xi = u_ref[rows, xcols]
        prev8, next8 = _halo_rows(u_ref, t0, ROWS, seq, xcols)
        xf = (cw_ref[0:1, :] * _shift_down(xi, prev8, 2) + cw_ref[1:2, :] * _shift_down(xi, prev8, 1)
              + cw_ref[2:3, :] * xi + cw_ref[3:4, :] * _shift_up(xi, next8, 1) + cb_ref[...])
        gi = u_ref[rows, gcols]
        gate_s[rows, :] = 0.5 * gi * (1.0 + jnp.tanh(0.7978845608028654 * (gi + 0.044715 * gi * gi * gi)))
        xfb = xf.astype(BF16)
        for d in range(2):
            rec = _sigmoid(_dot(xfb, wa_ref[d]) + ba_ref[d:d + 1, :])
            inp = _sigmoid(_dot(xfb, wx_ref[d]) + bx_ref[d:d + 1, :])
            log_a = -LRU_C * rec * _softplus(-lam_ref[d:d + 1, :])
            a_s[d, rows, :] = jnp.exp(log_a)
            b_s[d, rows, :] = jnp.sqrt(1.0 - jnp.exp(2.0 * log_a)) * inp * xf
        return carry

    lax.fori_loop(0, seq // ROWS, pre, 0)

    row8 = _iota((8, GROUP_W), 0)

    def scan8(a, b, rev):
        for s in (1, 2, 4):
            if rev:
                keep = row8 < 8 - s
                a_sh = jnp.where(keep, pltpu.roll(a, 8 - s, 0), 1.0)
                b_sh = jnp.where(keep, pltpu.roll(b, 8 - s, 0), 0.0)
            else:
                keep = row8 >= s
                a_sh = jnp.where(keep, pltpu.roll(a, s, 0), 1.0)
                b_sh = jnp.where(keep, pltpu.roll(b, s, 0), 0.0)
            b = a * b_sh + b
            a = a * a_sh
        return a, b

    nblk = seq // 8

    def scan(i, carry):
        cf, cb = carry
        rf = pl.ds(pl.multiple_of(i * 8, 8), 8)
        rb = pl.ds(pl.multiple_of((nblk - 1 - i) * 8, 8), 8)
        af, bf = scan8(a_s[0, rf, :], b_s[0, rf, :], False)
        hf = bf + af * cf
        a_s[0, rf, :] = hf
        ab, bb = scan8(a_s[1, rb, :], b_s[1, rb, :], True)
        hb = bb + ab * cb
        a_s[1, rb, :] = hb
        return hf[7:8, :], hb[0:1, :]

    zero = jnp.zeros((1, GROUP_W), F32)
    lax.fori_loop(0, nblk, scan, (zero, zero))

    def post(i, carry):
        rows = pl.ds(pl.multiple_of(i * ROWS, ROWS), ROWS)
        o_ref[rows, :] = gate_s[rows, :] * (a_s[0, rows, :] + a_s[1, rows, :])
        return carry

    lax.fori_loop(0, seq // ROWS, post, 0)


def _lru(u, col_block, batch, seq, cw, cb, wa, ba, wx, bx, lam):
    full = lambda shape: pl.BlockSpec(shape, lambda b: (0,) * len(shape))
    return pl.pallas_call(
        functools.partial(_lru_kernel, seq=seq),
        grid=(batch,),
        in_specs=[pl.BlockSpec((seq, LRU_COLS), lambda b: (b, col_block)),
                  full((4, GROUP_W)), full((1, GROUP_W)),
                  full((2, GROUP_W, GROUP_W)), full((2, GROUP_W)),
                  full((2, GROUP_W, GROUP_W)), full((2, GROUP_W)), full((2, GROUP_W))],
        out_specs=pl.BlockSpec((seq, GROUP_W), lambda b: (b, 0)),
        out_shape=jax.ShapeDtypeStruct((batch * seq, GROUP_W), F32),
        scratch_shapes=[pltpu.VMEM((seq, GROUP_W), F32),
                        pltpu.VMEM((2, seq, GROUP_W), F32),
                        pltpu.VMEM((2, seq, GROUP_W), F32)],
        compiler_params=pltpu.CompilerParams(dimension_semantics=("parallel",),
                                             vmem_limit_bytes=V7X_VMEM_LIMIT),
        name="rglru",
    )(u, cw, cb, wa, ba, wx, bx, lam)


def _ssd_kernel(u_ref, cw_ref, cb_ref, dtb64_ref, alog64_ref, e64_ref,
                dskip_ref, nw_ref, o_ref, xc_s, y_s, st_s, *, seq):
    zcols = slice(0, GROUP_W)
    xbc_cols = slice(GROUP_W, GROUP_W + SSD_XBC)
    dt_cols = slice(GROUP_W + SSD_XBC, SSD_COLS_PAD)

    def pre(i, carry):
        t0 = pl.multiple_of(i * ROWS, ROWS)
        rows = pl.ds(t0, ROWS)
        xi = u_ref[rows, xbc_cols]
        prev8, next8 = _halo_rows(u_ref, t0, ROWS, seq, xbc_cols)
        xc = (cw_ref[0:1, :] * _shift_down(xi, prev8, 2) + cw_ref[1:2, :] * _shift_down(xi, prev8, 1)
              + cw_ref[2:3, :] * xi + cw_ref[3:4, :] * _shift_up(xi, next8, 1) + cb_ref[...])
        xc = xc * _sigmoid(xc)
        xc_s[rows, :] = xc
        y_s[rows, :] = dskip_ref[...] * xc[:, 0:GROUP_W]
        return carry

    lax.fori_loop(0, seq // ROWS, pre, 0)

    st_s[...] = jnp.zeros_like(st_s)
    n = SSD_CHUNK
    nchunk = seq // n
    rr = _iota((n, n), 0)
    cc = _iota((n, n), 1)
    incls = [cc <= rr, cc >= rr]
    tris = [m.astype(BF16) for m in incls]
    edge_rows = [n - 1, 0]
    hmasks = [_head_mask(h) for h in range(N_HEADS)]
    dirs = (0, 1)
    groups = (0, 1)
    heads = range(N_HEADS)
    gsl = [slice(g * SSD_D_STATE, (g + 1) * SSD_D_STATE) for g in groups]

    def chunk_pair(t0s):
        rows = [pl.ds(t0, n) for t0 in t0s]
        xs = [xc_s[rows[d], 0:GROUP_W] for d in dirs]
        bm = [xc_s[rows[d], GROUP_W:2 * GROUP_W].astype(BF16) for d in dirs]
        cm = [xc_s[rows[d], 2 * GROUP_W:3 * GROUP_W].astype(BF16) for d in dirs]
        dtraw = [u_ref[rows[d], dt_cols] for d in dirs]
        dt64 = [_softplus(_mm_exact_lhs(dtraw[d], e64_ref[d]) + dtb64_ref[d:d + 1, :]) for d in dirs]
        adt64 = [dt64[d] * (-jnp.exp(alog64_ref[d:d + 1, :])) for d in dirs]
        xdt = [xs[d] * dt64[d] for d in dirs]
        xdt_b = [x.astype(BF16) for x in xdt]
        cs64 = [_mm_exact_rhs(tris[d], adt64[d]) for d in dirs]
        scores = [[_dot_t(cm[d][:, gsl[g]], bm[d][:, gsl[g]]) for g in groups] for d in dirs]
        y_off = [jnp.concatenate([_dot(cm[d][:, gsl[g]], st_s[d, g].astype(BF16)) for g in groups], axis=1)
                 for d in dirs]
        y = [y_off[d] * jnp.exp(cs64[d]) for d in dirs]
        for d in dirs:
            for h in heads:
                cb = jnp.broadcast_to(cs64[d][:, h * HEAD_DIM:h * HEAD_DIM + 1], (n, n))
                seg = jnp.where(incls[d], cb - cb.T, NA_MASKED)
                m = (scores[d][h // 2] * jnp.exp(seg)).astype(BF16)
                y[d] = y[d] + _dot(m, xdt_b[d]) * hmasks[h]
        for d in dirs:
            y_s[rows[d], :] += y[d]
            edge = cs64[d][edge_rows[d]:edge_rows[d] + 1, :]
            xd = (xdt[d] * jnp.exp(edge - cs64[d])).astype(BF16)
            egrow = jnp.exp(edge)
            for g in groups:
                st_s[d, g] = st_s[d, g] * egrow[:, gsl[g]] + _tdot(bm[d][:, gsl[g]], xd[:, gsl[g]])

    def body(c, carry):
        chunk_pair([pl.multiple_of(c * n, n), pl.multiple_of((nchunk - 1 - c) * n, n)])
        return carry

    lax.fori_loop(0, nchunk, body, 0)

    def post(i, carry):
        rows = pl.ds(pl.multiple_of(i * ROWS, ROWS), ROWS)
        z = u_ref[rows, zcols]
        y = y_s[rows, :] * (z * _sigmoid(z))
        o_ref[rows, :] = _rmsnorm(y, nw_ref[...])
        return carry

    lax.fori_loop(0, seq // ROWS, post, 0)


def _ssd(u, col_block, batch, seq, cw, cb, dtb64, alog64, e64, dskip, nw):
    full = lambda shape: pl.BlockSpec(shape, lambda b: (0,) * len(shape))
    return pl.pallas_call(
        functools.partial(_ssd_kernel, seq=seq),
        grid=(batch,),
        in_specs=[pl.BlockSpec((seq, SSD_COLS_PAD), lambda b: (b, col_block)),
                  full((4, SSD_XBC)), full((1, SSD_XBC)),
                  full((2, GROUP_W)), full((2, GROUP_W)), full((2, SSD_DT_PAD, GROUP_W)),
                  full((1, GROUP_W)), full((1, GROUP_W))],
        out_specs=pl.BlockSpec((seq, GROUP_W), lambda b: (b, 0)),
        out_shape=jax.ShapeDtypeStruct((batch * seq, GROUP_W), F32),
        scratch_shapes=[pltpu.VMEM((seq, SSD_XBC), F32),
                        pltpu.VMEM((seq, GROUP_W), F32),
                        pltpu.VMEM((2, 2, SSD_D_STATE, 128), F32)],
        compiler_params=pltpu.CompilerParams(dimension_semantics=("parallel",),
                                             vmem_limit_bytes=V7X_VMEM_LIMIT),
        name="ssd",
    )(u, cw, cb, dtb64, alog64, e64, dskip, nw)


def _pair_blockdiag(x, pmasks):
    return jnp.concatenate([x * pmasks[0], x * pmasks[1]], axis=0)


def _unit_lower_inverses(n_mats, eye, blk_mask, pmasks):
    mm = lambda a, b: _mm(a, _pair_blockdiag(b, pmasks), RWKV_PREC_INV)
    ps = [-(n * blk_mask) for n in n_mats]
    tds = [eye + p for p in ps]
    for _ in range(3):
        ps = [mm(p, p) for p in ps]
        tds = [t + mm(t, p) for t, p in zip(tds, ps)]
    ms = [-mm(t, n * (1.0 - blk_mask)) for t, n in zip(tds, n_mats)]
    m2s = [mm(m, m) for m in ms]
    tos = [eye + m for m in ms]
    tos = [t + mm(t, m2) for t, m2 in zip(tos, m2s)]
    return [mm(to, td) for to, td in zip(tos, tds)]


def _rwkv_kernel(u_ref, mu_ref, w0_ref, wup_ref, a0_ref, aup_ref, gup_ref, kk_ref, ka_ref, rk_ref,
                 gnw_ref, gnb_ref, o_ref,
                 r_s, v_s, n_s, g_s, bonus_s, lw_s, kd_s, b_s, y_s, st_s, *, seq):
    allc = slice(0, RWKV_COLS)
    lane_r = _iota((GROUP_W, GROUP_W), 0) >> 6
    lane_c = _iota((GROUP_W, GROUP_W), 1) >> 6
    blockdiag = (lane_r == lane_c).astype(F32)
    blockdiag_b = blockdiag.astype(BF16)

    def pre(i, carry):
        t0 = pl.multiple_of(i * ROWS, ROWS)
        rows = pl.ds(t0, ROWS)
        x = u_ref[rows, allc]
        prev8, next8 = _halo_rows(u_ref, t0, ROWS, seq, allc)
        prev = _shift_down(x, prev8, 1)
        nxt = _shift_up(x, next8, 1)
        x = x + mu_ref[0:1, :] * (prev - x) + mu_ref[1:2, :] * (nxt - x)
        r = x[:, 0:GROUP_W]
        k = x[:, GROUP_W:2 * GROUP_W]
        v = x[:, 2 * GROUP_W:3 * GROUP_W]
        wd = jnp.tanh(x[:, 3 * GROUP_W:3 * GROUP_W + RWKV_RANK2]).astype(BF16)
        ad = x[:, 3 * GROUP_W + RWKV_RANK2:3 * GROUP_W + 2 * RWKV_RANK2].astype(BF16)
        gd = x[:, 3 * GROUP_W + 2 * RWKV_RANK2:RWKV_COLS]
        kk = k * kk_ref[...]
        kk = kk * lax.rsqrt(_mm_exact_lhs(kk * kk, blockdiag_b) + 1e-12)
        r_s[rows, :] = r
        v_s[rows, :] = v
        n_s[rows, :] = kk
        g_s[rows, :] = _dot(_sigmoid(gd).astype(BF16), gup_ref[...])
        kd_sum = jnp.zeros_like(k)
        for d in range(2):
            z_w = w0_ref[d:d + 1, :] + _dot(wd, wup_ref[d])
            lw_s[d, rows, :] = -RWKV_DECAY_SCALE * _sigmoid(z_w)
            alpha = _sigmoid(a0_ref[d:d + 1, :] + _dot(ad, aup_ref[d]))
            kd = k * (1.0 + (alpha - 1.0) * ka_ref[...])
            kd_s[d, rows, :] = kd
            b_s[d, rows, :] = alpha * kk
            kd_sum = kd_sum + kd
        bonus_s[rows, :] = _mm_exact_lhs(r * kd_sum * rk_ref[...], blockdiag_b) * v
        y_s[rows, :] = jnp.zeros((ROWS, GROUP_W), F32)
        return carry

    lax.fori_loop(0, seq // ROWS, pre, 0)

    st_s[...] = jnp.zeros_like(st_s)
    nchunk = seq // CHUNK
    rr = _iota((CHUNK, CHUNK), 0)
    cc = _iota((CHUNK, CHUNK), 1)
    eye = (rr == cc).astype(F32)
    blk16 = ((rr >> 4) == (cc >> 4)).astype(F32)
    hmasks = [_head_mask(h) for h in range(N_HEADS)]

    tris = [_tri(CHUNK, False).astype(BF16), _tri(CHUNK, True).astype(BF16)]
    stricts = [(cc < rr).astype(F32), (cc > rr).astype(F32)]
    incls = [(cc <= rr).astype(F32), (cc >= rr).astype(F32)]
    edge_rows = [CHUNK - 1, 0]
    dirs = (0, 1)
    heads = range(N_HEADS)
    lane128 = _iota((1, 128), 1)
    pmasks = [(lane128 < HEAD_DIM).astype(F32), (lane128 >= HEAD_DIM).astype(F32)]
    pairs = (0, 1)
    twice = lambda m: jnp.concatenate([m, m], axis=1)
    stricts2 = [twice(m) for m in stricts]
    incls2 = [twice(m) for m in incls]
    eye2 = twice(eye)
    blk16_2 = twice(blk16)
    sel12 = [jnp.concatenate([stricts2[d], incls2[d]], axis=0) for d in dirs]

    def chunk_group(t0s):
        units = [(d, k) for d in dirs for k in range(len(t0s[d]))]
        rows = {u: pl.ds(t0s[u[0]][u[1]], CHUNK) for u in units}
        lw = {u: lw_s[u[0], rows[u], :] for u in units}
        cs = {u: _mm_exact_rhs(tris[u[0]], lw[u]) for u in units}
        ginv = {u: jnp.exp(-cs[u]) for u in units}
        kkt = {u: n_s[rows[u], :] * jnp.exp(cs[u] - lw[u]) for u in units}
        rt = {u: r_s[rows[u], :] * jnp.exp(cs[u]) for u in units}
        kh = {u: kd_s[u[0], rows[u], :] * ginv[u] for u in units}
        bh = {u: b_s[u[0], rows[u], :] * ginv[u] for u in units}
        v = {u: v_s[rows[u], :] for u in units}
        lhs = {u: jnp.concatenate([kkt[u], rt[u]], axis=0) for u in units}
        uh = [(u, h) for u in units for h in heads]
        up = [(u, p) for u in units for p in pairs]
        pair = lambda a, p: a[:, p * 128:(p + 1) * 128]
        bd = lambda a: _pair_blockdiag(a, pmasks)
        lm = {x: pair(lhs[x[0]], x[1] // 2) * pmasks[x[1] % 2] for x in uh}
        a_k = {x: _mm(lm[x], pair(kh[x[0]], x[1] // 2), RWKV_PREC_GRAM, _NT) for x in uh}
        a_b = {x: _mm(lm[x], pair(bh[x[0]], x[1] // 2), RWKV_PREC_GRAM, _NT) for x in uh}
        side = lambda f, x: jnp.concatenate([f((x[0], 2 * x[1])), f((x[0], 2 * x[1] + 1))], axis=1)
        t_inv = dict(zip(up, _unit_lower_inverses(
            [side(lambda y: a_b[y][0:CHUNK], x) * stricts2[x[0][0]] for x in up], eye2, blk16_2, pmasks)))
        a12v = {x: _mm(side(lambda y: a_k[y], x) * sel12[x[0][0]], bd(pair(v[x[0]], x[1])), RWKV_PREC_APPLY)
                for x in up}
        a3 = {x: side(lambda y: a_b[y][CHUNK:], x) * incls2[x[0][0]] for x in up}
        gl = {u: jnp.exp(cs[u][edge_rows[u[0]]:edge_rows[u[0]] + 1, :]) for u in units}
        kb = {u: jnp.concatenate([kh[u], bh[u]], axis=0) for u in units}

        s_mat = [st_s[d] for d in dirs]
        for k in range(max(len(t) for t in t0s)):
            us = [(d, k) for d in dirs if k < len(t0s[d])]
            p_all = {u: _mm(lhs[u], s_mat[u[0]], RWKV_PREC_STATE, _NT) for u in us}
            u_p = {(u, p): _mm(t_inv[(u, p)], bd(pair(p_all[u][0:CHUNK], p) + a12v[(u, p)][0:CHUNK]), RWKV_PREC_APPLY)
                   for u in us for p in pairs}
            u_all = {u: jnp.concatenate([u_p[(u, p)] for p in pairs], axis=1) for u in us}
            y_p = {(u, p): a12v[(u, p)][CHUNK:] - _mm(a3[(u, p)], bd(u_p[(u, p)]), RWKV_PREC_APPLY)
                   for u in us for p in pairs}
            upd = {u: _mm(jnp.concatenate([v[u], -u_all[u]], axis=0), kb[u], RWKV_PREC_STATE, _TN) for u in us}
            for u in us:
                y_s[rows[u], :] += p_all[u][CHUNK:] + jnp.concatenate([y_p[(u, p)] for p in pairs], axis=1)
                s_mat[u[0]] = (s_mat[u[0]] + upd[u]) * gl[u] * blockdiag
        for d in dirs:
            st_s[d] = s_mat[d]

    def body(c, carry):
        fwd = [pl.multiple_of((c * RWKV_GROUP + k) * CHUNK, CHUNK) for k in range(RWKV_GROUP)]
        bwd = [pl.multiple_of((nchunk - 1 - c * RWKV_GROUP - k) * CHUNK, CHUNK) for k in range(RWKV_GROUP)]
        chunk_group([fwd, bwd])
        return carry

    lax.fori_loop(0, nchunk // RWKV_GROUP, body, 0)

    def post(i, carry):
        rows = pl.ds(pl.multiple_of(i * ROWS, ROWS), ROWS)
        y = y_s[rows, :]
        mean = _mm_exact_lhs(y, blockdiag_b) * (1.0 / HEAD_DIM)
        yc = y - mean
        var = _mm_exact_lhs(yc * yc, blockdiag_b) * (1.0 / HEAD_DIM)
        y = yc * lax.rsqrt(var + RWKV_GN_EPS) * gnw_ref[...] + gnb_ref[...]
        o_ref[rows, :] = (y + bonus_s[rows, :]) * g_s[rows, :]
        return carry

    lax.fori_loop(0, seq // ROWS, post, 0)


def _rwkv(u, col_block, batch, seq, mu, w0, wup, a0, aup, gup, k_k, k_a, r_k, gn_w, gn_b):
    full = lambda shape: pl.BlockSpec(shape, lambda b: (0,) * len(shape))
    tok = lambda: pltpu.VMEM((seq, GROUP_W), F32)
    tok2 = lambda: pltpu.VMEM((2, seq, GROUP_W), F32)
    return pl.pallas_call(
        functools.partial(_rwkv_kernel, seq=seq),
        grid=(batch,),
        in_specs=[pl.BlockSpec((seq, RWKV_COLS), lambda b: (b, col_block)),
                  full((2, RWKV_COLS)), full((2, GROUP_W)), full((2, RWKV_RANK2, GROUP_W)),
                  full((2, GROUP_W)), full((2, RWKV_RANK2, GROUP_W)), full((RWKV_RANK2, GROUP_W)),
                  full((1, GROUP_W)), full((1, GROUP_W)), full((1, GROUP_W)),
                  full((1, GROUP_W)), full((1, GROUP_W))],
        out_specs=pl.BlockSpec((seq, GROUP_W), lambda b: (b, 0)),
        out_shape=jax.ShapeDtypeStruct((batch * seq, GROUP_W), F32),
        scratch_shapes=[tok(), tok(), tok(), tok(), tok(), tok2(), tok2(), tok2(), tok(),
                        pltpu.VMEM((2, GROUP_W, GROUP_W), F32)],
        compiler_params=pltpu.CompilerParams(dimension_semantics=("parallel",),
                                             vmem_limit_bytes=V7X_VMEM_LIMIT),
        name="rwkv7",
    )(u, mu, w0, wup, a0, aup, gup, k_k, k_a, r_k, gn_w, gn_b)


def _natten_kernel(q_ref, k_ref, v_ref, *rest, n_rows):
    bias_refs, o_ref = rest[:-1], rest[-1]
    steps = range(NA_ROWS_PER_STEP)
    hmasks = [_head_mask(h) for h in range(N_HEADS)]
    r = [pl.program_id(1) * NA_ROWS_PER_STEP + j for j in steps]
    start = [jnp.clip(r[j] - NA_WIN_ROWS // 2, 0, n_rows - NA_WIN_ROWS) for j in steps]
    win = [pl.ds(pl.multiple_of(start[j] * GRID_W, GRID_W), NA_WIN_ROWS * GRID_W) for j in steps]
    kw = [k_ref[win[j], :].astype(BF16) for j in steps]
    vw = [v_ref[win[j], :].astype(BF16) for j in steps]
    q = [q_ref[j * GRID_W:(j + 1) * GRID_W, :] * (HEAD_DIM ** -0.5) for j in steps]
    qs = [jnp.concatenate([q[j] * hmasks[h] for h in range(N_HEADS)], axis=0).astype(BF16) for j in steps]
    s = [_dot_t(qs[j], kw[j]) + bias_refs[j][0] for j in steps]
    m = [jnp.max(s[j], axis=-1, keepdims=True) for j in steps]
    p = [jnp.exp(s[j] - m[j]) for j in steps]
    p = [p[j] / jnp.sum(p[j], axis=-1, keepdims=True) for j in steps]
    o = [_dot(p[j].astype(BF16), vw[j]) for j in steps]
    for j in steps:
        o_ref[j * GRID_W:(j + 1) * GRID_W, :] = sum(o[j][h * GRID_W:(h + 1) * GRID_W] * hmasks[h]
                                                    for h in range(N_HEADS))


def _natten(u, q_block, batch, seq, bias_tab):
    n_rows = seq // GRID_W
    nkeys = NA_WIN_ROWS * GRID_W
    nstep = n_rows // NA_ROWS_PER_STEP
    qrows = NA_ROWS_PER_STEP * GRID_W

    def bias_index(j):
        def index(b, i):
            r = i * NA_ROWS_PER_STEP + j
            start = jnp.clip(r - NA_WIN_ROWS // 2, 0, n_rows - NA_WIN_ROWS)
            return (start - r + NA_WIN_ROWS - 1, 0, 0)
        return index

    bias_specs = [pl.BlockSpec((1, N_HEADS * GRID_W, nkeys), bias_index(j)) for j in range(NA_ROWS_PER_STEP)]
    return pl.pallas_call(
        functools.partial(_natten_kernel, n_rows=n_rows),
        grid=(batch, nstep),
        in_specs=[pl.BlockSpec((qrows, GROUP_W), lambda b, i: (b * nstep + i, q_block)),
                  pl.BlockSpec((seq, GROUP_W), lambda b, i: (b, q_block + 1)),
                  pl.BlockSpec((seq, GROUP_W), lambda b, i: (b, q_block + 2))] + bias_specs,
        out_specs=pl.BlockSpec((qrows, GROUP_W), lambda b, i: (b * nstep + i, 0)),
        out_shape=jax.ShapeDtypeStruct((batch * seq, GROUP_W), F32),
        compiler_params=pltpu.CompilerParams(dimension_semantics=("parallel", "arbitrary")),
        name="natten",
    )(u, u, u, *([bias_tab] * NA_ROWS_PER_STEP))


def _natten_bias_table(rel_bias):
    ncol = 2 * NA_WIN_COLS - 1
    qc = np.arange(GRID_W)
    kc = np.arange(GRID_W)
    ws = np.clip(qc - NA_WIN_COLS // 2, 0, GRID_W - NA_WIN_COLS)
    in_win = (kc[None, :] >= ws[:, None]) & (kc[None, :] < ws[:, None] + NA_WIN_COLS)
    col_i = np.clip(kc[None, :] - qc[:, None] + NA_WIN_COLS - 1, 0, ncol - 1)
    onehot = ((col_i[None] == np.arange(ncol)[:, None, None]) & in_win[None]).astype(np.float32)
    mask_add = np.where(in_win, 0.0, NA_MASKED).astype(np.float32)
    toep = jnp.einsum('hrc,cqk->hrqk', rel_bias.astype(F32), jnp.asarray(onehot), precision=HI) + mask_add
    tab = jnp.stack([toep[:, d:d + NA_WIN_ROWS] for d in range(NA_WIN_ROWS)])
    return tab.transpose(0, 1, 3, 2, 4).reshape(NA_WIN_ROWS, N_HEADS * GRID_W, NA_WIN_ROWS * GRID_W)


def _pack_w_in(w_in):
    o_ssd = RWKV_COLS
    o_dt_end = o_ssd + GROUP_W + SSD_XBC + 8
    o_lru_end = o_dt_end + LRU_COLS
    pad = jnp.zeros(w_in.shape[:-1] + (SSD_DT_PAD - 8,), w_in.dtype)
    return jnp.concatenate([w_in[..., :o_dt_end], pad, w_in[..., o_lru_end:], w_in[..., o_dt_end:o_lru_end]],
                           axis=-1).astype(BF16)


def _pad_rank(w_up):
    z = jnp.zeros_like(w_up[0])
    return jnp.stack([jnp.concatenate([w_up[0], z], axis=0), jnp.concatenate([z, w_up[1]], axis=0)])


def _block_diag(w):
    _, nb, n, _ = w.shape
    eye = jnp.eye(nb, dtype=w.dtype)
    return jnp.einsum('dkij,kl->dkilj', w, eye).reshape(2, nb * n, nb * n)


def _dt_expander():
    e64 = np.zeros((2, SSD_DT_PAD, GROUP_W), np.float32)
    for d in range(2):
        for h in range(N_HEADS):
            e64[d, d * N_HEADS + h, h * HEAD_DIM:(h + 1) * HEAD_DIM] = 1.0
    return jnp.asarray(e64, BF16)


def kernel(x, norm1_w, w_in, rwkv_shift_mu, rwkv_w0, rwkv_w_up, rwkv_a0, rwkv_a_up, rwkv_g_up, rwkv_k_k, rwkv_k_a, rwkv_r_k, rwkv_gn_w, rwkv_gn_b, ssd_conv_w, ssd_conv_b, ssd_dt_bias, ssd_a_log, ssd_d, ssd_norm_w, lru_conv_w, lru_conv_b, lru_gate_a_w, lru_gate_a_b, lru_gate_x_w, lru_gate_x_b, lru_lambda, na_rel_bias, w_out, norm2_w, w_mlp1, w_mlp2, final_norm_w):
    batch, seq, _ = x.shape
    depth = w_in.shape[0]
    h = x.reshape(batch * seq, D_MODEL)

    w_in_p = _pack_w_in(w_in)
    w_out_b = w_out.astype(BF16)
    w1_b = w_mlp1.astype(BF16)
    w2_b = w_mlp2.astype(BF16)
    e64 = _dt_expander()
    row = lambda a: a.reshape(1, -1)
    rep = lambda a, n: jnp.repeat(a, n, axis=-1)

    for l in range(depth):
        u = _norm_inproj(h, row(norm1_w[l]), w_in_p, l)
        y_a = _rwkv(u, 0, batch, seq, rwkv_shift_mu[l], rwkv_w0[l], _pad_rank(rwkv_w_up[l]).astype(BF16),
                    rwkv_a0[l], _pad_rank(rwkv_a_up[l]).astype(BF16), rwkv_g_up[l].astype(BF16),
                    row(rwkv_k_k[l]), row(rwkv_k_a[l]), row(rwkv_r_k[l]), row(rwkv_gn_w[l]), row(rwkv_gn_b[l]))
        y_b = _ssd(u, 1, batch, seq, ssd_conv_w[l], row(ssd_conv_b[l]),
                   rep(ssd_dt_bias[l], HEAD_DIM), rep(ssd_a_log[l], HEAD_DIM), e64,
                   row(rep(ssd_d[l], HEAD_DIM)), row(ssd_norm_w[l]))
        y_c = _lru(u, (RWKV_COLS + SSD_COLS_PAD + NA_COLS) // LRU_COLS, batch, seq,
                   lru_conv_w[l], row(lru_conv_b[l]),
                   _block_diag(lru_gate_a_w[l]).astype(BF16), lru_gate_a_b[l],
                   _block_diag(lru_gate_x_w[l]).astype(BF16), lru_gate_x_b[l], lru_lambda[l])
        y_d = _natten(u, (RWKV_COLS + SSD_COLS_PAD) // GROUP_W, batch, seq, _natten_bias_table(na_rel_bias[l]))
        h = _outproj(h, (y_a, y_b, y_c, y_d), w_out_b, l)
        h = _mlp(h, row(norm2_w[l]), w1_b, w2_b, row(final_norm_w), l, final_norm=(l == depth - 1))
    return h.reshape(batch, seq, D_MODEL)
```

```python
import functools

import numpy as np
import jax
import jax.numpy as jnp
from jax import lax
from jax.experimental import pallas as pl
from jax.experimental.pallas import tpu as pltpu

F32 = jnp.float32
BF16 = jnp.bfloat16
HI = lax.Precision.HIGHEST

D_MODEL = 1024
GRID_W = 64
GROUP_W = 256
HEAD_DIM = 64
N_HEADS = GROUP_W // HEAD_DIM
D_FF = 4 * D_MODEL
NORM_EPS = 1e-5

RWKV_RANK2 = 128
RWKV_DECAY_SCALE = 0.6065306597126334
RWKV_GN_EPS = 64e-5
RWKV_COLS = 3 * GROUP_W + 3 * RWKV_RANK2

SSD_D_STATE = 128
SSD_XBC = GROUP_W + 4 * SSD_D_STATE
SSD_DT_PAD = 128
SSD_COLS_PAD = GROUP_W + SSD_XBC + SSD_DT_PAD

LRU_C = 8.0
LRU_COLS = 2 * GROUP_W
NA_COLS = 3 * GROUP_W
NA_WIN_ROWS = 8
NA_WIN_COLS = 16
NA_MASKED = -1e30
NA_ROWS_PER_STEP = 4

U_COLS = RWKV_COLS + SSD_COLS_PAD + NA_COLS + LRU_COLS
CHUNK = 64
RWKV_GROUP = 4
SSD_CHUNK = 256
ROWS = 256

V7X_VMEM_LIMIT = 60 * 1024 * 1024

RWKV_PREC_GRAM = "bf16"
RWKV_PREC_APPLY = "bf16"
RWKV_PREC_INV = "bf16"


def _dot(a, b, prec=None):
    return jnp.dot(a, b, preferred_element_type=F32, precision=prec)


def _dot_t(a, b, prec=None):
    return lax.dot_general(a, b, (((1,), (1,)), ((), ())), preferred_element_type=F32, precision=prec)


def _tdot(a, b, prec=None):
    return lax.dot_general(a, b, (((0,), (0,)), ((), ())), preferred_element_type=F32, precision=prec)


_NN = (((1,), (0,)), ((), ()))
_NT = (((1,), (1,)), ((), ()))
_TN = (((0,), (0,)), ((), ()))


def _mm(a, b, mode, dims=_NN):
    dg = lambda x, y: lax.dot_general(x, y, dims, preferred_element_type=F32)
    if mode == "hi":
        return lax.dot_general(a, b, dims, preferred_element_type=F32, precision=HI)
    ah = a.astype(BF16)
    bh = b.astype(BF16)
    if mode == "bf16":
        return dg(ah, bh)
    al = (a - ah.astype(F32)).astype(BF16)
    bl = (b - bh.astype(F32)).astype(BF16)
    return dg(ah, bh) + (dg(ah, bl) + dg(al, bh))


def _split3(x):
    x1 = x.astype(BF16)
    r1 = x - x1.astype(F32)
    x2 = r1.astype(BF16)
    x3 = (r1 - x2.astype(F32)).astype(BF16)
    return x1, x2, x3


def _mm_exact_lhs(x, w01):
    x1, x2, x3 = _split3(x)
    return (_dot(x3, w01) + _dot(x2, w01)) + _dot(x1, w01)


def _mm_exact_rhs(w01, x):
    x1, x2, x3 = _split3(x)
    return (_dot(w01, x3) + _dot(w01, x2)) + _dot(w01, x1)


def _iota(shape, dim):
    return lax.broadcasted_iota(jnp.int32, shape, dim)


def _head_mask(h, width=GROUP_W):
    lane = _iota((1, width), 1)
    return ((lane >= h * HEAD_DIM) & (lane < (h + 1) * HEAD_DIM)).astype(F32)


def _sigmoid(x):
    return 0.5 * jnp.tanh(0.5 * x) + 0.5


def _softplus(x):
    return jnp.maximum(x, 0.0) + jnp.log(1.0 + jnp.exp(-jnp.abs(x)))


def _shift_down(x, prev8, k):
    rx = pltpu.roll(x, k, 0)
    row8 = _iota((8, x.shape[1]), 0)
    head = jnp.where(row8 < k, pltpu.roll(prev8, k, 0), rx[0:8])
    return jnp.concatenate([head, rx[8:]], axis=0)


def _shift_up(x, next8, k):
    n = x.shape[0]
    rx = pltpu.roll(x, n - k, 0)
    row8 = _iota((8, x.shape[1]), 0)
    tail = jnp.where(row8 >= 8 - k, pltpu.roll(next8, 8 - k, 0), rx[n - 8:])
    return jnp.concatenate([rx[:n - 8], tail], axis=0)


def _halo_rows(ref, t0, nrows, seq, cols):
    pstart = pl.multiple_of(jnp.maximum(t0 - 8, 0), 8)
    nstart = pl.multiple_of(jnp.minimum(t0 + nrows, seq - 8), 8)
    prev8 = ref[pl.ds(pstart, 8), cols] * (t0 > 0).astype(F32)
    next8 = ref[pl.ds(nstart, 8), cols] * (t0 + nrows < seq).astype(F32)
    return prev8, next8


def _tri(n, upper):
    r = _iota((n, n), 0)
    c = _iota((n, n), 1)
    return ((c >= r) if upper else (c <= r)).astype(F32)


def _rmsnorm(x, w):
    ms = jnp.mean(x * x, axis=-1, keepdims=True)
    return x * lax.rsqrt(ms + NORM_EPS) * w


def _norm_inproj_kernel(x_ref, nw_ref, w_ref, o_ref, xn_ref):
    @pl.when(pl.program_id(1) == 0)
    def _():
        xn_ref[...] = _rmsnorm(x_ref[...], nw_ref[...]).astype(BF16)

    o_ref[...] = _dot(xn_ref[...], w_ref[...])


def _norm_inproj(h, norm_w, w_bf16, layer, tm=2048, tn=896):
    t = h.shape[0]
    n = w_bf16.shape[2]
    return pl.pallas_call(
        _norm_inproj_kernel,
        grid=(t // tm, n // tn),
        in_specs=[pl.BlockSpec((tm, D_MODEL), lambda i, j: (i, 0)),
                  pl.BlockSpec((1, D_MODEL), lambda i, j: (0, 0)),
                  pl.BlockSpec((None, D_MODEL, tn), lambda i, j: (layer, 0, j))],
        out_specs=pl.BlockSpec((tm, tn), lambda i, j: (i, j)),
        out_shape=jax.ShapeDtypeStruct((t, n), F32),
        scratch_shapes=[pltpu.VMEM((tm, D_MODEL), BF16)],
        compiler_params=pltpu.CompilerParams(dimension_semantics=("parallel", "arbitrary"),
                                             vmem_limit_bytes=V7X_VMEM_LIMIT),
        name="norm_inproj",
    )(h, norm_w, w_bf16)


def _outproj_kernel(h_ref, ya_ref, yb_ref, yc_ref, yd_ref, w_ref, o_ref):
    acc = h_ref[...]
    for g, y_ref in enumerate((ya_ref, yb_ref, yc_ref, yd_ref)):
        acc = acc + _dot(y_ref[...].astype(BF16), w_ref[g * GROUP_W:(g + 1) * GROUP_W, :])
    o_ref[...] = acc


def _outproj(h, ys, w_bf16, layer, tm=512):
    t = h.shape[0]
    yspec = pl.BlockSpec((tm, GROUP_W), lambda i: (i, 0))
    return pl.pallas_call(
        _outproj_kernel,
        grid=(t // tm,),
        in_specs=[pl.BlockSpec((tm, D_MODEL), lambda i: (i, 0)), yspec, yspec, yspec, yspec,
                  pl.BlockSpec((None, D_MODEL, D_MODEL), lambda i: (layer, 0, 0))],
        out_specs=pl.BlockSpec((tm, D_MODEL), lambda i: (i, 0)),
        out_shape=jax.ShapeDtypeStruct((t, D_MODEL), F32),
        compiler_params=pltpu.CompilerParams(dimension_semantics=("parallel",)),
        name="outproj",
    )(h, *ys, w_bf16)


def _mlp_kernel(h_ref, nw_ref, w1_ref, w2_ref, fw_ref, o_ref, xn_ref, *, final_norm):
    f = pl.program_id(1)

    @pl.when(f == 0)
    def _():
        hv = h_ref[...]
        xn_ref[...] = _rmsnorm(hv, nw_ref[...]).astype(BF16)
        o_ref[...] = hv

    m = _dot(xn_ref[...], w1_ref[...])
    a = jnp.square(jnp.maximum(m, 0.0))
    o_ref[...] += _dot(a.astype(BF16), w2_ref[...])

    if final_norm:
        @pl.when(f == pl.num_programs(1) - 1)
        def _():
            o_ref[...] = _rmsnorm(o_ref[...], fw_ref[...])


def _mlp(h, norm_w, w1_bf16, w2_bf16, final_w, layer, final_norm, tm=1024, tf=1024):
    t = h.shape[0]
    return pl.pallas_call(
        functools.partial(_mlp_kernel, final_norm=final_norm),
        grid=(t // tm, D_FF // tf),
        in_specs=[pl.BlockSpec((tm, D_MODEL), lambda i, f: (i, 0)),
                  pl.BlockSpec((1, D_MODEL), lambda i, f: (0, 0)),
                  pl.BlockSpec((None, D_MODEL, tf), lambda i, f: (layer, 0, f)),
                  pl.BlockSpec((None, tf, D_MODEL), lambda i, f: (layer, f, 0)),
                  pl.BlockSpec((1, D_MODEL), lambda i, f: (0, 0))],
        out_specs=pl.BlockSpec((tm, D_MODEL), lambda i, f: (i, 0)),
        out_shape=jax.ShapeDtypeStruct((t, D_MODEL), F32),
        scratch_shapes=[pltpu.VMEM((tm, D_MODEL), BF16)],
        compiler_params=pltpu.CompilerParams(dimension_semantics=("parallel", "arbitrary"),
                                             vmem_limit_bytes=V7X_VMEM_LIMIT),
        name="mlp",
    )(h, norm_w, w1_bf16, w2_bf16, final_w)


def _lru_kernel(u_ref, cw_ref, cb_ref, wa_ref, ba_ref, wx_ref, bx_ref, lam_ref, o_ref,
                gate_s, a_s, b_s, *, seq):
    gcols = slice(0, GROUP_W)
    xcols = slice(GROUP_W, 2 * GROUP_W)

    def pre(i, carry):
        t0 = pl.multiple_of(i * ROWS, ROWS)
        rows = pl.ds(t0, ROWS)
        xi = u_ref[rows, xcols]
        prev8, next8 = _halo_rows(u_ref, t0, ROWS, seq, xcols)
        xf = (cw_ref[0:1, :] * _shift_down(xi, prev8, 2) + cw_ref[1:2, :] * _shift_down(xi, prev8, 1)
              + cw_ref[2:3, :] * xi + cw_ref[3:4, :] * _shift_up(xi, next8, 1) + cb_ref[...])
        gi = u_ref[rows, gcols]
        gate_s[rows, :] = 0.5 * gi * (1.0 + jnp.tanh(0.7978845608028654 * (gi + 0.044715 * gi * gi * gi)))
        xfb = xf.astype(BF16)
        for d in range(2):
            rec = _sigmoid(_dot(xfb, wa_ref[d]) + ba_ref[d:d + 1, :])
            inp = _sigmoid(_dot(xfb, wx_ref[d]) + bx_ref[d:d + 1, :])
            log_a = -LRU_C * rec * _softplus(-lam_ref[d:d + 1, :])
            a_s[d, rows, :] = jnp.exp(log_a)
            b_s[d, rows, :] = jnp.sqrt(1.0 - jnp.exp(2.0 * log_a)) * inp * xf
        return carry

    lax.fori_loop(0, seq // ROWS, pre, 0)

    row8 = _iota((8, GROUP_W), 0)

    def scan8(a, b, rev):
        for s in (1, 2, 4):
            if rev:
                keep = row8 < 8 - s
                a_sh = jnp.where(keep, pltpu.roll(a, 8 - s, 0), 1.0)
                b_sh = jnp.where(keep, pltpu.roll(b, 8 - s, 0), 0.0)
            else:
                keep = row8 >= s
                a_sh = jnp.where(keep, pltpu.roll(a, s, 0), 1.0)
                b_sh = jnp.where(keep, pltpu.roll(b, s, 0), 0.0)
            b = a * b_sh + b
            a = a * a_sh
        return a, b

    nblk = seq // 8

    def scan(i, carry):
        cf, cb = carry
        rf = pl.ds(pl.multiple_of(i * 8, 8), 8)
        rb = pl.ds(pl.multiple_of((nblk - 1 - i) * 8, 8), 8)
        af, bf = scan8(a_s[0, rf, :], b_s[0, rf, :], False)
        hf = bf + af * cf
        a_s[0, rf, :] = hf
        ab, bb = scan8(a_s[1, rb, :], b_s[1, rb, :], True)
        hb = bb + ab * cb
        a_s[1, rb, :] = hb
        return hf[7:8, :], hb[0:1, :]

    zero = jnp.zeros((1, GROUP_W), F32)
    lax.fori_loop(0, nblk, scan, (zero, zero))

    def post(i, carry):
        rows = pl.ds(pl.multiple_of(i * ROWS, ROWS), ROWS)
        o_ref[rows, :] = gate_s[rows, :] * (a_s[0, rows, :] + a_s[1, rows, :])
        return carry

    lax.fori_loop(0, seq // ROWS, post, 0)


def _lru(u, col_block, batch, seq, cw, cb, wa, ba, wx, bx, lam):
    full = lambda shape: pl.BlockSpec(shape, lambda b: (0,) * len(shape))
    return pl.pallas_call(
        functools.partial(_lru_kernel, seq=seq),
        grid=(batch,),
        in_specs=[pl.BlockSpec((seq, LRU_COLS), lambda b: (b, col_block)),
                  full((4, GROUP_W)), full((1, GROUP_W)),
                  full((2, GROUP_W, GROUP_W)), full((2, GROUP_W)),
                  full((2, GROUP_W, GROUP_W)), full((2, GROUP_W)), full((2, GROUP_W))],
        out_specs=pl.BlockSpec((seq, GROUP_W), lambda b: (b, 0)),
        out_shape=jax.ShapeDtypeStruct((batch * seq, GROUP_W), F32),
        scratch_shapes=[pltpu.VMEM((seq, GROUP_W), F32),
                        pltpu.VMEM((2, seq, GROUP_W), F32),
                        pltpu.VMEM((2, seq, GROUP_W), F32)],
        compiler_params=pltpu.CompilerParams(dimension_semantics=("parallel",),
                                             vmem_limit_bytes=V7X_VMEM_LIMIT),
        name="rglru",
    )(u, cw, cb, wa, ba, wx, bx, lam)


def _ssd_kernel(u_ref, cw_ref, cb_ref, dtb64_ref, alog64_ref, e64_ref,
                dskip_ref, nw_ref, o_ref, xc_s, y_s, st_s, *, seq):
    zcols = slice(0, GROUP_W)
    xbc_cols = slice(GROUP_W, GROUP_W + SSD_XBC)
    dt_cols = slice(GROUP_W + SSD_XBC, SSD_COLS_PAD)

    def pre(i, carry):
        t0 = pl.multiple_of(i * ROWS, ROWS)
        rows = pl.ds(t0, ROWS)
        xi = u_ref[rows, xbc_cols]
        prev8, next8 = _halo_rows(u_ref, t0, ROWS, seq, xbc_cols)
        xc = (cw_ref[0:1, :] * _shift_down(xi, prev8, 2) + cw_ref[1:2, :] * _shift_down(xi, prev8, 1)
              + cw_ref[2:3, :] * xi + cw_ref[3:4, :] * _shift_up(xi, next8, 1) + cb_ref[...])
        xc = xc * _sigmoid(xc)
        xc_s[rows, :] = xc
        y_s[rows, :] = dskip_ref[...] * xc[:, 0:GROUP_W]
        return carry

    lax.fori_loop(0, seq // ROWS, pre, 0)

    st_s[...] = jnp.zeros_like(st_s)
    n = SSD_CHUNK
    nchunk = seq // n
    rr = _iota((n, n), 0)
    cc = _iota((n, n), 1)
    incls = [cc <= rr, cc >= rr]
    tris = [m.astype(BF16) for m in incls]
    edge_rows = [n - 1, 0]
    hmasks = [_head_mask(h) for h in range(N_HEADS)]
    dirs = (0, 1)
    groups = (0, 1)
    heads = range(N_HEADS)
    gsl = [slice(g * SSD_D_STATE, (g + 1) * SSD_D_STATE) for g in groups]

    def chunk_pair(t0s):
        rows = [pl.ds(t0, n) for t0 in t0s]
        xs = [xc_s[rows[d], 0:GROUP_W] for d in dirs]
        bm = [xc_s[rows[d], GROUP_W:2 * GROUP_W].astype(BF16) for d in dirs]
        cm = [xc_s[rows[d], 2 * GROUP_W:3 * GROUP_W].astype(BF16) for d in dirs]
        dtraw = [u_ref[rows[d], dt_cols] for d in dirs]
        dt64 = [_softplus(_mm_exact_lhs(dtraw[d], e64_ref[d]) + dtb64_ref[d:d + 1, :]) for d in dirs]
        adt64 = [dt64[d] * (-jnp.exp(alog64_ref[d:d + 1, :])) for d in dirs]
        xdt = [xs[d] * dt64[d] for d in dirs]
        xdt_b = [x.astype(BF16) for x in xdt]
        cs64 = [_mm_exact_rhs(tris[d], adt64[d]) for d in dirs]
        scores = [[_dot_t(cm[d][:, gsl[g]], bm[d][:, gsl[g]]) for g in groups] for d in dirs]
        y_off = [jnp.concatenate([_dot(cm[d][:, gsl[g]], st_s[d, g].astype(BF16)) for g in groups], axis=1)
                 for d in dirs]
        y = [y_off[d] * jnp.exp(cs64[d]) for d in dirs]
        for d in dirs:
            for h in heads:
                cb = jnp.broadcast_to(cs64[d][:, h * HEAD_DIM:h * HEAD_DIM + 1], (n, n))
                seg = jnp.where(incls[d], cb - cb.T, NA_MASKED)
                m = (scores[d][h // 2] * jnp.exp(seg)).astype(BF16)
                y[d] = y[d] + _dot(m, xdt_b[d]) * hmasks[h]
        for d in dirs:
            y_s[rows[d], :] += y[d]
            edge = cs64[d][edge_rows[d]:edge_rows[d] + 1, :]
            xd = (xdt[d] * jnp.exp(edge - cs64[d])).astype(BF16)
            egrow = jnp.exp(edge)
            for g in groups:
                st_s[d, g] = st_s[d, g] * egrow[:, gsl[g]] + _tdot(bm[d][:, gsl[g]], xd[:, gsl[g]])

    def body(c, carry):
        chunk_pair([pl.multiple_of(c * n, n), pl.multiple_of((nchunk - 1 - c) * n, n)])
        return carry

    lax.fori_loop(0, nchunk, body, 0)

    def post(i, carry):
        rows = pl.ds(pl.multiple_of(i * ROWS, ROWS), ROWS)
        z = u_ref[rows, zcols]
        y = y_s[rows, :] * (z * _sigmoid(z))
        o_ref[rows, :] = _rmsnorm(y, nw_ref[...])
        return carry

    lax.fori_loop(0, seq // ROWS, post, 0)


def _ssd(u, col_block, batch, seq, cw, cb, dtb64, alog64, e64, dskip, nw):
    full = lambda shape: pl.BlockSpec(shape, lambda b: (0,) * len(shape))
    return pl.pallas_call(
        functools.partial(_ssd_kernel, seq=seq),
        grid=(batch,),
        in_specs=[pl.BlockSpec((seq, SSD_COLS_PAD), lambda b: (b, col_block)),
                  full((4, SSD_XBC)), full((1, SSD_XBC)),
                  full((2, GROUP_W)), full((2, GROUP_W)), full((2, SSD_DT_PAD, GROUP_W)),
                  full((1, GROUP_W)), full((1, GROUP_W))],
        out_specs=pl.BlockSpec((seq, GROUP_W), lambda b: (b, 0)),
        out_shape=jax.ShapeDtypeStruct((batch * seq, GROUP_W), F32),
        scratch_shapes=[pltpu.VMEM((seq, SSD_XBC), F32),
                        pltpu.VMEM((seq, GROUP_W), F32),
                        pltpu.VMEM((2, 2, SSD_D_STATE, 128), F32)],
        compiler_params=pltpu.CompilerParams(dimension_semantics=("parallel",),
                                             vmem_limit_bytes=V7X_VMEM_LIMIT),
        name="ssd",
    )(u, cw, cb, dtb64, alog64, e64, dskip, nw)


def _pair_blockdiag(x, pmasks):
    return jnp.concatenate([x * pmasks[0], x * pmasks[1]], axis=0)


def _unit_lower_inverses(n_mats, eye, blk_mask, pmasks):
    mm = lambda a, b: _mm(a, _pair_blockdiag(b, pmasks), RWKV_PREC_INV)
    ps = [-(n * blk_mask) for n in n_mats]
    tds = [eye + p for p in ps]
    for _ in range(3):
        ps = [mm(p, p) for p in ps]
        yield None
        tds = [t + mm(t, p) for t, p in zip(tds, ps)]
        yield None
    ms = [-mm(t, n * (1.0 - blk_mask)) for t, n in zip(tds, n_mats)]
    yield None
    m2s = [mm(m, m) for m in ms]
    yield None
    tos = [eye + m for m in ms]
    tos = [t + mm(t, m2) for t, m2 in zip(tos, m2s)]
    yield None
    yield [mm(to, td) for to, td in zip(tos, tds)]


def _trace_interleaved(*gens):
    live = list(gens)
    while live:
        for gen in list(live):
            try:
                next(gen)
            except StopIteration:
                live.remove(gen)


def _rwkv_kernel(u_ref, mu_ref, w0_ref, wup_ref, a0_ref, aup_ref, gup_ref, kk_ref, ka_ref, rk_ref,
                 gnw_ref, gnb_ref, o_ref,
                 r_s, v_s, n_s, g_s, lw_s, kd_s, b_s, st_s,
                 h_lhs, h_kb, h_gl, h_t, h_a12v, h_a3, *, seq):
    allc = slice(0, RWKV_COLS)
    lane_r = _iota((GROUP_W, GROUP_W), 0) >> 6
    lane_c = _iota((GROUP_W, GROUP_W), 1) >> 6
    blockdiag = (lane_r == lane_c).astype(F32)
    blockdiag_b = blockdiag.astype(BF16)

    def pre(i, carry):
        t0 = pl.multiple_of(i * ROWS, ROWS)
        rows = pl.ds(t0, ROWS)
        x = u_ref[rows, allc]
        prev8, next8 = _halo_rows(u_ref, t0, ROWS, seq, allc)
        prev = _shift_down(x, prev8, 1)
        nxt = _shift_up(x, next8, 1)
        x = x + mu_ref[0:1, :] * (prev - x) + mu_ref[1:2, :] * (nxt - x)
        r = x[:, 0:GROUP_W]
        k = x[:, GROUP_W:2 * GROUP_W]
        v = x[:, 2 * GROUP_W:3 * GROUP_W]
        wd = jnp.tanh(x[:, 3 * GROUP_W:3 * GROUP_W + RWKV_RANK2]).astype(BF16)
        ad = x[:, 3 * GROUP_W + RWKV_RANK2:3 * GROUP_W + 2 * RWKV_RANK2].astype(BF16)
        gd = x[:, 3 * GROUP_W + 2 * RWKV_RANK2:RWKV_COLS]
        kk = k * kk_ref[...]
        kk = kk * lax.rsqrt(_mm_exact_lhs(kk * kk, blockdiag_b) + 1e-12)
        r_s[rows, :] = r
        v_s[rows, :] = v
        n_s[rows, :] = kk
        g_s[rows, :] = _dot(_sigmoid(gd).astype(BF16), gup_ref[...])
        for d in range(2):
            z_w = w0_ref[d:d + 1, :] + _dot(wd, wup_ref[d])
            lw_s[d, rows, :] = -RWKV_DECAY_SCALE * _sigmoid(z_w)
            alpha = _sigmoid(a0_ref[d:d + 1, :] + _dot(ad, aup_ref[d]))
            kd_s[d, rows, :] = k * (1.0 + (alpha - 1.0) * ka_ref[...])
            b_s[d, rows, :] = alpha * kk
        o_ref[rows, :] = jnp.zeros((ROWS, GROUP_W), F32)
        return carry

    lax.fori_loop(0, seq // ROWS, pre, 0)

    st_s[...] = jnp.zeros_like(st_s)
    nchunk = seq // CHUNK
    rr = _iota((CHUNK, CHUNK), 0)
    cc = _iota((CHUNK, CHUNK), 1)
    eye = (rr == cc).astype(F32)
    blk16 = ((rr >> 4) == (cc >> 4)).astype(F32)
    hmasks = [_head_mask(h) for h in range(N_HEADS)]

    tris = [_tri(CHUNK, False).astype(BF16), _tri(CHUNK, True).astype(BF16)]
    stricts = [(cc < rr).astype(F32), (cc > rr).astype(F32)]
    incls = [(cc <= rr).astype(F32), (cc >= rr).astype(F32)]
    edge_rows = [CHUNK - 1, 0]
    dirs = (0, 1)
    heads = range(N_HEADS)
    lane128 = _iota((1, 128), 1)
    pmasks = [(lane128 < HEAD_DIM).astype(F32), (lane128 >= HEAD_DIM).astype(F32)]
    pairs = (0, 1)
    twice = lambda m: jnp.concatenate([m, m], axis=1)
    stricts2 = [twice(m) for m in stricts]
    incls2 = [twice(m) for m in incls]
    eye2 = twice(eye)
    blk16_2 = twice(blk16)
    sel12 = [jnp.concatenate([stricts2[d], incls2[d]], axis=0) for d in dirs]

    units = [(d, k) for d in dirs for k in range(RWKV_GROUP)]
    uidx = {u: j for j, u in enumerate(units)}
    ngroup = nchunk // RWKV_GROUP
    pair = lambda a, p: a[:, p * 128:(p + 1) * 128]
    bd = lambda a: _pair_blockdiag(a, pmasks)

    def group_rows(g):
        t0 = {(0, k): (g * RWKV_GROUP + k) * CHUNK for k in range(RWKV_GROUP)}
        t0.update({(1, k): (nchunk - 1 - g * RWKV_GROUP - k) * CHUNK for k in range(RWKV_GROUP)})
        return {u: pl.ds(pl.multiple_of(t0[u], CHUNK), CHUNK) for u in units}

    def prepare(g, slot):
        rows = group_rows(g)
        lw = {u: lw_s[u[0], rows[u], :] for u in units}
        cs = {u: _mm_exact_rhs(tris[u[0]], lw[u]) for u in units}
        yield
        ginv = {u: jnp.exp(-cs[u]) for u in units}
        kkt = {u: n_s[rows[u], :] * jnp.exp(cs[u] - lw[u]) for u in units}
        rt = {u: r_s[rows[u], :] * jnp.exp(cs[u]) for u in units}
        kh = {u: kd_s[u[0], rows[u], :] * ginv[u] for u in units}
        bh = {u: b_s[u[0], rows[u], :] * ginv[u] for u in units}
        v = {u: v_s[rows[u], :] for u in units}
        lhs = {u: jnp.concatenate([kkt[u], rt[u]], axis=0) for u in units}
        uh = [(u, h) for u in units for h in heads]
        up = [(u, p) for u in units for p in pairs]
        yield
        lm = {x: pair(lhs[x[0]], x[1] // 2) * pmasks[x[1] % 2] for x in uh}
        a_k = {x: _mm(lm[x], pair(kh[x[0]], x[1] // 2), RWKV_PREC_GRAM, _NT) for x in uh}
        a_b = {x: _mm(lm[x], pair(bh[x[0]], x[1] // 2), RWKV_PREC_GRAM, _NT) for x in uh}
        side = lambda f, x: jnp.concatenate([f((x[0], 2 * x[1])), f((x[0], 2 * x[1] + 1))], axis=1)
        yield
        a12v = {x: _mm(side(lambda y: a_k[y], x) * sel12[x[0][0]], bd(pair(v[x[0]], x[1])), RWKV_PREC_APPLY)
                for x in up}
        yield
        t_list = None
        for t_list in _unit_lower_inverses(
                [side(lambda y: a_b[y][0:CHUNK], x) * stricts2[x[0][0]] for x in up], eye2, blk16_2, pmasks):
            yield
        t_inv = dict(zip(up, t_list))
        for u in units:
            j = uidx[u]
            h_lhs[slot, j] = lhs[u].astype(BF16)
            h_kb[slot, j] = jnp.concatenate([kh[u], bh[u]], axis=0).astype(BF16)
            edge = cs[u][edge_rows[u[0]]:edge_rows[u[0]] + 1, :]
            h_gl[slot, j] = jnp.broadcast_to(jnp.exp(edge), (8, GROUP_W))
            for p in pairs:
                h_t[slot, j, p] = t_inv[(u, p)].astype(BF16)
                h_a12v[slot, j, p] = a12v[(u, p)]
                h_a3[slot, j, p] = (side(lambda y: a_b[y][CHUNK:], (u, p)) * incls2[u[0]]).astype(BF16)

    def advance(g, slot):
        rows = group_rows(g)
        s_mat = [st_s[d] for d in dirs]
        for k in range(RWKV_GROUP):
            us = [(d, k) for d in dirs]
            p_all = {u: _dot_t(h_lhs[slot, uidx[u]], s_mat[u[0]].astype(BF16)) for u in us}
            yield
            u_p = {(u, p): _dot(h_t[slot, uidx[u], p],
                                bd(pair(p_all[u][0:CHUNK], p) + h_a12v[slot, uidx[u], p, 0:CHUNK, :]).astype(BF16))
                   for u in us for p in pairs}
            yield
            y_p = {(u, p): h_a12v[slot, uidx[u], p, CHUNK:, :]
                   - _dot(h_a3[slot, uidx[u], p], bd(u_p[(u, p)]).astype(BF16)) for u in us for p in pairs}
            u_all = {u: jnp.concatenate([u_p[(u, p)] for p in pairs], axis=1) for u in us}
            upd = {u: _tdot(jnp.concatenate([v_s[rows[u], :], -u_all[u]], axis=0).astype(BF16), h_kb[slot, uidx[u]])
                   for u in us}
            for u in us:
                o_ref[rows[u], :] += p_all[u][CHUNK:] + jnp.concatenate([y_p[(u, p)] for p in pairs], axis=1)
                s_mat[u[0]] = (s_mat[u[0]] + upd[u]) * h_gl[slot, uidx[u], 0:1, :] * blockdiag
            yield
        for d in dirs:
            st_s[d] = s_mat[d]

    _trace_interleaved(prepare(0, 0))

    def body(i, carry):
        _trace_interleaved(advance(2 * i, 0), prepare(2 * i + 1, 1))
        _trace_interleaved(advance(2 * i + 1, 1), prepare(jnp.minimum(2 * i + 2, ngroup - 1), 0))
        return carry

    lax.fori_loop(0, ngroup // 2, body, 0)

    def post(i, carry):
        rows = pl.ds(pl.multiple_of(i * ROWS, ROWS), ROWS)
        y = o_ref[rows, :]
        mean = _mm_exact_lhs(y, blockdiag_b) * (1.0 / HEAD_DIM)
        yc = y - mean
        var = _mm_exact_lhs(yc * yc, blockdiag_b) * (1.0 / HEAD_DIM)
        y = yc * lax.rsqrt(var + RWKV_GN_EPS) * gnw_ref[...] + gnb_ref[...]
        rk = r_s[rows, :] * (kd_s[0, rows, :] + kd_s[1, rows, :]) * rk_ref[...]
        bonus = _mm_exact_lhs(rk, blockdiag_b) * v_s[rows, :]
        o_ref[rows, :] = (y + bonus) * g_s[rows, :]
        return carry

    lax.fori_loop(0, seq // ROWS, post, 0)


def _rwkv(u, col_block, batch, seq, mu, w0, wup, a0, aup, gup, k_k, k_a, r_k, gn_w, gn_b):
    full = lambda shape: pl.BlockSpec(shape, lambda b: (0,) * len(shape))
    tok = lambda: pltpu.VMEM((seq, GROUP_W), F32)
    tok2 = lambda: pltpu.VMEM((2, seq, GROUP_W), F32)
    nunit = 2 * RWKV_GROUP
    return pl.pallas_call(
        functools.partial(_rwkv_kernel, seq=seq),
        grid=(batch,),
        in_specs=[pl.BlockSpec((seq, RWKV_COLS), lambda b: (b, col_block)),
                  full((2, RWKV_COLS)), full((2, GROUP_W)), full((2, RWKV_RANK2, GROUP_W)),
                  full((2, GROUP_W)), full((2, RWKV_RANK2, GROUP_W)), full((RWKV_RANK2, GROUP_W)),
                  full((1, GROUP_W)), full((1, GROUP_W)), full((1, GROUP_W)),
                  full((1, GROUP_W)), full((1, GROUP_W))],
        out_specs=pl.BlockSpec((seq, GROUP_W), lambda b: (b, 0)),
        out_shape=jax.ShapeDtypeStruct((batch * seq, GROUP_W), F32),
        scratch_shapes=[tok(), tok(), tok(), tok(), tok2(), tok2(), tok2(),
                        pltpu.VMEM((2, GROUP_W, GROUP_W), F32),
                        pltpu.VMEM((2, nunit, 2 * CHUNK, GROUP_W), BF16),
                        pltpu.VMEM((2, nunit, 2 * CHUNK, GROUP_W), BF16),
                        pltpu.VMEM((2, nunit, 8, GROUP_W), F32),
                        pltpu.VMEM((2, nunit, 2, CHUNK, 128), BF16),
                        pltpu.VMEM((2, nunit, 2, 2 * CHUNK, 128), F32),
                        pltpu.VMEM((2, nunit, 2, CHUNK, 128), BF16)],
        compiler_params=pltpu.CompilerParams(dimension_semantics=("parallel",),
                                             vmem_limit_bytes=V7X_VMEM_LIMIT),
        name="rwkv7",
    )(u, mu, w0, wup, a0, aup, gup, k_k, k_a, r_k, gn_w, gn_b)


def _natten_kernel(q_ref, k_ref, v_ref, *rest, n_rows):
    bias_refs, o_ref = rest[:-1], rest[-1]
    steps = range(NA_ROWS_PER_STEP)
    hmasks = [_head_mask(h) for h in range(N_HEADS)]
    r = [pl.program_id(1) * NA_ROWS_PER_STEP + j for j in steps]
    start = [jnp.clip(r[j] - NA_WIN_ROWS // 2, 0, n_rows - NA_WIN_ROWS) for j in steps]
    win = [pl.ds(pl.multiple_of(start[j] * GRID_W, GRID_W), NA_WIN_ROWS * GRID_W) for j in steps]
    kw = [k_ref[win[j], :].astype(BF16) for j in steps]
    vw = [v_ref[win[j], :].astype(BF16) for j in steps]
    q = [q_ref[j * GRID_W:(j + 1) * GRID_W, :] * (HEAD_DIM ** -0.5) for j in steps]
    qs = [jnp.concatenate([q[j] * hmasks[h] for h in range(N_HEADS)], axis=0).astype(BF16) for j in steps]
    s = [_dot_t(qs[j], kw[j]) + bias_refs[j][0] for j in steps]
    m = [jnp.max(s[j], axis=-1, keepdims=True) for j in steps]
    p = [jnp.exp(s[j] - m[j]) for j in steps]
    p = [p[j] / jnp.sum(p[j], axis=-1, keepdims=True) for j in steps]
    o = [_dot(p[j].astype(BF16), vw[j]) for j in steps]
    for j in steps:
        o_ref[j * GRID_W:(j + 1) * GRID_W, :] = sum(o[j][h * GRID_W:(h + 1) * GRID_W] * hmasks[h]
                                                    for h in range(N_HEADS))


def _natten(u, q_block, batch, seq, bias_tab):
    n_rows = seq // GRID_W
    nkeys = NA_WIN_ROWS * GRID_W
    nstep = n_rows // NA_ROWS_PER_STEP
    qrows = NA_ROWS_PER_STEP * GRID_W

    def bias_index(j):
        def index(b, i):
            r = i * NA_ROWS_PER_STEP + j
            start = jnp.clip(r - NA_WIN_ROWS // 2, 0, n_rows - NA_WIN_ROWS)
            return (start - r + NA_WIN_ROWS - 1, 0, 0)
        return index

    bias_specs = [pl.BlockSpec((1, N_HEADS * GRID_W, nkeys), bias_index(j)) for j in range(NA_ROWS_PER_STEP)]
    return pl.pallas_call(
        functools.partial(_natten_kernel, n_rows=n_rows),
        grid=(batch, nstep),
        in_specs=[pl.BlockSpec((qrows, GROUP_W), lambda b, i: (b * nstep + i, q_block)),
                  pl.BlockSpec((seq, GROUP_W), lambda b, i: (b, q_block + 1)),
                  pl.BlockSpec((seq, GROUP_W), lambda b, i: (b, q_block + 2))] + bias_specs,
        out_specs=pl.BlockSpec((qrows, GROUP_W), lambda b, i: (b * nstep + i, 0)),
        out_shape=jax.ShapeDtypeStruct((batch * seq, GROUP_W), F32),
        compiler_params=pltpu.CompilerParams(dimension_semantics=("parallel", "arbitrary")),
        name="natten",
    )(u, u, u, *([bias_tab] * NA_ROWS_PER_STEP))


def _natten_bias_table(rel_bias):
    ncol = 2 * NA_WIN_COLS - 1
    qc = np.arange(GRID_W)
    kc = np.arange(GRID_W)
    ws = np.clip(qc - NA_WIN_COLS // 2, 0, GRID_W - NA_WIN_COLS)
    in_win = (kc[None, :] >= ws[:, None]) & (kc[None, :] < ws[:, None] + NA_WIN_COLS)
    col_i = np.clip(kc[None, :] - qc[:, None] + NA_WIN_COLS - 1, 0, ncol - 1)
    onehot = ((col_i[None] == np.arange(ncol)[:, None, None]) & in_win[None]).astype(np.float32)
    mask_add = np.where(in_win, 0.0, NA_MASKED).astype(np.float32)
    toep = jnp.einsum('hrc,cqk->hrqk', rel_bias.astype(F32), jnp.asarray(onehot), precision=HI) + mask_add
    tab = jnp.stack([toep[:, d:d + NA_WIN_ROWS] for d in range(NA_WIN_ROWS)])
    return tab.transpose(0, 1, 3, 2, 4).reshape(NA_WIN_ROWS, N_HEADS * GRID_W, NA_WIN_ROWS * GRID_W)


def _pack_w_in(w_in):
    o_ssd = RWKV_COLS
    o_dt_end = o_ssd + GROUP_W + SSD_XBC + 8
    o_lru_end = o_dt_end + LRU_COLS
    pad = jnp.zeros(w_in.shape[:-1] + (SSD_DT_PAD - 8,), w_in.dtype)
    return jnp.concatenate([w_in[..., :o_dt_end], pad, w_in[..., o_lru_end:], w_in[..., o_dt_end:o_lru_end]],
                           axis=-1).astype(BF16)


def _pad_rank(w_up):
    z = jnp.zeros_like(w_up[0])
    return jnp.stack([jnp.concatenate([w_up[0], z], axis=0), jnp.concatenate([z, w_up[1]], axis=0)])


def _block_diag(w):
    _, nb, n, _ = w.shape
    eye = jnp.eye(nb, dtype=w.dtype)
    return jnp.einsum('dkij,kl->dkilj', w, eye).reshape(2, nb * n, nb * n)


def _dt_expander():
    e64 = np.zeros((2, SSD_DT_PAD, GROUP_W), np.float32)
    for d in range(2):
        for h in range(N_HEADS):
            e64[d, d * N_HEADS + h, h * HEAD_DIM:(h + 1) * HEAD_DIM] = 1.0
    return jnp.asarray(e64, BF16)


def kernel(x, norm1_w, w_in, rwkv_shift_mu, rwkv_w0, rwkv_w_up, rwkv_a0, rwkv_a_up, rwkv_g_up, rwkv_k_k, rwkv_k_a, rwkv_r_k, rwkv_gn_w, rwkv_gn_b, ssd_conv_w, ssd_conv_b, ssd_dt_bias, ssd_a_log, ssd_d, ssd_norm_w, lru_conv_w, lru_conv_b, lru_gate_a_w, lru_gate_a_b, lru_gate_x_w, lru_gate_x_b, lru_lambda, na_rel_bias, w_out, norm2_w, w_mlp1, w_mlp2, final_norm_w):
    batch, seq, _ = x.shape
    depth = w_in.shape[0]
    h = x.reshape(batch * seq, D_MODEL)

    w_in_p = _pack_w_in(w_in)
    w_out_b = w_out.astype(BF16)
    w1_b = w_mlp1.astype(BF16)
    w2_b = w_mlp2.astype(BF16)
    e64 = _dt_expander()
    row = lambda a: a.reshape(1, -1)
    rep = lambda a, n: jnp.repeat(a, n, axis=-1)

    for l in range(depth):
        u = _norm_inproj(h, row(norm1_w[l]), w_in_p, l)
        y_a = _rwkv(u, 0, batch, seq, rwkv_shift_mu[l], rwkv_w0[l], _pad_rank(rwkv_w_up[l]).astype(BF16),
                    rwkv_a0[l], _pad_rank(rwkv_a_up[l]).astype(BF16), rwkv_g_up[l].astype(BF16),
                    row(rwkv_k_k[l]), row(rwkv_k_a[l]), row(rwkv_r_k[l]), row(rwkv_gn_w[l]), row(rwkv_gn_b[l]))
        y_b = _ssd(u, 1, batch, seq, ssd_conv_w[l], row(ssd_conv_b[l]),
                   rep(ssd_dt_bias[l], HEAD_DIM), rep(ssd_a_log[l], HEAD_DIM), e64,
                   row(rep(ssd_d[l], HEAD_DIM)), row(ssd_norm_w[l]))
        y_c = _lru(u, (RWKV_COLS + SSD_COLS_PAD + NA_COLS) // LRU_COLS, batch, seq,
                   lru_conv_w[l], row(lru_conv_b[l]),
                   _block_diag(lru_gate_a_w[l]).astype(BF16), lru_gate_a_b[l],
                   _block_diag(lru_gate_x_w[l]).astype(BF16), lru_gate_x_b[l], lru_lambda[l])
        y_d = _natten(u, (RWKV_COLS + SSD_COLS_PAD) // GROUP_W, batch, seq, _natten_bias_table(na_rel_bias[l]))
        h = _outproj(h, (y_a, y_b, y_c, y_d), w_out_b, l)
        h = _mlp(h, row(norm2_w[l]), w1_b, w2_b, row(final_norm_w), l, final_norm=(l == depth - 1))
    return h.reshape(batch, seq, D_MODEL)
```

```python
import functools

import numpy as np
import jax
import jax.numpy as jnp
from jax import lax
from jax.experimental import pallas as pl
from jax.experimental.pallas import tpu as pltpu

F32 = jnp.float32
BF16 = jnp.bfloat16
HI = lax.Precision.HIGHEST

D_MODEL = 1024
GRID_W = 64
GROUP_W = 256
HEAD_DIM = 64
N_HEADS = GROUP_W // HEAD_DIM
D_FF = 4 * D_MODEL
NORM_EPS = 1e-5

RWKV_RANK2 = 128
RWKV_DECAY_SCALE = 0.6065306597126334
RWKV_GN_EPS = 64e-5
RWKV_COLS = 3 * GROUP_W + 3 * RWKV_RANK2

SSD_D_STATE = 128
SSD_XBC = GROUP_W + 4 * SSD_D_STATE
SSD_DT_PAD = 128
SSD_COLS_PAD = GROUP_W + SSD_XBC + SSD_DT_PAD

LRU_C = 8.0
LRU_COLS = 2 * GROUP_W
NA_COLS = 3 * GROUP_W
NA_WIN_ROWS = 8
NA_WIN_COLS = 16
NA_MASKED = -1e30
NA_ROWS_PER_STEP = 8

U_COLS = RWKV_COLS + SSD_COLS_PAD + NA_COLS + LRU_COLS
CHUNK = 64
RWKV_GROUP = 4
SSD_CHUNK = 256
ROWS = 256

V7X_VMEM_LIMIT = 60 * 1024 * 1024

RWKV_PREC_GRAM = "bf16"
RWKV_PREC_APPLY = "bf16"
RWKV_PREC_INV = "bf16"


def _dot(a, b, prec=None):
    return jnp.dot(a, b, preferred_element_type=F32, precision=prec)


def _dot_t(a, b, prec=None):
    return lax.dot_general(a, b, (((1,), (1,)), ((), ())), preferred_element_type=F32, precision=prec)


def _tdot(a, b, prec=None):
    return lax.dot_general(a, b, (((0,), (0,)), ((), ())), preferred_element_type=F32, precision=prec)


_NN = (((1,), (0,)), ((), ()))
_NT = (((1,), (1,)), ((), ()))
_TN = (((0,), (0,)), ((), ()))


def _mm(a, b, mode, dims=_NN):
    dg = lambda x, y: lax.dot_general(x, y, dims, preferred_element_type=F32)
    if mode == "hi":
        return lax.dot_general(a, b, dims, preferred_element_type=F32, precision=HI)
    ah = a.astype(BF16)
    bh = b.astype(BF16)
    if mode == "bf16":
        return dg(ah, bh)
    al = (a - ah.astype(F32)).astype(BF16)
    bl = (b - bh.astype(F32)).astype(BF16)
    return dg(ah, bh) + (dg(ah, bl) + dg(al, bh))


def _split3(x):
    x1 = x.astype(BF16)
    r1 = x - x1.astype(F32)
    x2 = r1.astype(BF16)
    x3 = (r1 - x2.astype(F32)).astype(BF16)
    return x1, x2, x3


def _mm_exact_lhs(x, w01):
    x1, x2, x3 = _split3(x)
    return (_dot(x3, w01) + _dot(x2, w01)) + _dot(x1, w01)


def _mm_exact_rhs(w01, x):
    x1, x2, x3 = _split3(x)
    return (_dot(w01, x3) + _dot(w01, x2)) + _dot(w01, x1)


def _iota(shape, dim):
    return lax.broadcasted_iota(jnp.int32, shape, dim)


def _head_mask(h, width=GROUP_W):
    lane = _iota((1, width), 1)
    return ((lane >= h * HEAD_DIM) & (lane < (h + 1) * HEAD_DIM)).astype(F32)


def _sigmoid(x):
    return 0.5 * jnp.tanh(0.5 * x) + 0.5


def _softplus(x):
    return jnp.maximum(x, 0.0) + jnp.log(1.0 + jnp.exp(-jnp.abs(x)))


def _shift_down(x, prev8, k):
    rx = pltpu.roll(x, k, 0)
    row8 = _iota((8, x.shape[1]), 0)
    head = jnp.where(row8 < k, pltpu.roll(prev8, k, 0), rx[0:8])
    return jnp.concatenate([head, rx[8:]], axis=0)


def _shift_up(x, next8, k):
    n = x.shape[0]
    rx = pltpu.roll(x, n - k, 0)
    row8 = _iota((8, x.shape[1]), 0)
    tail = jnp.where(row8 >= 8 - k, pltpu.roll(next8, 8 - k, 0), rx[n - 8:])
    return jnp.concatenate([rx[:n - 8], tail], axis=0)


def _halo_rows(ref, t0, nrows, seq, cols):
    pstart = pl.multiple_of(jnp.maximum(t0 - 8, 0), 8)
    nstart = pl.multiple_of(jnp.minimum(t0 + nrows, seq - 8), 8)
    prev8 = ref[pl.ds(pstart, 8), cols] * (t0 > 0).astype(F32)
    next8 = ref[pl.ds(nstart, 8), cols] * (t0 + nrows < seq).astype(F32)
    return prev8, next8


def _tri(n, upper):
    r = _iota((n, n), 0)
    c = _iota((n, n), 1)
    return ((c >= r) if upper else (c <= r)).astype(F32)


def _rmsnorm(x, w):
    ms = jnp.mean(x * x, axis=-1, keepdims=True)
    return x * lax.rsqrt(ms + NORM_EPS) * w


def _norm_inproj_kernel(x_ref, nw_ref, w_ref, o_ref, xn_ref):
    @pl.when(pl.program_id(1) == 0)
    def _():
        xn_ref[...] = _rmsnorm(x_ref[...], nw_ref[...]).astype(BF16)

    o_ref[...] = _dot(xn_ref[...], w_ref[...])


def _norm_inproj(h, norm_w, w_bf16, layer, tm=2048, tn=896):
    t = h.shape[0]
    n = w_bf16.shape[2]
    return pl.pallas_call(
        _norm_inproj_kernel,
        grid=(t // tm, n // tn),
        in_specs=[pl.BlockSpec((tm, D_MODEL), lambda i, j: (i, 0)),
                  pl.BlockSpec((1, D_MODEL), lambda i, j: (0, 0)),
                  pl.BlockSpec((None, D_MODEL, tn), lambda i, j: (layer, 0, j))],
        out_specs=pl.BlockSpec((tm, tn), lambda i, j: (i, j)),
        out_shape=jax.ShapeDtypeStruct((t, n), F32),
        scratch_shapes=[pltpu.VMEM((tm, D_MODEL), BF16)],
        compiler_params=pltpu.CompilerParams(dimension_semantics=("parallel", "arbitrary"),
                                             vmem_limit_bytes=V7X_VMEM_LIMIT),
        name="norm_inproj",
    )(h, norm_w, w_bf16)


def _outproj_mlp_kernel(h_ref, ya_ref, yb_ref, yc_ref, yd_ref, wo_ref, nw_ref, w1_ref, w2_ref, fw_ref,
                        o_ref, xn_ref, *, final_norm):
    f = pl.program_id(1)

    @pl.when(f == 0)
    def _():
        hv = h_ref[...]
        for g, y_ref in enumerate((ya_ref, yb_ref, yc_ref, yd_ref)):
            hv = hv + _dot(y_ref[...].astype(BF16), wo_ref[g * GROUP_W:(g + 1) * GROUP_W, :])
        xn_ref[...] = _rmsnorm(hv, nw_ref[...]).astype(BF16)
        o_ref[...] = hv

    m = _dot(xn_ref[...], w1_ref[...])
    a = jnp.square(jnp.maximum(m, 0.0))
    o_ref[...] += _dot(a.astype(BF16), w2_ref[...])

    if final_norm:
        @pl.when(f == pl.num_programs(1) - 1)
        def _():
            o_ref[...] = _rmsnorm(o_ref[...], fw_ref[...])


def _outproj_mlp(h, ys, wo_bf16, norm_w, w1_bf16, w2_bf16, final_w, layer, final_norm, tm=1024, tf=1024):
    t = h.shape[0]
    yspec = pl.BlockSpec((tm, GROUP_W), lambda i, f: (i, 0))
    return pl.pallas_call(
        functools.partial(_outproj_mlp_kernel, final_norm=final_norm),
        grid=(t // tm, D_FF // tf),
        in_specs=[pl.BlockSpec((tm, D_MODEL), lambda i, f: (i, 0)), yspec, yspec, yspec, yspec,
                  pl.BlockSpec((None, D_MODEL, D_MODEL), lambda i, f: (layer, 0, 0)),
                  pl.BlockSpec((1, D_MODEL), lambda i, f: (0, 0)),
                  pl.BlockSpec((None, D_MODEL, tf), lambda i, f: (layer, 0, f)),
                  pl.BlockSpec((None, tf, D_MODEL), lambda i, f: (layer, f, 0)),
                  pl.BlockSpec((1, D_MODEL), lambda i, f: (0, 0))],
        out_specs=pl.BlockSpec((tm, D_MODEL), lambda i, f: (i, 0)),
        out_shape=jax.ShapeDtypeStruct((t, D_MODEL), F32),
        scratch_shapes=[pltpu.VMEM((tm, D_MODEL), BF16)],
        compiler_params=pltpu.CompilerParams(dimension_semantics=("parallel", "arbitrary"),
                                             vmem_limit_bytes=V7X_VMEM_LIMIT),
        name="outproj_mlp",
    )(h, *ys, wo_bf16, norm_w, w1_bf16, w2_bf16, final_w)


def _lru_kernel(u_ref, cw_ref, cb_ref, wa_ref, ba_ref, wx_ref, bx_ref, lam_ref, o_ref,
                gate_s, a_s, b_s, *, seq):
    gcols = slice(0, GROUP_W)
    xcols = slice(GROUP_W, 2 * GROUP_W)

    def pre(i, carry):
        t0 = pl.multiple_of(i * ROWS, ROWS)
        rows = pl.ds(t0, ROWS)
        xi = u_ref[rows, xcols]
        prev8, next8 = _halo_rows(u_ref, t0, ROWS, seq, xcols)
        xf = (cw_ref[0:1, :] * _shift_down(xi, prev8, 2) + cw_ref[1:2, :] * _shift_down(xi, prev8, 1)
              + cw_ref[2:3, :] * xi + cw_ref[3:4, :] * _shift_up(xi, next8, 1) + cb_ref[...])
        gi = u_ref[rows, gcols]
        gate_s[rows, :] = 0.5 * gi * (1.0 + jnp.tanh(0.7978845608028654 * (gi + 0.044715 * gi * gi * gi)))
        xfb = xf.astype(BF16)
        for d in range(2):
            rec = _sigmoid(_dot(xfb, wa_ref[d]) + ba_ref[d:d + 1, :])
            inp = _sigmoid(_dot(xfb, wx_ref[d]) + bx_ref[d:d + 1, :])
            log_a = -LRU_C * rec * _softplus(-lam_ref[d:d + 1, :])
            a_s[d, rows, :] = jnp.exp(log_a)
            b_s[d, rows, :] = jnp.sqrt(1.0 - jnp.exp(2.0 * log_a)) * inp * xf
        return carry

    lax.fori_loop(0, seq // ROWS, pre, 0)

    row8 = _iota((8, GROUP_W), 0)

    def scan8(a, b, rev):
        for s in (1, 2, 4):
            if rev:
                keep = row8 < 8 - s
                a_sh = jnp.where(keep, pltpu.roll(a, 8 - s, 0), 1.0)
                b_sh = jnp.where(keep, pltpu.roll(b, 8 - s, 0), 0.0)
            else:
                keep = row8 >= s
                a_sh = jnp.where(keep, pltpu.roll(a, s, 0), 1.0)
                b_sh = jnp.where(keep, pltpu.roll(b, s, 0), 0.0)
            b = a * b_sh + b
            a = a * a_sh
        return a, b

    nblk = seq // 8

    def scan(i, carry):
        cf, cb = carry
        rf = pl.ds(pl.multiple_of(i * 8, 8), 8)
        rb = pl.ds(pl.multiple_of((nblk - 1 - i) * 8, 8), 8)
        af, bf = scan8(a_s[0, rf, :], b_s[0, rf, :], False)
        hf = bf + af * cf
        a_s[0, rf, :] = hf
        ab, bb = scan8(a_s[1, rb, :], b_s[1, rb, :], True)
        hb = bb + ab * cb
        a_s[1, rb, :] = hb
        return hf[7:8, :], hb[0:1, :]

    zero = jnp.zeros((1, GROUP_W), F32)
    lax.fori_loop(0, nblk, scan, (zero, zero))

    def post(i, carry):
        rows = pl.ds(pl.multiple_of(i * ROWS, ROWS), ROWS)
        o_ref[rows, :] = gate_s[rows, :] * (a_s[0, rows, :] + a_s[1, rows, :])
        return carry

    lax.fori_loop(0, seq // ROWS, post, 0)


def _lru(u, col_block, batch, seq, cw, cb, wa, ba, wx, bx, lam):
    full = lambda shape: pl.BlockSpec(shape, lambda b: (0,) * len(shape))
    return pl.pallas_call(
        functools.partial(_lru_kernel, seq=seq),
        grid=(batch,),
        in_specs=[pl.BlockSpec((seq, LRU_COLS), lambda b: (b, col_block)),
                  full((4, GROUP_W)), full((1, GROUP_W)),
                  full((2, GROUP_W, GROUP_W)), full((2, GROUP_W)),
                  full((2, GROUP_W, GROUP_W)), full((2, GROUP_W)), full((2, GROUP_W))],
        out_specs=pl.BlockSpec((seq, GROUP_W), lambda b: (b, 0)),
        out_shape=jax.ShapeDtypeStruct((batch * seq, GROUP_W), F32),
        scratch_shapes=[pltpu.VMEM((seq, GROUP_W), F32),
                        pltpu.VMEM((2, seq, GROUP_W), F32),
                        pltpu.VMEM((2, seq, GROUP_W), F32)],
        compiler_params=pltpu.CompilerParams(dimension_semantics=("parallel",),
                                             vmem_limit_bytes=V7X_VMEM_LIMIT),
        name="rglru",
    )(u, cw, cb, wa, ba, wx, bx, lam)


def _ssd_kernel(u_ref, cw_ref, cb_ref, dtb64_ref, alog64_ref, e64_ref,
                dskip_ref, nw_ref, o_ref, xc_s, y_s, st_s, *, seq):
    zcols = slice(0, GROUP_W)
    xbc_cols = slice(GROUP_W, GROUP_W + SSD_XBC)
    dt_cols = slice(GROUP_W + SSD_XBC, SSD_COLS_PAD)

    def pre(i, carry):
        t0 = pl.multiple_of(i * ROWS, ROWS)
        rows = pl.ds(t0, ROWS)
        xi = u_ref[rows, xbc_cols]
        prev8, next8 = _halo_rows(u_ref, t0, ROWS, seq, xbc_cols)
        xc = (cw_ref[0:1, :] * _shift_down(xi, prev8, 2) + cw_ref[1:2, :] * _shift_down(xi, prev8, 1)
              + cw_ref[2:3, :] * xi + cw_ref[3:4, :] * _shift_up(xi, next8, 1) + cb_ref[...])
        xc = xc * _sigmoid(xc)
        xc_s[rows, :] = xc
        y_s[rows, :] = dskip_ref[...] * xc[:, 0:GROUP_W]
        return carry

    lax.fori_loop(0, seq // ROWS, pre, 0)

    st_s[...] = jnp.zeros_like(st_s)
    n = SSD_CHUNK
    nchunk = seq // n
    rr = _iota((n, n), 0)
    cc = _iota((n, n), 1)
    incls = [cc <= rr, cc >= rr]
    tris = [m.astype(BF16) for m in incls]
    edge_rows = [n - 1, 0]
    hmasks = [_head_mask(h) for h in range(N_HEADS)]
    dirs = (0, 1)
    groups = (0, 1)
    heads = range(N_HEADS)
    gsl = [slice(g * SSD_D_STATE, (g + 1) * SSD_D_STATE) for g in groups]

    def chunk_pair(t0s):
        rows = [pl.ds(t0, n) for t0 in t0s]
        xs = [xc_s[rows[d], 0:GROUP_W] for d in dirs]
        bm = [xc_s[rows[d], GROUP_W:2 * GROUP_W].astype(BF16) for d in dirs]
        cm = [xc_s[rows[d], 2 * GROUP_W:3 * GROUP_W].astype(BF16) for d in dirs]
        dtraw = [u_ref[rows[d], dt_cols] for d in dirs]
        dt64 = [_softplus(_mm_exact_lhs(dtraw[d], e64_ref[d]) + dtb64_ref[d:d + 1, :]) for d in dirs]
        adt64 = [dt64[d] * (-jnp.exp(alog64_ref[d:d + 1, :])) for d in dirs]
        xdt = [xs[d] * dt64[d] for d in dirs]
        xdt_b = [x.astype(BF16) for x in xdt]
        cs64 = [_mm_exact_rhs(tris[d], adt64[d]) for d in dirs]
        scores = [[_dot_t(cm[d][:, gsl[g]], bm[d][:, gsl[g]]) for g in groups] for d in dirs]
        y_off = [jnp.concatenate([_dot(cm[d][:, gsl[g]], st_s[d, g].astype(BF16)) for g in groups], axis=1)
                 for d in dirs]
        y = [y_off[d] * jnp.exp(cs64[d]) for d in dirs]
        for d in dirs:
            for h in heads:
                cb = jnp.broadcast_to(cs64[d][:, h * HEAD_DIM:h * HEAD_DIM + 1], (n, n))
                seg = jnp.where(incls[d], cb - cb.T, NA_MASKED)
                m = (scores[d][h // 2] * jnp.exp(seg)).astype(BF16)
                y[d] = y[d] + _dot(m, xdt_b[d]) * hmasks[h]
        for d in dirs:
            y_s[rows[d], :] += y[d]
            edge = cs64[d][edge_rows[d]:edge_rows[d] + 1, :]
            xd = (xdt[d] * jnp.exp(edge - cs64[d])).astype(BF16)
            egrow = jnp.exp(edge)
            for g in groups:
                st_s[d, g] = st_s[d, g] * egrow[:, gsl[g]] + _tdot(bm[d][:, gsl[g]], xd[:, gsl[g]])

    def body(c, carry):
        chunk_pair([pl.multiple_of(c * n, n), pl.multiple_of((nchunk - 1 - c) * n, n)])
        return carry

    lax.fori_loop(0, nchunk, body, 0)

    def post(i, carry):
        rows = pl.ds(pl.multiple_of(i * ROWS, ROWS), ROWS)
        z = u_ref[rows, zcols]
        y = y_s[rows, :] * (z * _sigmoid(z))
        o_ref[rows, :] = _rmsnorm(y, nw_ref[...])
        return carry

    lax.fori_loop(0, seq // ROWS, post, 0)


def _ssd(u, col_block, batch, seq, cw, cb, dtb64, alog64, e64, dskip, nw):
    full = lambda shape: pl.BlockSpec(shape, lambda b: (0,) * len(shape))
    return pl.pallas_call(
        functools.partial(_ssd_kernel, seq=seq),
        grid=(batch,),
        in_specs=[pl.BlockSpec((seq, SSD_COLS_PAD), lambda b: (b, col_block)),
                  full((4, SSD_XBC)), full((1, SSD_XBC)),
                  full((2, GROUP_W)), full((2, GROUP_W)), full((2, SSD_DT_PAD, GROUP_W)),
                  full((1, GROUP_W)), full((1, GROUP_W))],
        out_specs=pl.BlockSpec((seq, GROUP_W), lambda b: (b, 0)),
        out_shape=jax.ShapeDtypeStruct((batch * seq, GROUP_W), F32),
        scratch_shapes=[pltpu.VMEM((seq, SSD_XBC), F32),
                        pltpu.VMEM((seq, GROUP_W), F32),
                        pltpu.VMEM((2, 2, SSD_D_STATE, 128), F32)],
        compiler_params=pltpu.CompilerParams(dimension_semantics=("parallel",),
                                             vmem_limit_bytes=V7X_VMEM_LIMIT),
        name="ssd",
    )(u, cw, cb, dtb64, alog64, e64, dskip, nw)


def _pair_blockdiag(x, pmasks):
    return jnp.concatenate([x * pmasks[0], x * pmasks[1]], axis=0)


def _unit_lower_inverses(n_mats, eye, blk_mask, pmasks):
    mm = lambda a, b: _mm(a, _pair_blockdiag(b, pmasks), RWKV_PREC_INV)
    ps = [-(n * blk_mask) for n in n_mats]
    tds = [eye + p for p in ps]
    for _ in range(3):
        ps = [mm(p, p) for p in ps]
        yield None
        tds = [t + mm(t, p) for t, p in zip(tds, ps)]
        yield None
    ms = [-mm(t, n * (1.0 - blk_mask)) for t, n in zip(tds, n_mats)]
    yield None
    m2s = [mm(m, m) for m in ms]
    yield None
    tos = [eye + m for m in ms]
    tos = [t + mm(t, m2) for t, m2 in zip(tos, m2s)]
    yield None
    yield [mm(to, td) for to, td in zip(tos, tds)]


def _trace_interleaved(*gens):
    live = list(gens)
    while live:
        for gen in list(live):
            try:
                next(gen)
            except StopIteration:
                live.remove(gen)


def _rwkv_kernel(u_ref, mu_ref, w0_ref, wup_ref, a0_ref, aup_ref, gup_ref, kk_ref, ka_ref, rk_ref,
                 gnw_ref, gnb_ref, o_ref,
                 r_s, v_s, n_s, g_s, lw_s, kd_s, b_s, st_s,
                 h_lhs, h_kb, h_gl, h_t, h_a12v, h_a3, *, seq):
    allc = slice(0, RWKV_COLS)
    lane_r = _iota((GROUP_W, GROUP_W), 0) >> 6
    lane_c = _iota((GROUP_W, GROUP_W), 1) >> 6
    blockdiag = (lane_r == lane_c).astype(F32)
    blockdiag_b = blockdiag.astype(BF16)

    def pre(i, carry):
        t0 = pl.multiple_of(i * ROWS, ROWS)
        rows = pl.ds(t0, ROWS)
        x = u_ref[rows, allc]
        prev8, next8 = _halo_rows(u_ref, t0, ROWS, seq, allc)
        prev = _shift_down(x, prev8, 1)
        nxt = _shift_up(x, next8, 1)
        x = x + mu_ref[0:1, :] * (prev - x) + mu_ref[1:2, :] * (nxt - x)
        r = x[:, 0:GROUP_W]
        k = x[:, GROUP_W:2 * GROUP_W]
        v = x[:, 2 * GROUP_W:3 * GROUP_W]
        wd = jnp.tanh(x[:, 3 * GROUP_W:3 * GROUP_W + RWKV_RANK2]).astype(BF16)
        ad = x[:, 3 * GROUP_W + RWKV_RANK2:3 * GROUP_W + 2 * RWKV_RANK2].astype(BF16)
        gd = x[:, 3 * GROUP_W + 2 * RWKV_RANK2:RWKV_COLS]
        kk = k * kk_ref[...]
        kk = kk * lax.rsqrt(_mm_exact_lhs(kk * kk, blockdiag_b) + 1e-12)
        r_s[rows, :] = r
        v_s[rows, :] = v
        n_s[rows, :] = kk
        g_s[rows, :] = _dot(_sigmoid(gd).astype(BF16), gup_ref[...])
        for d in range(2):
            z_w = w0_ref[d:d + 1, :] + _dot(wd, wup_ref[d])
            lw_s[d, rows, :] = -RWKV_DECAY_SCALE * _sigmoid(z_w)
            alpha = _sigmoid(a0_ref[d:d + 1, :] + _dot(ad, aup_ref[d]))
            kd_s[d, rows, :] = k * (1.0 + (alpha - 1.0) * ka_ref[...])
            b_s[d, rows, :] = alpha * kk
        o_ref[rows, :] = jnp.zeros((ROWS, GROUP_W), F32)
        return carry

    lax.fori_loop(0, seq // ROWS, pre, 0)

    st_s[...] = jnp.zeros_like(st_s)
    nchunk = seq // CHUNK
    rr = _iota((CHUNK, CHUNK), 0)
    cc = _iota((CHUNK, CHUNK), 1)
    eye = (rr == cc).astype(F32)
    blk16 = ((rr >> 4) == (cc >> 4)).astype(F32)
    hmasks = [_head_mask(h) for h in range(N_HEADS)]

    tris = [_tri(CHUNK, False).astype(BF16), _tri(CHUNK, True).astype(BF16)]
    stricts = [(cc < rr).astype(F32), (cc > rr).astype(F32)]
    incls = [(cc <= rr).astype(F32), (cc >= rr).astype(F32)]
    edge_rows = [CHUNK - 1, 0]
    dirs = (0, 1)
    heads = range(N_HEADS)
    lane128 = _iota((1, 128), 1)
    pmasks = [(lane128 < HEAD_DIM).astype(F32), (lane128 >= HEAD_DIM).astype(F32)]
    pairs = (0, 1)
    twice = lambda m: jnp.concatenate([m, m], axis=1)
    stricts2 = [twice(m) for m in stricts]
    incls2 = [twice(m) for m in incls]
    eye2 = twice(eye)
    blk16_2 = twice(blk16)
    sel12 = [jnp.concatenate([stricts2[d], incls2[d]], axis=0) for d in dirs]

    units = [(d, k) for d in dirs for k in range(RWKV_GROUP)]
    uidx = {u: j for j, u in enumerate(units)}
    ngroup = nchunk // RWKV_GROUP
    pair = lambda a, p: a[:, p * 128:(p + 1) * 128]
    bd = lambda a: _pair_blockdiag(a, pmasks)

    def group_rows(g):
        t0 = {(0, k): (g * RWKV_GROUP + k) * CHUNK for k in range(RWKV_GROUP)}
        t0.update({(1, k): (nchunk - 1 - g * RWKV_GROUP - k) * CHUNK for k in range(RWKV_GROUP)})
        return {u: pl.ds(pl.multiple_of(t0[u], CHUNK), CHUNK) for u in units}

    def prepare(g, slot):
        rows = group_rows(g)
        lw = {u: lw_s[u[0], rows[u], :] for u in units}
        cs = {u: _mm_exact_rhs(tris[u[0]], lw[u]) for u in units}
        yield
        ginv = {u: jnp.exp(-cs[u]) for u in units}
        kkt = {u: n_s[rows[u], :] * jnp.exp(cs[u] - lw[u]) for u in units}
        rt = {u: r_s[rows[u], :] * jnp.exp(cs[u]) for u in units}
        kh = {u: kd_s[u[0], rows[u], :] * ginv[u] for u in units}
        bh = {u: b_s[u[0], rows[u], :] * ginv[u] for u in units}
        v = {u: v_s[rows[u], :] for u in units}
        lhs = {u: jnp.concatenate([kkt[u], rt[u]], axis=0) for u in units}
        uh = [(u, h) for u in units for h in heads]
        up = [(u, p) for u in units for p in pairs]
        yield
        lm = {x: pair(lhs[x[0]], x[1] // 2) * pmasks[x[1] % 2] for x in uh}
        a_k = {x: _mm(lm[x], pair(kh[x[0]], x[1] // 2), RWKV_PREC_GRAM, _NT) for x in uh}
        a_b = {x: _mm(lm[x], pair(bh[x[0]], x[1] // 2), RWKV_PREC_GRAM, _NT) for x in uh}
        side = lambda f, x: jnp.concatenate([f((x[0], 2 * x[1])), f((x[0], 2 * x[1] + 1))], axis=1)
        yield
        a12v = {x: _mm(side(lambda y: a_k[y], x) * sel12[x[0][0]], bd(pair(v[x[0]], x[1])), RWKV_PREC_APPLY)
                for x in up}
        yield
        t_list = None
        for t_list in _unit_lower_inverses(
                [side(lambda y: a_b[y][0:CHUNK], x) * stricts2[x[0][0]] for x in up], eye2, blk16_2, pmasks):
            yield
        t_inv = dict(zip(up, t_list))
        for u in units:
            j = uidx[u]
            h_lhs[slot, j] = lhs[u].astype(BF16)
            h_kb[slot, j] = jnp.concatenate([kh[u], bh[u]], axis=0).astype(BF16)
            edge = cs[u][edge_rows[u[0]]:edge_rows[u[0]] + 1, :]
            h_gl[slot, j] = jnp.broadcast_to(jnp.exp(edge), (8, GROUP_W))
            for p in pairs:
                h_t[slot, j, p] = t_inv[(u, p)].astype(BF16)
                h_a12v[slot, j, p] = a12v[(u, p)]
                h_a3[slot, j, p] = (side(lambda y: a_b[y][CHUNK:], (u, p)) * incls2[u[0]]).astype(BF16)

    def advance(g, slot):
        rows = group_rows(g)
        s_mat = [st_s[d] for d in dirs]
        for k in range(RWKV_GROUP):
            us = [(d, k) for d in dirs]
            p_all = {u: _dot_t(h_lhs[slot, uidx[u]], s_mat[u[0]].astype(BF16)) for u in us}
            yield
            u_p = {(u, p): _dot(h_t[slot, uidx[u], p],
                                bd(pair(p_all[u][0:CHUNK], p) + h_a12v[slot, uidx[u], p, 0:CHUNK, :]).astype(BF16))
                   for u in us for p in pairs}
            yield
            y_p = {(u, p): h_a12v[slot, uidx[u], p, CHUNK:, :]
                   - _dot(h_a3[slot, uidx[u], p], bd(u_p[(u, p)]).astype(BF16)) for u in us for p in pairs}
            u_all = {u: jnp.concatenate([u_p[(u, p)] for p in pairs], axis=1) for u in us}
            upd = {u: _tdot(jnp.concatenate([v_s[rows[u], :], -u_all[u]], axis=0).astype(BF16), h_kb[slot, uidx[u]])
                   for u in us}
            for u in us:
                o_ref[rows[u], :] += p_all[u][CHUNK:] + jnp.concatenate([y_p[(u, p)] for p in pairs], axis=1)
                s_mat[u[0]] = (s_mat[u[0]] + upd[u]) * h_gl[slot, uidx[u], 0:1, :] * blockdiag
            yield
        for d in dirs:
            st_s[d] = s_mat[d]

    _trace_interleaved(prepare(0, 0))

    def body(i, carry):
        _trace_interleaved(advance(2 * i, 0), prepare(2 * i + 1, 1))
        _trace_interleaved(advance(2 * i + 1, 1), prepare(jnp.minimum(2 * i + 2, ngroup - 1), 0))
        return carry

    lax.fori_loop(0, ngroup // 2, body, 0)

    def post(i, carry):
        rows = pl.ds(pl.multiple_of(i * ROWS, ROWS), ROWS)
        y = o_ref[rows, :]
        mean = _mm_exact_lhs(y, blockdiag_b) * (1.0 / HEAD_DIM)
        yc = y - mean
        var = _mm_exact_lhs(yc * yc, blockdiag_b) * (1.0 / HEAD_DIM)
        y = yc * lax.rsqrt(var + RWKV_GN_EPS) * gnw_ref[...] + gnb_ref[...]
        rk = r_s[rows, :] * (kd_s[0, rows, :] + kd_s[1, rows, :]) * rk_ref[...]
        bonus = _mm_exact_lhs(rk, blockdiag_b) * v_s[rows, :]
        o_ref[rows, :] = (y + bonus) * g_s[rows, :]
        return carry

    lax.fori_loop(0, seq // ROWS, post, 0)


def _rwkv(u, col_block, batch, seq, mu, w0, wup, a0, aup, gup, k_k, k_a, r_k, gn_w, gn_b):
    full = lambda shape: pl.BlockSpec(shape, lambda b: (0,) * len(shape))
    tok = lambda: pltpu.VMEM((seq, GROUP_W), F32)
    tok2 = lambda: pltpu.VMEM((2, seq, GROUP_W), F32)
    nunit = 2 * RWKV_GROUP
    return pl.pallas_call(
        functools.partial(_rwkv_kernel, seq=seq),
        grid=(batch,),
        in_specs=[pl.BlockSpec((seq, RWKV_COLS), lambda b: (b, col_block)),
                  full((2, RWKV_COLS)), full((2, GROUP_W)), full((2, RWKV_RANK2, GROUP_W)),
                  full((2, GROUP_W)), full((2, RWKV_RANK2, GROUP_W)), full((RWKV_RANK2, GROUP_W)),
                  full((1, GROUP_W)), full((1, GROUP_W)), full((1, GROUP_W)),
                  full((1, GROUP_W)), full((1, GROUP_W))],
        out_specs=pl.BlockSpec((seq, GROUP_W), lambda b: (b, 0)),
        out_shape=jax.ShapeDtypeStruct((batch * seq, GROUP_W), F32),
        scratch_shapes=[tok(), tok(), tok(), tok(), tok2(), tok2(), tok2(),
                        pltpu.VMEM((2, GROUP_W, GROUP_W), F32),
                        pltpu.VMEM((2, nunit, 2 * CHUNK, GROUP_W), BF16),
                        pltpu.VMEM((2, nunit, 2 * CHUNK, GROUP_W), BF16),
                        pltpu.VMEM((2, nunit, 8, GROUP_W), F32),
                        pltpu.VMEM((2, nunit, 2, CHUNK, 128), BF16),
                        pltpu.VMEM((2, nunit, 2, 2 * CHUNK, 128), F32),
                        pltpu.VMEM((2, nunit, 2, CHUNK, 128), BF16)],
        compiler_params=pltpu.CompilerParams(dimension_semantics=("parallel",),
                                             vmem_limit_bytes=V7X_VMEM_LIMIT),
        name="rwkv7",
    )(u, mu, w0, wup, a0, aup, gup, k_k, k_a, r_k, gn_w, gn_b)


def _natten_kernel(q_ref, k_ref, v_ref, bias_ref, o_ref, *, n_rows):
    steps = range(NA_ROWS_PER_STEP)
    hmasks = [_head_mask(h) for h in range(N_HEADS)]
    r = [pl.program_id(1) * NA_ROWS_PER_STEP + j for j in steps]
    start = [jnp.clip(r[j] - NA_WIN_ROWS // 2, 0, n_rows - NA_WIN_ROWS) for j in steps]
    win = [pl.ds(pl.multiple_of(start[j] * GRID_W, GRID_W), NA_WIN_ROWS * GRID_W) for j in steps]
    kw = [k_ref[win[j], :].astype(BF16) for j in steps]
    vw = [v_ref[win[j], :].astype(BF16) for j in steps]
    q = [q_ref[j * GRID_W:(j + 1) * GRID_W, :] * (HEAD_DIM ** -0.5) for j in steps]
    qs = [jnp.concatenate([q[j] * hmasks[h] for h in range(N_HEADS)], axis=0).astype(BF16) for j in steps]
    s = [_dot_t(qs[j], kw[j]) + bias_ref[start[j] - r[j] + NA_WIN_ROWS - 1] for j in steps]
    m = [jnp.max(s[j], axis=-1, keepdims=True) for j in steps]
    p = [jnp.exp(s[j] - m[j]) for j in steps]
    p = [p[j] / jnp.sum(p[j], axis=-1, keepdims=True) for j in steps]
    o = [_dot(p[j].astype(BF16), vw[j]) for j in steps]
    for j in steps:
        o_ref[j * GRID_W:(j + 1) * GRID_W, :] = sum(o[j][h * GRID_W:(h + 1) * GRID_W] * hmasks[h]
                                                    for h in range(N_HEADS))


def _natten(u, q_block, batch, seq, bias_tab):
    n_rows = seq // GRID_W
    nkeys = NA_WIN_ROWS * GRID_W
    nstep = n_rows // NA_ROWS_PER_STEP
    qrows = NA_ROWS_PER_STEP * GRID_W

    return pl.pallas_call(
        functools.partial(_natten_kernel, n_rows=n_rows),
        grid=(batch, nstep),
        in_specs=[pl.BlockSpec((qrows, GROUP_W), lambda b, i: (b * nstep + i, q_block)),
                  pl.BlockSpec((seq, GROUP_W), lambda b, i: (b, q_block + 1)),
                  pl.BlockSpec((seq, GROUP_W), lambda b, i: (b, q_block + 2)),
                  pl.BlockSpec((NA_WIN_ROWS, N_HEADS * GRID_W, nkeys), lambda b, i: (0, 0, 0))],
        out_specs=pl.BlockSpec((qrows, GROUP_W), lambda b, i: (b * nstep + i, 0)),
        out_shape=jax.ShapeDtypeStruct((batch * seq, GROUP_W), F32),
        compiler_params=pltpu.CompilerParams(dimension_semantics=("parallel", "arbitrary"),
                                             vmem_limit_bytes=V7X_VMEM_LIMIT),
        name="natten",
    )(u, u, u, bias_tab)


def _natten_bias_table(rel_bias):
    ncol = 2 * NA_WIN_COLS - 1
    qc = np.arange(GRID_W)
    kc = np.arange(GRID_W)
    ws = np.clip(qc - NA_WIN_COLS // 2, 0, GRID_W - NA_WIN_COLS)
    in_win = (kc[None, :] >= ws[:, None]) & (kc[None, :] < ws[:, None] + NA_WIN_COLS)
    col_i = np.clip(kc[None, :] - qc[:, None] + NA_WIN_COLS - 1, 0, ncol - 1)
    onehot = ((col_i[None] == np.arange(ncol)[:, None, None]) & in_win[None]).astype(np.float32)
    mask_add = np.where(in_win, 0.0, NA_MASKED).astype(np.float32)
    toep = jnp.einsum('hrc,cqk->hrqk', rel_bias.astype(F32), jnp.asarray(onehot), precision=HI) + mask_add
    tab = jnp.stack([toep[:, d:d + NA_WIN_ROWS] for d in range(NA_WIN_ROWS)])
    return tab.transpose(0, 1, 3, 2, 4).reshape(NA_WIN_ROWS, N_HEADS * GRID_W, NA_WIN_ROWS * GRID_W)


def _pack_w_in(w_in):
    o_ssd = RWKV_COLS
    o_dt_end = o_ssd + GROUP_W + SSD_XBC + 8
    o_lru_end = o_dt_end + LRU_COLS
    pad = jnp.zeros(w_in.shape[:-1] + (SSD_DT_PAD - 8,), w_in.dtype)
    return jnp.concatenate([w_in[..., :o_dt_end], pad, w_in[..., o_lru_end:], w_in[..., o_dt_end:o_lru_end]],
                           axis=-1).astype(BF16)


def _pad_rank(w_up):
    z = jnp.zeros_like(w_up[0])
    return jnp.stack([jnp.concatenate([w_up[0], z], axis=0), jnp.concatenate([z, w_up[1]], axis=0)])


def _block_diag(w):
    _, nb, n, _ = w.shape
    eye = jnp.eye(nb, dtype=w.dtype)
    return jnp.einsum('dkij,kl->dkilj', w, eye).reshape(2, nb * n, nb * n)


def _dt_expander():
    e64 = np.zeros((2, SSD_DT_PAD, GROUP_W), np.float32)
    for d in range(2):
        for h in range(N_HEADS):
            e64[d, d * N_HEADS + h, h * HEAD_DIM:(h + 1) * HEAD_DIM] = 1.0
    return jnp.asarray(e64, BF16)


def kernel(x, norm1_w, w_in, rwkv_shift_mu, rwkv_w0, rwkv_w_up, rwkv_a0, rwkv_a_up, rwkv_g_up, rwkv_k_k, rwkv_k_a, rwkv_r_k, rwkv_gn_w, rwkv_gn_b, ssd_conv_w, ssd_conv_b, ssd_dt_bias, ssd_a_log, ssd_d, ssd_norm_w, lru_conv_w, lru_conv_b, lru_gate_a_w, lru_gate_a_b, lru_gate_x_w, lru_gate_x_b, lru_lambda, na_rel_bias, w_out, norm2_w, w_mlp1, w_mlp2, final_norm_w):
    batch, seq, _ = x.shape
    depth = w_in.shape[0]
    h = x.reshape(batch * seq, D_MODEL)

    w_in_p = _pack_w_in(w_in)
    w_out_b = w_out.astype(BF16)
    w1_b = w_mlp1.astype(BF16)
    w2_b = w_mlp2.astype(BF16)
    e64 = _dt_expander()
    row = lambda a: a.reshape(1, -1)
    rep = lambda a, n: jnp.repeat(a, n, axis=-1)

    for l in range(depth):
        u = _norm_inproj(h, row(norm1_w[l]), w_in_p, l)
        y_a = _rwkv(u, 0, batch, seq, rwkv_shift_mu[l], rwkv_w0[l], _pad_rank(rwkv_w_up[l]).astype(BF16),
                    rwkv_a0[l], _pad_rank(rwkv_a_up[l]).astype(BF16), rwkv_g_up[l].astype(BF16),
                    row(rwkv_k_k[l]), row(rwkv_k_a[l]), row(rwkv_r_k[l]), row(rwkv_gn_w[l]), row(rwkv_gn_b[l]))
        y_b = _ssd(u, 1, batch, seq, ssd_conv_w[l], row(ssd_conv_b[l]),
                   rep(ssd_dt_bias[l], HEAD_DIM), rep(ssd_a_log[l], HEAD_DIM), e64,
                   row(rep(ssd_d[l], HEAD_DIM)), row(ssd_norm_w[l]))
        y_c = _lru(u, (RWKV_COLS + SSD_COLS_PAD + NA_COLS) // LRU_COLS, batch, seq,
                   lru_conv_w[l], row(lru_conv_b[l]),
                   _block_diag(lru_gate_a_w[l]).astype(BF16), lru_gate_a_b[l],
                   _block_diag(lru_gate_x_w[l]).astype(BF16), lru_gate_x_b[l], lru_lambda[l])
        y_d = _natten(u, (RWKV_COLS + SSD_COLS_PAD) // GROUP_W, batch, seq, _natten_bias_table(na_rel_bias[l]))
        h = _outproj_mlp(h, (y_a, y_b, y_c, y_d), w_out_b, row(norm2_w[l]), w1_b, w2_b, row(final_norm_w), l,
                         final_norm=(l == depth - 1))
    return h.reshape(batch, seq, D_MODEL)
```

```python
import functools

import numpy as np
import jax
import jax.numpy as jnp
from jax import lax
from jax.experimental import pallas as pl
from jax.experimental.pallas import tpu as pltpu

F32 = jnp.float32
BF16 = jnp.bfloat16
HI = lax.Precision.HIGHEST

D_MODEL = 1024
GRID_W = 64
GROUP_W = 256
HEAD_DIM = 64
N_HEADS = GROUP_W // HEAD_DIM
D_FF = 4 * D_MODEL
NORM_EPS = 1e-5

RWKV_RANK2 = 128
RWKV_DECAY_SCALE = 0.6065306597126334
RWKV_GN_EPS = 64e-5
RWKV_COLS = 3 * GROUP_W + 3 * RWKV_RANK2

SSD_D_STATE = 128
SSD_XBC = GROUP_W + 4 * SSD_D_STATE
SSD_DT_PAD = 128
SSD_COLS_PAD = GROUP_W + SSD_XBC + SSD_DT_PAD

LRU_C = 8.0
LRU_SCAN_BLOCKS = 8
LRU_COLS = 2 * GROUP_W
NA_COLS = 3 * GROUP_W
NA_WIN_ROWS = 8
NA_WIN_COLS = 16
NA_MASKED = -1e30
NA_ROWS_PER_STEP = 8

U_COLS = RWKV_COLS + SSD_COLS_PAD + NA_COLS + LRU_COLS
CHUNK = 64
RWKV_GROUP = 4
SSD_CHUNK = 256
ROWS = 256

V7X_VMEM_LIMIT = 60 * 1024 * 1024

RWKV_PREC_GRAM = "bf16"
RWKV_PREC_APPLY = "bf16"
RWKV_PREC_INV = "bf16"


def _dot(a, b, prec=None):
    return jnp.dot(a, b, preferred_element_type=F32, precision=prec)


def _dot_t(a, b, prec=None):
    return lax.dot_general(a, b, (((1,), (1,)), ((), ())), preferred_element_type=F32, precision=prec)


def _tdot(a, b, prec=None):
    return lax.dot_general(a, b, (((0,), (0,)), ((), ())), preferred_element_type=F32, precision=prec)


_NN = (((1,), (0,)), ((), ()))
_NT = (((1,), (1,)), ((), ()))
_TN = (((0,), (0,)), ((), ()))


def _mm(a, b, mode, dims=_NN):
    dg = lambda x, y: lax.dot_general(x, y, dims, preferred_element_type=F32)
    if mode == "hi":
        return lax.dot_general(a, b, dims, preferred_element_type=F32, precision=HI)
    ah = a.astype(BF16)
    bh = b.astype(BF16)
    if mode == "bf16":
        return dg(ah, bh)
    al = (a - ah.astype(F32)).astype(BF16)
    bl = (b - bh.astype(F32)).astype(BF16)
    return dg(ah, bh) + (dg(ah, bl) + dg(al, bh))


def _split3(x):
    x1 = x.astype(BF16)
    r1 = x - x1.astype(F32)
    x2 = r1.astype(BF16)
    x3 = (r1 - x2.astype(F32)).astype(BF16)
    return x1, x2, x3


def _mm_exact_lhs(x, w01):
    x1, x2, x3 = _split3(x)
    return (_dot(x3, w01) + _dot(x2, w01)) + _dot(x1, w01)


def _mm_exact_rhs(w01, x):
    x1, x2, x3 = _split3(x)
    return (_dot(w01, x3) + _dot(w01, x2)) + _dot(w01, x1)


def _iota(shape, dim):
    return lax.broadcasted_iota(jnp.int32, shape, dim)


def _head_mask(h, width=GROUP_W):
    lane = _iota((1, width), 1)
    return ((lane >= h * HEAD_DIM) & (lane < (h + 1) * HEAD_DIM)).astype(F32)


def _sigmoid(x):
    return 0.5 * jnp.tanh(0.5 * x) + 0.5


def _softplus(x):
    return jnp.maximum(x, 0.0) + jnp.log(1.0 + jnp.exp(-jnp.abs(x)))


def _shift_down(x, prev8, k):
    rx = pltpu.roll(x, k, 0)
    row8 = _iota((8, x.shape[1]), 0)
    head = jnp.where(row8 < k, pltpu.roll(prev8, k, 0), rx[0:8])
    return jnp.concatenate([head, rx[8:]], axis=0)


def _shift_up(x, next8, k):
    n = x.shape[0]
    rx = pltpu.roll(x, n - k, 0)
    row8 = _iota((8, x.shape[1]), 0)
    tail = jnp.where(row8 >= 8 - k, pltpu.roll(next8, 8 - k, 0), rx[n - 8:])
    return jnp.concatenate([rx[:n - 8], tail], axis=0)


def _halo_rows(ref, t0, nrows, seq, cols):
    pstart = pl.multiple_of(jnp.maximum(t0 - 8, 0), 8)
    nstart = pl.multiple_of(jnp.minimum(t0 + nrows, seq - 8), 8)
    prev8 = ref[pl.ds(pstart, 8), cols] * (t0 > 0).astype(F32)
    next8 = ref[pl.ds(nstart, 8), cols] * (t0 + nrows < seq).astype(F32)
    return prev8, next8


def _tri(n, upper):
    r = _iota((n, n), 0)
    c = _iota((n, n), 1)
    return ((c >= r) if upper else (c <= r)).astype(F32)


def _rmsnorm(x, w):
    ms = jnp.mean(x * x, axis=-1, keepdims=True)
    return x * lax.rsqrt(ms + NORM_EPS) * w


def _norm_inproj_kernel(x_ref, nw_ref, w_ref, o_ref, xn_ref):
    @pl.when(pl.program_id(1) == 0)
    def _():
        xn_ref[...] = _rmsnorm(x_ref[...], nw_ref[...]).astype(BF16)

    o_ref[...] = _dot(xn_ref[...], w_ref[...])


def _norm_inproj(h, norm_w, w_bf16, layer, tm=2048, tn=896):
    t = h.shape[0]
    n = w_bf16.shape[2]
    return pl.pallas_call(
        _norm_inproj_kernel,
        grid=(t // tm, n // tn),
        in_specs=[pl.BlockSpec((tm, D_MODEL), lambda i, j: (i, 0)),
                  pl.BlockSpec((1, D_MODEL), lambda i, j: (0, 0)),
                  pl.BlockSpec((None, D_MODEL, tn), lambda i, j: (layer, 0, j))],
        out_specs=pl.BlockSpec((tm, tn), lambda i, j: (i, j)),
        out_shape=jax.ShapeDtypeStruct((t, n), F32),
        scratch_shapes=[pltpu.VMEM((tm, D_MODEL), BF16)],
        compiler_params=pltpu.CompilerParams(dimension_semantics=("parallel", "arbitrary"),
                                             vmem_limit_bytes=V7X_VMEM_LIMIT),
        name="norm_inproj",
    )(h, norm_w, w_bf16)


def _outproj_mlp_kernel(h_ref, ya_ref, yb_ref, yc_ref, yd_ref, wo_ref, nw_ref, w1_ref, w2_ref, fw_ref,
                        o_ref, xn_ref, *, final_norm):
    f = pl.program_id(1)

    @pl.when(f == 0)
    def _():
        hv = h_ref[...]
        for g, y_ref in enumerate((ya_ref, yb_ref, yc_ref, yd_ref)):
            hv = hv + _dot(y_ref[...].astype(BF16), wo_ref[g * GROUP_W:(g + 1) * GROUP_W, :])
        xn_ref[...] = _rmsnorm(hv, nw_ref[...]).astype(BF16)
        o_ref[...] = hv

    m = _dot(xn_ref[...], w1_ref[...])
    a = jnp.square(jnp.maximum(m, 0.0))
    o_ref[...] += _dot(a.astype(BF16), w2_ref[...])

    if final_norm:
        @pl.when(f == pl.num_programs(1) - 1)
        def _():
            o_ref[...] = _rmsnorm(o_ref[...], fw_ref[...])


def _outproj_mlp(h, ys, wo_bf16, norm_w, w1_bf16, w2_bf16, final_w, layer, final_norm, tm=1024, tf=1024):
    t = h.shape[0]
    yspec = pl.BlockSpec((tm, GROUP_W), lambda i, f: (i, 0))
    return pl.pallas_call(
        functools.partial(_outproj_mlp_kernel, final_norm=final_norm),
        grid=(t // tm, D_FF // tf),
        in_specs=[pl.BlockSpec((tm, D_MODEL), lambda i, f: (i, 0)), yspec, yspec, yspec, yspec,
                  pl.BlockSpec((None, D_MODEL, D_MODEL), lambda i, f: (layer, 0, 0)),
                  pl.BlockSpec((1, D_MODEL), lambda i, f: (0, 0)),
                  pl.BlockSpec((None, D_MODEL, tf), lambda i, f: (layer, 0, f)),
                  pl.BlockSpec((None, tf, D_MODEL), lambda i, f: (layer, f, 0)),
                  pl.BlockSpec((1, D_MODEL), lambda i, f: (0, 0))],
        out_specs=pl.BlockSpec((tm, D_MODEL), lambda i, f: (i, 0)),
        out_shape=jax.ShapeDtypeStruct((t, D_MODEL), F32),
        scratch_shapes=[pltpu.VMEM((tm, D_MODEL), BF16)],
        compiler_params=pltpu.CompilerParams(dimension_semantics=("parallel", "arbitrary"),
                                             vmem_limit_bytes=V7X_VMEM_LIMIT),
        name="outproj_mlp",
    )(h, *ys, wo_bf16, norm_w, w1_bf16, w2_bf16, final_w)


def _lru_kernel(u_ref, cw_ref, cb_ref, wa_ref, ba_ref, wx_ref, bx_ref, lam_ref, o_ref,
                gate_s, a_s, b_s, *, seq):
    gcols = slice(0, GROUP_W)
    xcols = slice(GROUP_W, 2 * GROUP_W)

    def pre(i, carry):
        t0 = pl.multiple_of(i * ROWS, ROWS)
        rows = pl.ds(t0, ROWS)
        xi = u_ref[rows, xcols]
        prev8, next8 = _halo_rows(u_ref, t0, ROWS, seq, xcols)
        xf = (cw_ref[0:1, :] * _shift_down(xi, prev8, 2) + cw_ref[1:2, :] * _shift_down(xi, prev8, 1)
              + cw_ref[2:3, :] * xi + cw_ref[3:4, :] * _shift_up(xi, next8, 1) + cb_ref[...])
        gi = u_ref[rows, gcols]
        gate_s[rows, :] = 0.5 * gi * (1.0 + jnp.tanh(0.7978845608028654 * (gi + 0.044715 * gi * gi * gi)))
        xfb = xf.astype(BF16)
        for d in range(2):
            rec = _sigmoid(_dot(xfb, wa_ref[d]) + ba_ref[d:d + 1, :])
            inp = _sigmoid(_dot(xfb, wx_ref[d]) + bx_ref[d:d + 1, :])
            log_a = -LRU_C * rec * _softplus(-lam_ref[d:d + 1, :])
            a_s[d, rows, :] = jnp.exp(log_a)
            b_s[d, rows, :] = jnp.sqrt(1.0 - jnp.exp(2.0 * log_a)) * inp * xf
        return carry

    lax.fori_loop(0, seq // ROWS, pre, 0)

    row8 = _iota((8, GROUP_W), 0)

    def scan8_step(a, b, rev, s):
        if rev:
            keep = row8 < 8 - s
            a_sh = jnp.where(keep, pltpu.roll(a, 8 - s, 0), 1.0)
            b_sh = jnp.where(keep, pltpu.roll(b, 8 - s, 0), 0.0)
        else:
            keep = row8 >= s
            a_sh = jnp.where(keep, pltpu.roll(a, s, 0), 1.0)
            b_sh = jnp.where(keep, pltpu.roll(b, s, 0), 0.0)
        return a * a_sh, a * b_sh + b

    nblk = seq // 8
    per_iter = LRU_SCAN_BLOCKS

    def scan(i, carry):
        cf, cb = carry
        rows = [(0, pl.ds(pl.multiple_of((i * per_iter + j) * 8, 8), 8)) for j in range(per_iter)]
        rows += [(1, pl.ds(pl.multiple_of((nblk - 1 - i * per_iter - j) * 8, 8), 8)) for j in range(per_iter)]
        ab = [(a_s[d, r, :], b_s[d, r, :]) for d, r in rows]
        for s in (1, 2, 4):
            ab = [scan8_step(a, b, d == 1, s) for (a, b), (d, _) in zip(ab, rows)]
        for (a, b), (d, r) in zip(ab, rows):
            if d == 0:
                h = b + a * cf
                cf = h[7:8, :]
            else:
                h = b + a * cb
                cb = h[0:1, :]
            a_s[d, r, :] = h
        return cf, cb

    zero = jnp.zeros((1, GROUP_W), F32)
    lax.fori_loop(0, nblk // per_iter, scan, (zero, zero))

    def post(i, carry):
        rows = pl.ds(pl.multiple_of(i * ROWS, ROWS), ROWS)
        o_ref[rows, :] = gate_s[rows, :] * (a_s[0, rows, :] + a_s[1, rows, :])
        return carry

    lax.fori_loop(0, seq // ROWS, post, 0)


def _lru(u, col_block, batch, seq, cw, cb, wa, ba, wx, bx, lam):
    full = lambda shape: pl.BlockSpec(shape, lambda b: (0,) * len(shape))
    return pl.pallas_call(
        functools.partial(_lru_kernel, seq=seq),
        grid=(batch,),
        in_specs=[pl.BlockSpec((seq, LRU_COLS), lambda b: (b, col_block)),
                  full((4, GROUP_W)), full((1, GROUP_W)),
                  full((2, GROUP_W, GROUP_W)), full((2, GROUP_W)),
                  full((2, GROUP_W, GROUP_W)), full((2, GROUP_W)), full((2, GROUP_W))],
        out_specs=pl.BlockSpec((seq, GROUP_W), lambda b: (b, 0)),
        out_shape=jax.ShapeDtypeStruct((batch * seq, GROUP_W), F32),
        scratch_shapes=[pltpu.VMEM((seq, GROUP_W), F32),
                        pltpu.VMEM((2, seq, GROUP_W), F32),
                        pltpu.VMEM((2, seq, GROUP_W), F32)],
        compiler_params=pltpu.CompilerParams(dimension_semantics=("parallel",),
                                             vmem_limit_bytes=V7X_VMEM_LIMIT),
        name="rglru",
    )(u, cw, cb, wa, ba, wx, bx, lam)


def _ssd_kernel(u_ref, cw_ref, cb_ref, dtb64_ref, alog64_ref, e64_ref,
                dskip_ref, nw_ref, o_ref, xc_s, y_s, st_s, *, seq):
    zcols = slice(0, GROUP_W)
    xbc_cols = slice(GROUP_W, GROUP_W + SSD_XBC)
    dt_cols = slice(GROUP_W + SSD_XBC, SSD_COLS_PAD)

    def pre(i, carry):
        t0 = pl.multiple_of(i * ROWS, ROWS)
        rows = pl.ds(t0, ROWS)
        xi = u_ref[rows, xbc_cols]
        prev8, next8 = _halo_rows(u_ref, t0, ROWS, seq, xbc_cols)
        xc = (cw_ref[0:1, :] * _shift_down(xi, prev8, 2) + cw_ref[1:2, :] * _shift_down(xi, prev8, 1)
              + cw_ref[2:3, :] * xi + cw_ref[3:4, :] * _shift_up(xi, next8, 1) + cb_ref[...])
        xc = xc * _sigmoid(xc)
        xc_s[rows, :] = xc
        y_s[rows, :] = dskip_ref[...] * xc[:, 0:GROUP_W]
        return carry

    lax.fori_loop(0, seq // ROWS, pre, 0)

    st_s[...] = jnp.zeros_like(st_s)
    n = SSD_CHUNK
    nchunk = seq // n
    rr = _iota((n, n), 0)
    cc = _iota((n, n), 1)
    incls = [cc <= rr, cc >= rr]
    tris = [m.astype(BF16) for m in incls]
    edge_rows = [n - 1, 0]
    hmasks = [_head_mask(h) for h in range(N_HEADS)]
    dirs = (0, 1)
    groups = (0, 1)
    heads = range(N_HEADS)
    gsl = [slice(g * SSD_D_STATE, (g + 1) * SSD_D_STATE) for g in groups]

    def chunk_pair(t0s):
        rows = [pl.ds(t0, n) for t0 in t0s]
        xs = [xc_s[rows[d], 0:GROUP_W] for d in dirs]
        bm = [xc_s[rows[d], GROUP_W:2 * GROUP_W].astype(BF16) for d in dirs]
        cm = [xc_s[rows[d], 2 * GROUP_W:3 * GROUP_W].astype(BF16) for d in dirs]
        dtraw = [u_ref[rows[d], dt_cols] for d in dirs]
        dt64 = [_softplus(_mm_exact_lhs(dtraw[d], e64_ref[d]) + dtb64_ref[d:d + 1, :]) for d in dirs]
        adt64 = [dt64[d] * (-jnp.exp(alog64_ref[d:d + 1, :])) for d in dirs]
        xdt = [xs[d] * dt64[d] for d in dirs]
        xdt_b = [x.astype(BF16) for x in xdt]
        cs64 = [_mm_exact_rhs(tris[d], adt64[d]) for d in dirs]
        scores = [[_dot_t(cm[d][:, gsl[g]], bm[d][:, gsl[g]]) for g in groups] for d in dirs]
        y_off = [jnp.concatenate([_dot(cm[d][:, gsl[g]], st_s[d, g].astype(BF16)) for g in groups], axis=1)
                 for d in dirs]
        y = [y_off[d] * jnp.exp(cs64[d]) for d in dirs]
        for d in dirs:
            ms = []
            for h in heads:
                cb = jnp.broadcast_to(cs64[d][:, h * HEAD_DIM:h * HEAD_DIM + 1], (n, n))
                seg = jnp.where(incls[d], cb - cb.T, NA_MASKED)
                ms.append((scores[d][h // 2] * jnp.exp(seg)).astype(BF16))
            x_heads = jnp.concatenate([(xdt[d] * hmasks[h]).astype(BF16) for h in heads], axis=0)
            y[d] = y[d] + _dot(jnp.concatenate(ms, axis=1), x_heads)
        for d in dirs:
            y_s[rows[d], :] += y[d]
            edge = cs64[d][edge_rows[d]:edge_rows[d] + 1, :]
            xd = (xdt[d] * jnp.exp(edge - cs64[d])).astype(BF16)
            egrow = jnp.exp(edge)
            for g in groups:
                st_s[d, g] = st_s[d, g] * egrow[:, gsl[g]] + _tdot(bm[d][:, gsl[g]], xd[:, gsl[g]])

    def body(c, carry):
        chunk_pair([pl.multiple_of(c * n, n), pl.multiple_of((nchunk - 1 - c) * n, n)])
        return carry

    lax.fori_loop(0, nchunk, body, 0)

    def post(i, carry):
        rows = pl.ds(pl.multiple_of(i * ROWS, ROWS), ROWS)
        z = u_ref[rows, zcols]
        y = y_s[rows, :] * (z * _sigmoid(z))
        o_ref[rows, :] = _rmsnorm(y, nw_ref[...])
        return carry

    lax.fori_loop(0, seq // ROWS, post, 0)


def _ssd(u, col_block, batch, seq, cw, cb, dtb64, alog64, e64, dskip, nw):
    full = lambda shape: pl.BlockSpec(shape, lambda b: (0,) * len(shape))
    return pl.pallas_call(
        functools.partial(_ssd_kernel, seq=seq),
        grid=(batch,),
        in_specs=[pl.BlockSpec((seq, SSD_COLS_PAD), lambda b: (b, col_block)),
                  full((4, SSD_XBC)), full((1, SSD_XBC)),
                  full((2, GROUP_W)), full((2, GROUP_W)), full((2, SSD_DT_PAD, GROUP_W)),
                  full((1, GROUP_W)), full((1, GROUP_W))],
        out_specs=pl.BlockSpec((seq, GROUP_W), lambda b: (b, 0)),
        out_shape=jax.ShapeDtypeStruct((batch * seq, GROUP_W), F32),
        scratch_shapes=[pltpu.VMEM((seq, SSD_XBC), F32),
                        pltpu.VMEM((seq, GROUP_W), F32),
                        pltpu.VMEM((2, 2, SSD_D_STATE, 128), F32)],
        compiler_params=pltpu.CompilerParams(dimension_semantics=("parallel",),
                                             vmem_limit_bytes=V7X_VMEM_LIMIT),
        name="ssd",
    )(u, cw, cb, dtb64, alog64, e64, dskip, nw)


def _pair_blockdiag(x, pmasks):
    return jnp.concatenate([x * pmasks[0], x * pmasks[1]], axis=0)


def _unit_lower_inverses(n_mats, eye, blk_mask, pmasks):
    mm = lambda a, b: _mm(a, _pair_blockdiag(b, pmasks), RWKV_PREC_INV)
    ps = [-(n * blk_mask) for n in n_mats]
    tds = [eye + p for p in ps]
    for _ in range(3):
        ps = [mm(p, p) for p in ps]
        yield None
        tds = [t + mm(t, p) for t, p in zip(tds, ps)]
        yield None
    ms = [-mm(t, n * (1.0 - blk_mask)) for t, n in zip(tds, n_mats)]
    yield None
    m2s = [mm(m, m) for m in ms]
    yield None
    tos = [eye + m for m in ms]
    tos = [t + mm(t, m2) for t, m2 in zip(tos, m2s)]
    yield None
    yield [mm(to, td) for to, td in zip(tos, tds)]


def _trace_interleaved(*gens):
    live = list(gens)
    while live:
        for gen in list(live):
            try:
                next(gen)
            except StopIteration:
                live.remove(gen)


def _rwkv_kernel(u_ref, mu_ref, w0_ref, wup_ref, a0_ref, aup_ref, gup_ref, kk_ref, ka_ref, rk_ref,
                 gnw_ref, gnb_ref, o_ref,
                 r_s, v_s, n_s, g_s, lw_s, kd_s, b_s, st_s,
                 h_lhs, h_kb, h_gl, h_t, h_a12v, h_a3, *, seq):
    allc = slice(0, RWKV_COLS)
    lane_r = _iota((GROUP_W, GROUP_W), 0) >> 6
    lane_c = _iota((GROUP_W, GROUP_W), 1) >> 6
    blockdiag = (lane_r == lane_c).astype(F32)
    blockdiag_b = blockdiag.astype(BF16)

    def pre(i, carry):
        t0 = pl.multiple_of(i * ROWS, ROWS)
        rows = pl.ds(t0, ROWS)
        x = u_ref[rows, allc]
        prev8, next8 = _halo_rows(u_ref, t0, ROWS, seq, allc)
        prev = _shift_down(x, prev8, 1)
        nxt = _shift_up(x, next8, 1)
        x = x + mu_ref[0:1, :] * (prev - x) + mu_ref[1:2, :] * (nxt - x)
        r = x[:, 0:GROUP_W]
        k = x[:, GROUP_W:2 * GROUP_W]
        v = x[:, 2 * GROUP_W:3 * GROUP_W]
        wd = jnp.tanh(x[:, 3 * GROUP_W:3 * GROUP_W + RWKV_RANK2]).astype(BF16)
        ad = x[:, 3 * GROUP_W + RWKV_RANK2:3 * GROUP_W + 2 * RWKV_RANK2].astype(BF16)
        gd = x[:, 3 * GROUP_W + 2 * RWKV_RANK2:RWKV_COLS]
        kk = k * kk_ref[...]
        kk = kk * lax.rsqrt(_mm_exact_lhs(kk * kk, blockdiag_b) + 1e-12)
        r_s[rows, :] = r
        v_s[rows, :] = v
        n_s[rows, :] = kk
        g_s[rows, :] = _dot(_sigmoid(gd).astype(BF16), gup_ref[...])
        for d in range(2):
            z_w = w0_ref[d:d + 1, :] + _dot(wd, wup_ref[d])
            lw_s[d, rows, :] = -RWKV_DECAY_SCALE * _sigmoid(z_w)
            alpha = _sigmoid(a0_ref[d:d + 1, :] + _dot(ad, aup_ref[d]))
            kd_s[d, rows, :] = k * (1.0 + (alpha - 1.0) * ka_ref[...])
            b_s[d, rows, :] = alpha * kk
        o_ref[rows, :] = jnp.zeros((ROWS, GROUP_W), F32)
        return carry

    lax.fori_loop(0, seq // ROWS, pre, 0)

    st_s[...] = jnp.zeros_like(st_s)
    nchunk = seq // CHUNK
    rr = _iota((CHUNK, CHUNK), 0)
    cc = _iota((CHUNK, CHUNK), 1)
    eye = (rr == cc).astype(F32)
    blk16 = ((rr >> 4) == (cc >> 4)).astype(F32)
    hmasks = [_head_mask(h) for h in range(N_HEADS)]

    tris = [_tri(CHUNK, False).astype(BF16), _tri(CHUNK, True).astype(BF16)]
    stricts = [(cc < rr).astype(F32), (cc > rr).astype(F32)]
    incls = [(cc <= rr).astype(F32), (cc >= rr).astype(F32)]
    edge_rows = [CHUNK - 1, 0]
    dirs = (0, 1)
    heads = range(N_HEADS)
    lane128 = _iota((1, 128), 1)
    pmasks = [(lane128 < HEAD_DIM).astype(F32), (lane128 >= HEAD_DIM).astype(F32)]
    pairs = (0, 1)
    twice = lambda m: jnp.concatenate([m, m], axis=1)
    stricts2 = [twice(m) for m in stricts]
    incls2 = [twice(m) for m in incls]
    eye2 = twice(eye)
    blk16_2 = twice(blk16)
    sel12 = [jnp.concatenate([stricts2[d], incls2[d]], axis=0) for d in dirs]

    units = [(d, k) for d in dirs for k in range(RWKV_GROUP)]
    uidx = {u: j for j, u in enumerate(units)}
    ngroup = nchunk // RWKV_GROUP
    pair = lambda a, p: a[:, p * 128:(p + 1) * 128]
    bd = lambda a: _pair_blockdiag(a, pmasks)

    def group_rows(g):
        t0 = {(0, k): (g * RWKV_GROUP + k) * CHUNK for k in range(RWKV_GROUP)}
        t0.update({(1, k): (nchunk - 1 - g * RWKV_GROUP - k) * CHUNK for k in range(RWKV_GROUP)})
        return {u: pl.ds(pl.multiple_of(t0[u], CHUNK), CHUNK) for u in units}

    def prepare(g, slot):
        rows = group_rows(g)
        lw = {u: lw_s[u[0], rows[u], :] for u in units}
        cs = {u: _mm_exact_rhs(tris[u[0]], lw[u]) for u in units}
        yield
        ginv = {u: jnp.exp(-cs[u]) for u in units}
        kkt = {u: n_s[rows[u], :] * jnp.exp(cs[u] - lw[u]) for u in units}
        rt = {u: r_s[rows[u], :] * jnp.exp(cs[u]) for u in units}
        kh = {u: kd_s[u[0], rows[u], :] * ginv[u] for u in units}
        bh = {u: b_s[u[0], rows[u], :] * ginv[u] for u in units}
        v = {u: v_s[rows[u], :] for u in units}
        lhs = {u: jnp.concatenate([kkt[u], rt[u]], axis=0) for u in units}
        uh = [(u, h) for u in units for h in heads]
        up = [(u, p) for u in units for p in pairs]
        yield
        lm = {x: pair(lhs[x[0]], x[1] // 2) * pmasks[x[1] % 2] for x in uh}
        a_k = {x: _mm(lm[x], pair(kh[x[0]], x[1] // 2), RWKV_PREC_GRAM, _NT) for x in uh}
        a_b = {x: _mm(lm[x], pair(bh[x[0]], x[1] // 2), RWKV_PREC_GRAM, _NT) for x in uh}
        side = lambda f, x: jnp.concatenate([f((x[0], 2 * x[1])), f((x[0], 2 * x[1] + 1))], axis=1)
        yield
        a12v = {x: _mm(side(lambda y: a_k[y], x) * sel12[x[0][0]], bd(pair(v[x[0]], x[1])), RWKV_PREC_APPLY)
                for x in up}
        yield
        t_list = None
        for t_list in _unit_lower_inverses(
                [side(lambda y: a_b[y][0:CHUNK], x) * stricts2[x[0][0]] for x in up], eye2, blk16_2, pmasks):
            yield
        t_inv = dict(zip(up, t_list))
        for u in units:
            j = uidx[u]
            h_lhs[slot, j] = lhs[u].astype(BF16)
            h_kb[slot, j] = jnp.concatenate([kh[u], bh[u]], axis=0).astype(BF16)
            edge = cs[u][edge_rows[u[0]]:edge_rows[u[0]] + 1, :]
            h_gl[slot, j] = jnp.broadcast_to(jnp.exp(edge), (8, GROUP_W))
            for p in pairs:
                h_t[slot, j, p] = t_inv[(u, p)].astype(BF16)
                h_a12v[slot, j, p] = a12v[(u, p)]
                h_a3[slot, j, p] = (side(lambda y: a_b[y][CHUNK:], (u, p)) * incls2[u[0]]).astype(BF16)

    def advance(g, slot):
        rows = group_rows(g)
        s_mat = [st_s[d] for d in dirs]
        for k in range(RWKV_GROUP):
            us = [(d, k) for d in dirs]
            p_all = {u: _dot_t(h_lhs[slot, uidx[u]], s_mat[u[0]].astype(BF16)) for u in us}
            yield
            u_p = {(u, p): _dot(h_t[slot, uidx[u], p],
                                bd(pair(p_all[u][0:CHUNK], p) + h_a12v[slot, uidx[u], p, 0:CHUNK, :]).astype(BF16))
                   for u in us for p in pairs}
            yield
            y_p = {(u, p): h_a12v[slot, uidx[u], p, CHUNK:, :]
                   - _dot(h_a3[slot, uidx[u], p], bd(u_p[(u, p)]).astype(BF16)) for u in us for p in pairs}
            u_all = {u: jnp.concatenate([u_p[(u, p)] for p in pairs], axis=1) for u in us}
            upd = {u: _tdot(jnp.concatenate([v_s[rows[u], :], -u_all[u]], axis=0).astype(BF16), h_kb[slot, uidx[u]])
                   for u in us}
            for u in us:
                o_ref[rows[u], :] += p_all[u][CHUNK:] + jnp.concatenate([y_p[(u, p)] for p in pairs], axis=1)
                s_mat[u[0]] = (s_mat[u[0]] + upd[u]) * h_gl[slot, uidx[u], 0:1, :] * blockdiag
            yield
        for d in dirs:
            st_s[d] = s_mat[d]

    _trace_interleaved(prepare(0, 0))

    def body(i, carry):
        _trace_interleaved(advance(2 * i, 0), prepare(2 * i + 1, 1))
        _trace_interleaved(advance(2 * i + 1, 1), prepare(jnp.minimum(2 * i + 2, ngroup - 1), 0))
        return carry

    lax.fori_loop(0, ngroup // 2, body, 0)

    def post(i, carry):
        rows = pl.ds(pl.multiple_of(i * ROWS, ROWS), ROWS)
        y = o_ref[rows, :]
        mean = _mm_exact_lhs(y, blockdiag_b) * (1.0 / HEAD_DIM)
        yc = y - mean
        var = _mm_exact_lhs(yc * yc, blockdiag_b) * (1.0 / HEAD_DIM)
        y = yc * lax.rsqrt(var + RWKV_GN_EPS) * gnw_ref[...] + gnb_ref[...]
        rk = r_s[rows, :] * (kd_s[0, rows, :] + kd_s[1, rows, :]) * rk_ref[...]
        bonus = _mm_exact_lhs(rk, blockdiag_b) * v_s[rows, :]
        o_ref[rows, :] = (y + bonus) * g_s[rows, :]
        return carry

    lax.fori_loop(0, seq // ROWS, post, 0)


def _rwkv(u, col_block, batch, seq, mu, w0, wup, a0, aup, gup, k_k, k_a, r_k, gn_w, gn_b):
    full = lambda shape: pl.BlockSpec(shape, lambda b: (0,) * len(shape))
    tok = lambda: pltpu.VMEM((seq, GROUP_W), F32)
    tok2 = lambda: pltpu.VMEM((2, seq, GROUP_W), F32)
    nunit = 2 * RWKV_GROUP
    return pl.pallas_call(
        functools.partial(_rwkv_kernel, seq=seq),
        grid=(batch,),
        in_specs=[pl.BlockSpec((seq, RWKV_COLS), lambda b: (b, col_block)),
                  full((2, RWKV_COLS)), full((2, GROUP_W)), full((2, RWKV_RANK2, GROUP_W)),
                  full((2, GROUP_W)), full((2, RWKV_RANK2, GROUP_W)), full((RWKV_RANK2, GROUP_W)),
                  full((1, GROUP_W)), full((1, GROUP_W)), full((1, GROUP_W)),
                  full((1, GROUP_W)), full((1, GROUP_W))],
        out_specs=pl.BlockSpec((seq, GROUP_W), lambda b: (b, 0)),
        out_shape=jax.ShapeDtypeStruct((batch * seq, GROUP_W), F32),
        scratch_shapes=[tok(), tok(), tok(), tok(), tok2(), tok2(), tok2(),
                        pltpu.VMEM((2, GROUP_W, GROUP_W), F32),
                        pltpu.VMEM((2, nunit, 2 * CHUNK, GROUP_W), BF16),
                        pltpu.VMEM((2, nunit, 2 * CHUNK, GROUP_W), BF16),
                        pltpu.VMEM((2, nunit, 8, GROUP_W), F32),
                        pltpu.VMEM((2, nunit, 2, CHUNK, 128), BF16),
                        pltpu.VMEM((2, nunit, 2, 2 * CHUNK, 128), F32),
                        pltpu.VMEM((2, nunit, 2, CHUNK, 128), BF16)],
        compiler_params=pltpu.CompilerParams(dimension_semantics=("parallel",),
                                             vmem_limit_bytes=V7X_VMEM_LIMIT),
        name="rwkv7",
    )(u, mu, w0, wup, a0, aup, gup, k_k, k_a, r_k, gn_w, gn_b)


def _natten_kernel(q_ref, k_ref, v_ref, bias_ref, o_ref, *, n_rows):
    steps = range(NA_ROWS_PER_STEP)
    hmasks = [_head_mask(h) for h in range(N_HEADS)]
    r = [pl.program_id(1) * NA_ROWS_PER_STEP + j for j in steps]
    start = [jnp.clip(r[j] - NA_WIN_ROWS // 2, 0, n_rows - NA_WIN_ROWS) for j in steps]
    win = [pl.ds(pl.multiple_of(start[j] * GRID_W, GRID_W), NA_WIN_ROWS * GRID_W) for j in steps]
    kw = [k_ref[win[j], :].astype(BF16) for j in steps]
    vw = [v_ref[win[j], :].astype(BF16) for j in steps]
    q = [q_ref[j * GRID_W:(j + 1) * GRID_W, :] * (HEAD_DIM ** -0.5) for j in steps]
    qs = [jnp.concatenate([q[j] * hmasks[h] for h in range(N_HEADS)], axis=0).astype(BF16) for j in steps]
    s = [_dot_t(qs[j], kw[j]) + bias_ref[start[j] - r[j] + NA_WIN_ROWS - 1] for j in steps]
    m = [jnp.max(s[j], axis=-1, keepdims=True) for j in steps]
    p = [jnp.exp(s[j] - m[j]) for j in steps]
    p = [p[j] / jnp.sum(p[j], axis=-1, keepdims=True) for j in steps]
    o = [_dot(p[j].astype(BF16), vw[j]) for j in steps]
    for j in steps:
        o_ref[j * GRID_W:(j + 1) * GRID_W, :] = sum(o[j][h * GRID_W:(h + 1) * GRID_W] * hmasks[h]
                                                    for h in range(N_HEADS))


def _natten(u, q_block, batch, seq, bias_tab):
    n_rows = seq // GRID_W
    nkeys = NA_WIN_ROWS * GRID_W
    nstep = n_rows // NA_ROWS_PER_STEP
    qrows = NA_ROWS_PER_STEP * GRID_W

    return pl.pallas_call(
        functools.partial(_natten_kernel, n_rows=n_rows),
        grid=(batch, nstep),
        in_specs=[pl.BlockSpec((qrows, GROUP_W), lambda b, i: (b * nstep + i, q_block)),
                  pl.BlockSpec((seq, GROUP_W), lambda b, i: (b, q_block + 1)),
                  pl.BlockSpec((seq, GROUP_W), lambda b, i: (b, q_block + 2)),
                  pl.BlockSpec((NA_WIN_ROWS, N_HEADS * GRID_W, nkeys), lambda b, i: (0, 0, 0))],
        out_specs=pl.BlockSpec((qrows, GROUP_W), lambda b, i: (b * nstep + i, 0)),
        out_shape=jax.ShapeDtypeStruct((batch * seq, GROUP_W), F32),
        compiler_params=pltpu.CompilerParams(dimension_semantics=("parallel", "arbitrary"),
                                             vmem_limit_bytes=V7X_VMEM_LIMIT),
        name="natten",
    )(u, u, u, bias_tab)


def _natten_bias_table(rel_bias):
    ncol = 2 * NA_WIN_COLS - 1
    qc = np.arange(GRID_W)
    kc = np.arange(GRID_W)
    ws = np.clip(qc - NA_WIN_COLS // 2, 0, GRID_W - NA_WIN_COLS)
    in_win = (kc[None, :] >= ws[:, None]) & (kc[None, :] < ws[:, None] + NA_WIN_COLS)
    col_i = np.clip(kc[None, :] - qc[:, None] + NA_WIN_COLS - 1, 0, ncol - 1)
    onehot = ((col_i[None] == np.arange(ncol)[:, None, None]) & in_win[None]).astype(np.float32)
    mask_add = np.where(in_win, 0.0, NA_MASKED).astype(np.float32)
    toep = jnp.einsum('hrc,cqk->hrqk', rel_bias.astype(F32), jnp.asarray(onehot), precision=HI) + mask_add
    tab = jnp.stack([toep[:, d:d + NA_WIN_ROWS] for d in range(NA_WIN_ROWS)])
    return tab.transpose(0, 1, 3, 2, 4).reshape(NA_WIN_ROWS, N_HEADS * GRID_W, NA_WIN_ROWS * GRID_W)


def _pack_w_in(w_in):
    o_ssd = RWKV_COLS
    o_dt_end = o_ssd + GROUP_W + SSD_XBC + 8
    o_lru_end = o_dt_end + LRU_COLS
    pad = jnp.zeros(w_in.shape[:-1] + (SSD_DT_PAD - 8,), w_in.dtype)
    return jnp.concatenate([w_in[..., :o_dt_end], pad, w_in[..., o_lru_end:], w_in[..., o_dt_end:o_lru_end]],
                           axis=-1).astype(BF16)


def _pad_rank(w_up):
    z = jnp.zeros_like(w_up[0])
    return jnp.stack([jnp.concatenate([w_up[0], z], axis=0), jnp.concatenate([z, w_up[1]], axis=0)])


def _block_diag(w):
    _, nb, n, _ = w.shape
    eye = jnp.eye(nb, dtype=w.dtype)
    return jnp.einsum('dkij,kl->dkilj', w, eye).reshape(2, nb * n, nb * n)


def _dt_expander():
    e64 = np.zeros((2, SSD_DT_PAD, GROUP_W), np.float32)
    for d in range(2):
        for h in range(N_HEADS):
            e64[d, d * N_HEADS + h, h * HEAD_DIM:(h + 1) * HEAD_DIM] = 1.0
    return jnp.asarray(e64, BF16)


def kernel(x, norm1_w, w_in, rwkv_shift_mu, rwkv_w0, rwkv_w_up, rwkv_a0, rwkv_a_up, rwkv_g_up, rwkv_k_k, rwkv_k_a, rwkv_r_k, rwkv_gn_w, rwkv_gn_b, ssd_conv_w, ssd_conv_b, ssd_dt_bias, ssd_a_log, ssd_d, ssd_norm_w, lru_conv_w, lru_conv_b, lru_gate_a_w, lru_gate_a_b, lru_gate_x_w, lru_gate_x_b, lru_lambda, na_rel_bias, w_out, norm2_w, w_mlp1, w_mlp2, final_norm_w):
    batch, seq, _ = x.shape
    depth = w_in.shape[0]
    h = x.reshape(batch * seq, D_MODEL)

    w_in_p = _pack_w_in(w_in)
    w_out_b = w_out.astype(BF16)
    w1_b = w_mlp1.astype(BF16)
    w2_b = w_mlp2.astype(BF16)
    e64 = _dt_expander()
    row = lambda a: a.reshape(1, -1)
    rep = lambda a, n: jnp.repeat(a, n, axis=-1)

    for l in range(depth):
        u = _norm_inproj(h, row(norm1_w[l]), w_in_p, l)
        y_a = _rwkv(u, 0, batch, seq, rwkv_shift_mu[l], rwkv_w0[l], _pad_rank(rwkv_w_up[l]).astype(BF16),
                    rwkv_a0[l], _pad_rank(rwkv_a_up[l]).astype(BF16), rwkv_g_up[l].astype(BF16),
                    row(rwkv_k_k[l]), row(rwkv_k_a[l]), row(rwkv_r_k[l]), row(rwkv_gn_w[l]), row(rwkv_gn_b[l]))
        y_b = _ssd(u, 1, batch, seq, ssd_conv_w[l], row(ssd_conv_b[l]),
                   rep(ssd_dt_bias[l], HEAD_DIM), rep(ssd_a_log[l], HEAD_DIM), e64,
                   row(rep(ssd_d[l], HEAD_DIM)), row(ssd_norm_w[l]))
        y_c = _lru(u, (RWKV_COLS + SSD_COLS_PAD + NA_COLS) // LRU_COLS, batch, seq,
                   lru_conv_w[l], row(lru_conv_b[l]),
                   _block_diag(lru_gate_a_w[l]).astype(BF16), lru_gate_a_b[l],
                   _block_diag(lru_gate_x_w[l]).astype(BF16), lru_gate_x_b[l], lru_lambda[l])
        y_d = _natten(u, (RWKV_COLS + SSD_COLS_PAD) // GROUP_W, batch, seq, _natten_bias_table(na_rel_bias[l]))
        h = _outproj_mlp(h, (y_a, y_b, y_c, y_d), w_out_b, row(norm2_w[l]), w1_b, w2_b, row(final_norm_w), l,
                         final_norm=(l == depth - 1))
    return h.reshape(batch, seq, D_MODEL)
```

```python
import functools

import numpy as np
import jax
import jax.numpy as jnp
from jax import lax
from jax.experimental import pallas as pl
from jax.experimental.pallas import tpu as pltpu

F32 = jnp.float32
BF16 = jnp.bfloat16
HI = lax.Precision.HIGHEST

D_MODEL = 1024
GRID_W = 64
GROUP_W = 256
HEAD_DIM = 64
N_HEADS = GROUP_W // HEAD_DIM
D_FF = 4 * D_MODEL
NORM_EPS = 1e-5

RWKV_RANK2 = 128
RWKV_DECAY_SCALE = 0.6065306597126334
RWKV_GN_EPS = 64e-5
RWKV_COLS = 3 * GROUP_W + 3 * RWKV_RANK2

SSD_D_STATE = 128
SSD_XBC = GROUP_W + 4 * SSD_D_STATE
SSD_DT_PAD = 128
SSD_COLS_PAD = GROUP_W + SSD_XBC + SSD_DT_PAD

LRU_C = 8.0
LRU_SCAN_BLOCKS = 8
LRU_COLS = 2 * GROUP_W
NA_COLS = 3 * GROUP_W
NA_WIN_ROWS = 8
NA_WIN_COLS = 16
NA_MASKED = -1e30
NA_ROWS_PER_STEP = 8

U_COLS = RWKV_COLS + SSD_COLS_PAD + NA_COLS + LRU_COLS
CHUNK = 64
RWKV_GROUP = 4
SSD_CHUNK = 256
ROWS = 256

V7X_VMEM_LIMIT = 60 * 1024 * 1024

RWKV_PREC_GRAM = "bf16"
RWKV_PREC_APPLY = "bf16"
RWKV_PREC_INV = "bf16"


def _dot(a, b, prec=None):
    return jnp.dot(a, b, preferred_element_type=F32, precision=prec)


def _dot_t(a, b, prec=None):
    return lax.dot_general(a, b, (((1,), (1,)), ((), ())), preferred_element_type=F32, precision=prec)


def _tdot(a, b, prec=None):
    return lax.dot_general(a, b, (((0,), (0,)), ((), ())), preferred_element_type=F32, precision=prec)


_NN = (((1,), (0,)), ((), ()))
_NT = (((1,), (1,)), ((), ()))
_TN = (((0,), (0,)), ((), ()))


def _mm(a, b, mode, dims=_NN):
    dg = lambda x, y: lax.dot_general(x, y, dims, preferred_element_type=F32)
    if mode == "hi":
        return lax.dot_general(a, b, dims, preferred_element_type=F32, precision=HI)
    ah = a.astype(BF16)
    bh = b.astype(BF16)
    if mode == "bf16":
        return dg(ah, bh)
    al = (a - ah.astype(F32)).astype(BF16)
    bl = (b - bh.astype(F32)).astype(BF16)
    return dg(ah, bh) + (dg(ah, bl) + dg(al, bh))


def _split3(x):
    x1 = x.astype(BF16)
    r1 = x - x1.astype(F32)
    x2 = r1.astype(BF16)
    x3 = (r1 - x2.astype(F32)).astype(BF16)
    return x1, x2, x3


def _mm_exact_lhs(x, w01):
    x1, x2, x3 = _split3(x)
    return (_dot(x3, w01) + _dot(x2, w01)) + _dot(x1, w01)


def _mm_exact_rhs(w01, x):
    x1, x2, x3 = _split3(x)
    return (_dot(w01, x3) + _dot(w01, x2)) + _dot(w01, x1)


def _iota(shape, dim):
    return lax.broadcasted_iota(jnp.int32, shape, dim)


def _head_mask(h, width=GROUP_W):
    lane = _iota((1, width), 1)
    return ((lane >= h * HEAD_DIM) & (lane < (h + 1) * HEAD_DIM)).astype(F32)


def _sigmoid(x):
    return 0.5 * jnp.tanh(0.5 * x) + 0.5


def _softplus(x):
    return jnp.maximum(x, 0.0) + jnp.log(1.0 + jnp.exp(-jnp.abs(x)))


def _shift_down(x, prev8, k):
    rx = pltpu.roll(x, k, 0)
    row8 = _iota((8, x.shape[1]), 0)
    head = jnp.where(row8 < k, pltpu.roll(prev8, k, 0), rx[0:8])
    return jnp.concatenate([head, rx[8:]], axis=0)


def _shift_up(x, next8, k):
    n = x.shape[0]
    rx = pltpu.roll(x, n - k, 0)
    row8 = _iota((8, x.shape[1]), 0)
    tail = jnp.where(row8 >= 8 - k, pltpu.roll(next8, 8 - k, 0), rx[n - 8:])
    return jnp.concatenate([rx[:n - 8], tail], axis=0)


def _halo_rows(ref, t0, nrows, seq, cols):
    pstart = pl.multiple_of(jnp.maximum(t0 - 8, 0), 8)
    nstart = pl.multiple_of(jnp.minimum(t0 + nrows, seq - 8), 8)
    prev8 = ref[pl.ds(pstart, 8), cols] * (t0 > 0).astype(F32)
    next8 = ref[pl.ds(nstart, 8), cols] * (t0 + nrows < seq).astype(F32)
    return prev8, next8


def _tri(n, upper):
    r = _iota((n, n), 0)
    c = _iota((n, n), 1)
    return ((c >= r) if upper else (c <= r)).astype(F32)


def _rmsnorm(x, w):
    ms = jnp.mean(x * x, axis=-1, keepdims=True)
    return x * lax.rsqrt(ms + NORM_EPS) * w


def _norm_inproj_kernel(x_ref, nw_ref, w_ref, o_ref, xn_ref, *, tn):
    xn_ref[...] = _rmsnorm(x_ref[...], nw_ref[...]).astype(BF16)
    for j in range(o_ref.shape[1] // tn):
        cols = slice(j * tn, (j + 1) * tn)
        o_ref[:, cols] = _dot(xn_ref[...], w_ref[:, cols])


def _norm_inproj(h, norm_w, w_bf16, layer, tm=1024, tn=896):
    t = h.shape[0]
    n = w_bf16.shape[2]
    return pl.pallas_call(
        functools.partial(_norm_inproj_kernel, tn=tn),
        grid=(t // tm,),
        in_specs=[pl.BlockSpec((tm, D_MODEL), lambda i: (i, 0)),
                  pl.BlockSpec((1, D_MODEL), lambda i: (0, 0)),
                  pl.BlockSpec((None, D_MODEL, n), lambda i: (layer, 0, 0), pipeline_mode=pl.Buffered(1))],
        out_specs=pl.BlockSpec((tm, n), lambda i: (i, 0)),
        out_shape=jax.ShapeDtypeStruct((t, n), F32),
        scratch_shapes=[pltpu.VMEM((tm, D_MODEL), BF16)],
        compiler_params=pltpu.CompilerParams(dimension_semantics=("parallel",),
                                             vmem_limit_bytes=V7X_VMEM_LIMIT),
        name="norm_inproj",
    )(h, norm_w, w_bf16)


def _outproj_mlp_kernel(h_ref, ya_ref, yb_ref, yc_ref, yd_ref, wo_ref, nw_ref, w1_ref, w2_ref, fw_ref,
                        o_ref, xn_ref, *, final_norm):
    f = pl.program_id(1)

    @pl.when(f == 0)
    def _():
        hv = h_ref[...]
        for g, y_ref in enumerate((ya_ref, yb_ref, yc_ref, yd_ref)):
            hv = hv + _dot(y_ref[...].astype(BF16), wo_ref[g * GROUP_W:(g + 1) * GROUP_W, :])
        xn_ref[...] = _rmsnorm(hv, nw_ref[...]).astype(BF16)
        o_ref[...] = hv

    m = _dot(xn_ref[...], w1_ref[...])
    a = jnp.square(jnp.maximum(m, 0.0))
    o_ref[...] += _dot(a.astype(BF16), w2_ref[...])

    if final_norm:
        @pl.when(f == pl.num_programs(1) - 1)
        def _():
            o_ref[...] = _rmsnorm(o_ref[...], fw_ref[...])


def _outproj_mlp(h, ys, wo_bf16, norm_w, w1_bf16, w2_bf16, final_w, layer, final_norm, tm=1024, tf=1024):
    t = h.shape[0]
    yspec = pl.BlockSpec((tm, GROUP_W), lambda i, f: (i, 0))
    return pl.pallas_call(
        functools.partial(_outproj_mlp_kernel, final_norm=final_norm),
        grid=(t // tm, D_FF // tf),
        in_specs=[pl.BlockSpec((tm, D_MODEL), lambda i, f: (i, 0)), yspec, yspec, yspec, yspec,
                  pl.BlockSpec((None, D_MODEL, D_MODEL), lambda i, f: (layer, 0, 0)),
                  pl.BlockSpec((1, D_MODEL), lambda i, f: (0, 0)),
                  pl.BlockSpec((None, D_MODEL, tf), lambda i, f: (layer, 0, f)),
                  pl.BlockSpec((None, tf, D_MODEL), lambda i, f: (layer, f, 0)),
                  pl.BlockSpec((1, D_MODEL), lambda i, f: (0, 0))],
        out_specs=pl.BlockSpec((tm, D_MODEL), lambda i, f: (i, 0)),
        out_shape=jax.ShapeDtypeStruct((t, D_MODEL), F32),
        scratch_shapes=[pltpu.VMEM((tm, D_MODEL), BF16)],
        compiler_params=pltpu.CompilerParams(dimension_semantics=("parallel", "arbitrary"),
                                             vmem_limit_bytes=V7X_VMEM_LIMIT),
        name="outproj_mlp",
    )(h, *ys, wo_bf16, norm_w, w1_bf16, w2_bf16, final_w)


def _lru_kernel(u_ref, cw_ref, cb_ref, wa_ref, ba_ref, wx_ref, bx_ref, lam_ref, o_ref,
                gate_s, a_s, b_s, *, seq):
    gcols = slice(0, GROUP_W)
    xcols = slice(GROUP_W, 2 * GROUP_W)

    def pre(i, carry):
        t0 = pl.multiple_of(i * ROWS, ROWS)
        rows = pl.ds(t0, ROWS)
        xi = u_ref[rows, xcols]
        prev8, next8 = _halo_rows(u_ref, t0, ROWS, seq, xcols)
        xf = (cw_ref[0:1, :] * _shift_down(xi, prev8, 2) + cw_ref[1:2, :] * _shift_down(xi, prev8, 1)
              + cw_ref[2:3, :] * xi + cw_ref[3:4, :] * _shift_up(xi, next8, 1) + cb_ref[...])
        gi = u_ref[rows, gcols]
        gate_s[rows, :] = 0.5 * gi * (1.0 + jnp.tanh(0.7978845608028654 * (gi + 0.044715 * gi * gi * gi)))
        xfb = xf.astype(BF16)
        for d in range(2):
            rec = _sigmoid(_dot(xfb, wa_ref[d]) + ba_ref[d:d + 1, :])
            inp = _sigmoid(_dot(xfb, wx_ref[d]) + bx_ref[d:d + 1, :])
            a = jnp.exp(rec * (-LRU_C * _softplus(-lam_ref[d:d + 1, :])))
            a_s[d, rows, :] = a
            b_s[d, rows, :] = jnp.sqrt(1.0 - a * a) * inp * xf
        return carry

    lax.fori_loop(0, seq // ROWS, pre, 0)

    row8 = _iota((8, GROUP_W), 0)

    def scan8_step(a, b, rev, s):
        if rev:
            keep = row8 < 8 - s
            a_sh = jnp.where(keep, pltpu.roll(a, 8 - s, 0), 1.0)
            b_sh = jnp.where(keep, pltpu.roll(b, 8 - s, 0), 0.0)
        else:
            keep = row8 >= s
            a_sh = jnp.where(keep, pltpu.roll(a, s, 0), 1.0)
            b_sh = jnp.where(keep, pltpu.roll(b, s, 0), 0.0)
        return a * a_sh, a * b_sh + b

    nblk = seq // 8
    per_iter = LRU_SCAN_BLOCKS

    def scan(i, carry):
        cf, cb = carry
        rows = [(0, pl.ds(pl.multiple_of((i * per_iter + j) * 8, 8), 8)) for j in range(per_iter)]
        rows += [(1, pl.ds(pl.multiple_of((nblk - 1 - i * per_iter - j) * 8, 8), 8)) for j in range(per_iter)]
        ab = [(a_s[d, r, :], b_s[d, r, :]) for d, r in rows]
        for s in (1, 2, 4):
            ab = [scan8_step(a, b, d == 1, s) for (a, b), (d, _) in zip(ab, rows)]
        for (a, b), (d, r) in zip(ab, rows):
            if d == 0:
                h = b + a * cf
                cf = h[7:8, :]
            else:
                h = b + a * cb
                cb = h[0:1, :]
            a_s[d, r, :] = h
        return cf, cb

    zero = jnp.zeros((1, GROUP_W), F32)
    lax.fori_loop(0, nblk // per_iter, scan, (zero, zero))

    def post(i, carry):
        rows = pl.ds(pl.multiple_of(i * ROWS, ROWS), ROWS)
        o_ref[rows, :] = gate_s[rows, :] * (a_s[0, rows, :] + a_s[1, rows, :])
        return carry

    lax.fori_loop(0, seq // ROWS, post, 0)


def _lru(u, col_block, batch, seq, cw, cb, wa, ba, wx, bx, lam):
    full = lambda shape: pl.BlockSpec(shape, lambda b: (0,) * len(shape))
    return pl.pallas_call(
        functools.partial(_lru_kernel, seq=seq),
        grid=(batch,),
        in_specs=[pl.BlockSpec((seq, LRU_COLS), lambda b: (b, col_block)),
                  full((4, GROUP_W)), full((1, GROUP_W)),
                  full((2, GROUP_W, GROUP_W)), full((2, GROUP_W)),
                  full((2, GROUP_W, GROUP_W)), full((2, GROUP_W)), full((2, GROUP_W))],
        out_specs=pl.BlockSpec((seq, GROUP_W), lambda b: (b, 0)),
        out_shape=jax.ShapeDtypeStruct((batch * seq, GROUP_W), F32),
        scratch_shapes=[pltpu.VMEM((seq, GROUP_W), F32),
                        pltpu.VMEM((2, seq, GROUP_W), F32),
                        pltpu.VMEM((2, seq, GROUP_W), F32)],
        compiler_params=pltpu.CompilerParams(dimension_semantics=("parallel",),
                                             vmem_limit_bytes=V7X_VMEM_LIMIT),
        name="rglru",
    )(u, cw, cb, wa, ba, wx, bx, lam)


def _ssd_kernel(u_ref, cw_ref, cb_ref, dtb64_ref, alog64_ref, e64_ref,
                dskip_ref, nw_ref, o_ref, xc_s, y_s, st_s, *, seq):
    zcols = slice(0, GROUP_W)
    xbc_cols = slice(GROUP_W, GROUP_W + SSD_XBC)
    dt_cols = slice(GROUP_W + SSD_XBC, SSD_COLS_PAD)

    def pre(i, carry):
        t0 = pl.multiple_of(i * ROWS, ROWS)
        rows = pl.ds(t0, ROWS)
        xi = u_ref[rows, xbc_cols]
        prev8, next8 = _halo_rows(u_ref, t0, ROWS, seq, xbc_cols)
        xc = (cw_ref[0:1, :] * _shift_down(xi, prev8, 2) + cw_ref[1:2, :] * _shift_down(xi, prev8, 1)
              + cw_ref[2:3, :] * xi + cw_ref[3:4, :] * _shift_up(xi, next8, 1) + cb_ref[...])
        xc = xc * _sigmoid(xc)
        xc_s[rows, :] = xc
        y_s[rows, :] = dskip_ref[...] * xc[:, 0:GROUP_W]
        return carry

    lax.fori_loop(0, seq // ROWS, pre, 0)

    st_s[...] = jnp.zeros_like(st_s)
    n = SSD_CHUNK
    nchunk = seq // n
    rr = _iota((n, n), 0)
    cc = _iota((n, n), 1)
    incls = [cc <= rr, cc >= rr]
    tris = [m.astype(BF16) for m in incls]
    edge_rows = [n - 1, 0]
    hmasks = [_head_mask(h) for h in range(N_HEADS)]
    dirs = (0, 1)
    groups = (0, 1)
    heads = range(N_HEADS)
    gsl = [slice(g * SSD_D_STATE, (g + 1) * SSD_D_STATE) for g in groups]

    def chunk_pair(t0s):
        rows = [pl.ds(t0, n) for t0 in t0s]
        xs = [xc_s[rows[d], 0:GROUP_W] for d in dirs]
        bm = [xc_s[rows[d], GROUP_W:2 * GROUP_W].astype(BF16) for d in dirs]
        cm = [xc_s[rows[d], 2 * GROUP_W:3 * GROUP_W].astype(BF16) for d in dirs]
        dtraw = [u_ref[rows[d], dt_cols] for d in dirs]
        dt64 = [_softplus(_mm_exact_lhs(dtraw[d], e64_ref[d]) + dtb64_ref[d:d + 1, :]) for d in dirs]
        adt64 = [dt64[d] * (-jnp.exp(alog64_ref[d:d + 1, :])) for d in dirs]
        xdt = [xs[d] * dt64[d] for d in dirs]
        xdt_b = [x.astype(BF16) for x in xdt]
        cs64 = [_mm_exact_rhs(tris[d], adt64[d]) for d in dirs]
        scores = [[_dot_t(cm[d][:, gsl[g]], bm[d][:, gsl[g]]) for g in groups] for d in dirs]
        y_off = [jnp.concatenate([_dot(cm[d][:, gsl[g]], st_s[d, g].astype(BF16)) for g in groups], axis=1)
                 for d in dirs]
        y = [y_off[d] * jnp.exp(cs64[d]) for d in dirs]
        for d in dirs:
            ms = []
            for h in heads:
                cb = jnp.broadcast_to(cs64[d][:, h * HEAD_DIM:h * HEAD_DIM + 1], (n, n))
                seg = jnp.where(incls[d], cb - cb.T, NA_MASKED)
                ms.append((scores[d][h // 2] * jnp.exp(seg)).astype(BF16))
            x_heads = jnp.concatenate([(xdt[d] * hmasks[h]).astype(BF16) for h in heads], axis=0)
            y[d] = y[d] + _dot(jnp.concatenate(ms, axis=1), x_heads)
        for d in dirs:
            y_s[rows[d], :] += y[d]
            edge = cs64[d][edge_rows[d]:edge_rows[d] + 1, :]
            xd = (xdt[d] * jnp.exp(edge - cs64[d])).astype(BF16)
            egrow = jnp.exp(edge)
            for g in groups:
                st_s[d, g] = st_s[d, g] * egrow[:, gsl[g]] + _tdot(bm[d][:, gsl[g]], xd[:, gsl[g]])

    def body(c, carry):
        chunk_pair([pl.multiple_of(c * n, n), pl.multiple_of((nchunk - 1 - c) * n, n)])
        return carry

    lax.fori_loop(0, nchunk, body, 0)

    def post(i, carry):
        rows = pl.ds(pl.multiple_of(i * ROWS, ROWS), ROWS)
        z = u_ref[rows, zcols]
        y = y_s[rows, :] * (z * _sigmoid(z))
        o_ref[rows, :] = _rmsnorm(y, nw_ref[...])
        return carry

    lax.fori_loop(0, seq // ROWS, post, 0)


def _ssd(u, col_block, batch, seq, cw, cb, dtb64, alog64, e64, dskip, nw):
    full = lambda shape: pl.BlockSpec(shape, lambda b: (0,) * len(shape))
    return pl.pallas_call(
        functools.partial(_ssd_kernel, seq=seq),
        grid=(batch,),
        in_specs=[pl.BlockSpec((seq, SSD_COLS_PAD), lambda b: (b, col_block)),
                  full((4, SSD_XBC)), full((1, SSD_XBC)),
                  full((2, GROUP_W)), full((2, GROUP_W)), full((2, SSD_DT_PAD, GROUP_W)),
                  full((1, GROUP_W)), full((1, GROUP_W))],
        out_specs=pl.BlockSpec((seq, GROUP_W), lambda b: (b, 0)),
        out_shape=jax.ShapeDtypeStruct((batch * seq, GROUP_W), F32),
        scratch_shapes=[pltpu.VMEM((seq, SSD_XBC), F32),
                        pltpu.VMEM((seq, GROUP_W), F32),
                        pltpu.VMEM((2, 2, SSD_D_STATE, 128), F32)],
        compiler_params=pltpu.CompilerParams(dimension_semantics=("parallel",),
                                             vmem_limit_bytes=V7X_VMEM_LIMIT),
        name="ssd",
    )(u, cw, cb, dtb64, alog64, e64, dskip, nw)


def _pair_blockdiag(x, pmasks):
    return jnp.concatenate([x * pmasks[0], x * pmasks[1]], axis=0)


def _unit_lower_inverses(n_mats, eye, blk_mask, pmasks):
    mm = lambda a, b: _mm(a, _pair_blockdiag(b, pmasks), RWKV_PREC_INV)
    ps = [-(n * blk_mask) for n in n_mats]
    tds = [eye + p for p in ps]
    for _ in range(3):
        ps = [mm(p, p) for p in ps]
        yield None
        tds = [t + mm(t, p) for t, p in zip(tds, ps)]
        yield None
    ms = [-mm(t, n * (1.0 - blk_mask)) for t, n in zip(tds, n_mats)]
    yield None
    m2s = [mm(m, m) for m in ms]
    yield None
    tos = [eye + m for m in ms]
    tos = [t + mm(t, m2) for t, m2 in zip(tos, m2s)]
    yield None
    yield [mm(to, td) for to, td in zip(tos, tds)]


def _trace_interleaved(*gens):
    live = list(gens)
    while live:
        for gen in list(live):
            try:
                next(gen)
            except StopIteration:
                live.remove(gen)


def _rwkv_kernel(u_ref, mu_ref, w0_ref, wup_ref, a0_ref, aup_ref, gup_ref, kk_ref, ka_ref, rk_ref,
                 gnw_ref, gnb_ref, o_ref,
                 r_s, v_s, n_s, g_s, lw_s, kd_s, b_s, st_s,
                 h_lhs, h_kb, h_gl, h_t, h_a12v, h_a3, *, seq):
    allc = slice(0, RWKV_COLS)
    lane_r = _iota((GROUP_W, GROUP_W), 0) >> 6
    lane_c = _iota((GROUP_W, GROUP_W), 1) >> 6
    blockdiag = (lane_r == lane_c).astype(F32)
    blockdiag_b = blockdiag.astype(BF16)

    def pre(i, carry):
        t0 = pl.multiple_of(i * ROWS, ROWS)
        rows = pl.ds(t0, ROWS)
        x = u_ref[rows, allc]
        prev8, next8 = _halo_rows(u_ref, t0, ROWS, seq, allc)
        prev = _shift_down(x, prev8, 1)
        nxt = _shift_up(x, next8, 1)
        x = x + mu_ref[0:1, :] * (prev - x) + mu_ref[1:2, :] * (nxt - x)
        r = x[:, 0:GROUP_W]
        k = x[:, GROUP_W:2 * GROUP_W]
        v = x[:, 2 * GROUP_W:3 * GROUP_W]
        wd = jnp.tanh(x[:, 3 * GROUP_W:3 * GROUP_W + RWKV_RANK2]).astype(BF16)
        ad = x[:, 3 * GROUP_W + RWKV_RANK2:3 * GROUP_W + 2 * RWKV_RANK2].astype(BF16)
        gd = x[:, 3 * GROUP_W + 2 * RWKV_RANK2:RWKV_COLS]
        kk = k * kk_ref[...]
        kk = kk * lax.rsqrt(_mm_exact_lhs(kk * kk, blockdiag_b) + 1e-12)
        r_s[rows, :] = r
        v_s[rows, :] = v
        n_s[rows, :] = kk
        g_s[rows, :] = _dot(_sigmoid(gd).astype(BF16), gup_ref[...])
        for d in range(2):
            z_w = w0_ref[d:d + 1, :] + _dot(wd, wup_ref[d])
            lw_s[d, rows, :] = -RWKV_DECAY_SCALE * _sigmoid(z_w)
            alpha = _sigmoid(a0_ref[d:d + 1, :] + _dot(ad, aup_ref[d]))
            kd_s[d, rows, :] = k * (1.0 + (alpha - 1.0) * ka_ref[...])
            b_s[d, rows, :] = alpha * kk
        o_ref[rows, :] = jnp.zeros((ROWS, GROUP_W), F32)
        return carry

    lax.fori_loop(0, seq // ROWS, pre, 0)

    st_s[...] = jnp.zeros_like(st_s)
    nchunk = seq // CHUNK
    rr = _iota((CHUNK, CHUNK), 0)
    cc = _iota((CHUNK, CHUNK), 1)
    eye = (rr == cc).astype(F32)
    blk16 = ((rr >> 4) == (cc >> 4)).astype(F32)
    hmasks = [_head_mask(h) for h in range(N_HEADS)]

    tris = [_tri(CHUNK, False).astype(BF16), _tri(CHUNK, True).astype(BF16)]
    stricts = [(cc < rr).astype(F32), (cc > rr).astype(F32)]
    incls = [(cc <= rr).astype(F32), (cc >= rr).astype(F32)]
    edge_rows = [CHUNK - 1, 0]
    dirs = (0, 1)
    heads = range(N_HEADS)
    lane128 = _iota((1, 128), 1)
    pmasks = [(lane128 < HEAD_DIM).astype(F32), (lane128 >= HEAD_DIM).astype(F32)]
    pairs = (0, 1)
    twice = lambda m: jnp.concatenate([m, m], axis=1)
    stricts2 = [twice(m) for m in stricts]
    incls2 = [twice(m) for m in incls]
    eye2 = twice(eye)
    blk16_2 = twice(blk16)
    sel12 = [jnp.concatenate([stricts2[d], incls2[d]], axis=0) for d in dirs]

    units = [(d, k) for d in dirs for k in range(RWKV_GROUP)]
    uidx = {u: j for j, u in enumerate(units)}
    ngroup = nchunk // RWKV_GROUP
    pair = lambda a, p: a[:, p * 128:(p + 1) * 128]
    bd = lambda a: _pair_blockdiag(a, pmasks)

    def group_rows(g):
        t0 = {(0, k): (g * RWKV_GROUP + k) * CHUNK for k in range(RWKV_GROUP)}
        t0.update({(1, k): (nchunk - 1 - g * RWKV_GROUP - k) * CHUNK for k in range(RWKV_GROUP)})
        return {u: pl.ds(pl.multiple_of(t0[u], CHUNK), CHUNK) for u in units}

    def prepare(g, slot):
        rows = group_rows(g)
        lw = {u: lw_s[u[0], rows[u], :] for u in units}
        cs = {u: _mm_exact_rhs(tris[u[0]], lw[u]) for u in units}
        yield
        ginv = {u: jnp.exp(-cs[u]) for u in units}
        kkt = {u: n_s[rows[u], :] * jnp.exp(cs[u] - lw[u]) for u in units}
        rt = {u: r_s[rows[u], :] * jnp.exp(cs[u]) for u in units}
        kh = {u: kd_s[u[0], rows[u], :] * ginv[u] for u in units}
        bh = {u: b_s[u[0], rows[u], :] * ginv[u] for u in units}
        v = {u: v_s[rows[u], :] for u in units}
        lhs = {u: jnp.concatenate([kkt[u], rt[u]], axis=0) for u in units}
        uh = [(u, h) for u in units for h in heads]
        up = [(u, p) for u in units for p in pairs]
        yield
        lm = {x: pair(lhs[x[0]], x[1] // 2) * pmasks[x[1] % 2] for x in uh}
        a_k = {x: _mm(lm[x], pair(kh[x[0]], x[1] // 2), RWKV_PREC_GRAM, _NT) for x in uh}
        a_b = {x: _mm(lm[x], pair(bh[x[0]], x[1] // 2), RWKV_PREC_GRAM, _NT) for x in uh}
        side = lambda f, x: jnp.concatenate([f((x[0], 2 * x[1])), f((x[0], 2 * x[1] + 1))], axis=1)
        yield
        a12v = {x: _mm(side(lambda y: a_k[y], x) * sel12[x[0][0]], bd(pair(v[x[0]], x[1])), RWKV_PREC_APPLY)
                for x in up}
        yield
        t_list = None
        for t_list in _unit_lower_inverses(
                [side(lambda y: a_b[y][0:CHUNK], x) * stricts2[x[0][0]] for x in up], eye2, blk16_2, pmasks):
            yield
        t_inv = dict(zip(up, t_list))
        for u in units:
            j = uidx[u]
            h_lhs[slot, j] = lhs[u].astype(BF16)
            h_kb[slot, j] = jnp.concatenate([kh[u], bh[u]], axis=0).astype(BF16)
            edge = cs[u][edge_rows[u[0]]:edge_rows[u[0]] + 1, :]
            h_gl[slot, j] = jnp.broadcast_to(jnp.exp(edge), (8, GROUP_W))
            for p in pairs:
                h_t[slot, j, p] = t_inv[(u, p)].astype(BF16)
                h_a12v[slot, j, p] = a12v[(u, p)]
                h_a3[slot, j, p] = (side(lambda y: a_b[y][CHUNK:], (u, p)) * incls2[u[0]]).astype(BF16)

    def advance(g, slot):
        rows = group_rows(g)
        s_mat = [st_s[d] for d in dirs]
        for k in range(RWKV_GROUP):
            us = [(d, k) for d in dirs]
            p_all = {u: _dot_t(h_lhs[slot, uidx[u]], s_mat[u[0]].astype(BF16)) for u in us}
            yield
            u_p = {(u, p): _dot(h_t[slot, uidx[u], p],
                                bd(pair(p_all[u][0:CHUNK], p) + h_a12v[slot, uidx[u], p, 0:CHUNK, :]).astype(BF16))
                   for u in us for p in pairs}
            yield
            y_p = {(u, p): h_a12v[slot, uidx[u], p, CHUNK:, :]
                   - _dot(h_a3[slot, uidx[u], p], bd(u_p[(u, p)]).astype(BF16)) for u in us for p in pairs}
            u_all = {u: jnp.concatenate([u_p[(u, p)] for p in pairs], axis=1) for u in us}
            upd = {u: _tdot(jnp.concatenate([v_s[rows[u], :], -u_all[u]], axis=0).astype(BF16), h_kb[slot, uidx[u]])
                   for u in us}
            for u in us:
                o_ref[rows[u], :] += p_all[u][CHUNK:] + jnp.concatenate([y_p[(u, p)] for p in pairs], axis=1)
                s_mat[u[0]] = (s_mat[u[0]] + upd[u]) * h_gl[slot, uidx[u], 0:1, :] * blockdiag
            yield
        for d in dirs:
            st_s[d] = s_mat[d]

    _trace_interleaved(prepare(0, 0))

    def body(i, carry):
        _trace_interleaved(advance(2 * i, 0), prepare(2 * i + 1, 1))
        _trace_interleaved(advance(2 * i + 1, 1), prepare(jnp.minimum(2 * i + 2, ngroup - 1), 0))
        return carry

    lax.fori_loop(0, ngroup // 2, body, 0)

    def post(i, carry):
        rows = pl.ds(pl.multiple_of(i * ROWS, ROWS), ROWS)
        y = o_ref[rows, :]
        mean = _mm_exact_lhs(y, blockdiag_b) * (1.0 / HEAD_DIM)
        yc = y - mean
        var = _mm_exact_lhs(yc * yc, blockdiag_b) * (1.0 / HEAD_DIM)
        y = yc * lax.rsqrt(var + RWKV_GN_EPS) * gnw_ref[...] + gnb_ref[...]
        rk = r_s[rows, :] * (kd_s[0, rows, :] + kd_s[1, rows, :]) * rk_ref[...]
        bonus = _mm_exact_lhs(rk, blockdiag_b) * v_s[rows, :]
        o_ref[rows, :] = (y + bonus) * g_s[rows, :]
        return carry

    lax.fori_loop(0, seq // ROWS, post, 0)


def _rwkv(u, col_block, batch, seq, mu, w0, wup, a0, aup, gup, k_k, k_a, r_k, gn_w, gn_b):
    full = lambda shape: pl.BlockSpec(shape, lambda b: (0,) * len(shape))
    tok = lambda: pltpu.VMEM((seq, GROUP_W), F32)
    tok2 = lambda: pltpu.VMEM((2, seq, GROUP_W), F32)
    nunit = 2 * RWKV_GROUP
    return pl.pallas_call(
        functools.partial(_rwkv_kernel, seq=seq),
        grid=(batch,),
        in_specs=[pl.BlockSpec((seq, RWKV_COLS), lambda b: (b, col_block)),
                  full((2, RWKV_COLS)), full((2, GROUP_W)), full((2, RWKV_RANK2, GROUP_W)),
                  full((2, GROUP_W)), full((2, RWKV_RANK2, GROUP_W)), full((RWKV_RANK2, GROUP_W)),
                  full((1, GROUP_W)), full((1, GROUP_W)), full((1, GROUP_W)),
                  full((1, GROUP_W)), full((1, GROUP_W))],
        out_specs=pl.BlockSpec((seq, GROUP_W), lambda b: (b, 0)),
        out_shape=jax.ShapeDtypeStruct((batch * seq, GROUP_W), F32),
        scratch_shapes=[tok(), tok(), tok(), tok(), tok2(), tok2(), tok2(),
                        pltpu.VMEM((2, GROUP_W, GROUP_W), F32),
                        pltpu.VMEM((2, nunit, 2 * CHUNK, GROUP_W), BF16),
                        pltpu.VMEM((2, nunit, 2 * CHUNK, GROUP_W), BF16),
                        pltpu.VMEM((2, nunit, 8, GROUP_W), F32),
                        pltpu.VMEM((2, nunit, 2, CHUNK, 128), BF16),
                        pltpu.VMEM((2, nunit, 2, 2 * CHUNK, 128), F32),
                        pltpu.VMEM((2, nunit, 2, CHUNK, 128), BF16)],
        compiler_params=pltpu.CompilerParams(dimension_semantics=("parallel",),
                                             vmem_limit_bytes=V7X_VMEM_LIMIT),
        name="rwkv7",
    )(u, mu, w0, wup, a0, aup, gup, k_k, k_a, r_k, gn_w, gn_b)


def _natten_kernel(q_ref, k_ref, v_ref, bias_ref, o_ref, *, n_rows):
    steps = range(NA_ROWS_PER_STEP)
    hmasks = [_head_mask(h) for h in range(N_HEADS)]
    r = [pl.program_id(1) * NA_ROWS_PER_STEP + j for j in steps]
    start = [jnp.clip(r[j] - NA_WIN_ROWS // 2, 0, n_rows - NA_WIN_ROWS) for j in steps]
    win = [pl.ds(pl.multiple_of(start[j] * GRID_W, GRID_W), NA_WIN_ROWS * GRID_W) for j in steps]
    kw = [k_ref[win[j], :].astype(BF16) for j in steps]
    vw = [v_ref[win[j], :].astype(BF16) for j in steps]
    q = [q_ref[j * GRID_W:(j + 1) * GRID_W, :] * (HEAD_DIM ** -0.5) for j in steps]
    qs = [jnp.concatenate([q[j] * hmasks[h] for h in range(N_HEADS)], axis=0).astype(BF16) for j in steps]
    s = [_dot_t(qs[j], kw[j]) + bias_ref[start[j] - r[j] + NA_WIN_ROWS - 1] for j in steps]
    m = [jnp.max(s[j], axis=-1, keepdims=True) for j in steps]
    p = [jnp.exp(s[j] - m[j]) for j in steps]
    p = [p[j] / jnp.sum(p[j], axis=-1, keepdims=True) for j in steps]
    o = [_dot(p[j].astype(BF16), vw[j]) for j in steps]
    for j in steps:
        o_ref[j * GRID_W:(j + 1) * GRID_W, :] = sum(o[j][h * GRID_W:(h + 1) * GRID_W] * hmasks[h]
                                                    for h in range(N_HEADS))


def _natten(u, q_block, batch, seq, bias_tab):
    n_rows = seq // GRID_W
    nkeys = NA_WIN_ROWS * GRID_W
    nstep = n_rows // NA_ROWS_PER_STEP
    qrows = NA_ROWS_PER_STEP * GRID_W

    return pl.pallas_call(
        functools.partial(_natten_kernel, n_rows=n_rows),
        grid=(batch, nstep),
        in_specs=[pl.BlockSpec((qrows, GROUP_W), lambda b, i: (b * nstep + i, q_block)),
                  pl.BlockSpec((seq, GROUP_W), lambda b, i: (b, q_block + 1)),
                  pl.BlockSpec((seq, GROUP_W), lambda b, i: (b, q_block + 2)),
                  pl.BlockSpec((NA_WIN_ROWS, N_HEADS * GRID_W, nkeys), lambda b, i: (0, 0, 0))],
        out_specs=pl.BlockSpec((qrows, GROUP_W), lambda b, i: (b * nstep + i, 0)),
        out_shape=jax.ShapeDtypeStruct((batch * seq, GROUP_W), F32),
        compiler_params=pltpu.CompilerParams(dimension_semantics=("parallel", "arbitrary"),
                                             vmem_limit_bytes=V7X_VMEM_LIMIT),
        name="natten",
    )(u, u, u, bias_tab)


def _natten_bias_table(rel_bias):
    ncol = 2 * NA_WIN_COLS - 1
    qc = np.arange(GRID_W)
    kc = np.arange(GRID_W)
    ws = np.clip(qc - NA_WIN_COLS // 2, 0, GRID_W - NA_WIN_COLS)
    in_win = (kc[None, :] >= ws[:, None]) & (kc[None, :] < ws[:, None] + NA_WIN_COLS)
    col_i = np.clip(kc[None, :] - qc[:, None] + NA_WIN_COLS - 1, 0, ncol - 1)
    onehot = ((col_i[None] == np.arange(ncol)[:, None, None]) & in_win[None]).astype(np.float32)
    mask_add = np.where(in_win, 0.0, NA_MASKED).astype(np.float32)
    toep = jnp.einsum('hrc,cqk->hrqk', rel_bias.astype(F32), jnp.asarray(onehot), precision=HI) + mask_add
    tab = jnp.stack([toep[:, d:d + NA_WIN_ROWS] for d in range(NA_WIN_ROWS)])
    return tab.transpose(0, 1, 3, 2, 4).reshape(NA_WIN_ROWS, N_HEADS * GRID_W, NA_WIN_ROWS * GRID_W)


def _pack_w_in(w_in):
    o_ssd = RWKV_COLS
    o_dt_end = o_ssd + GROUP_W + SSD_XBC + 8
    o_lru_end = o_dt_end + LRU_COLS
    pad = jnp.zeros(w_in.shape[:-1] + (SSD_DT_PAD - 8,), w_in.dtype)
    return jnp.concatenate([w_in[..., :o_dt_end], pad, w_in[..., o_lru_end:], w_in[..., o_dt_end:o_lru_end]],
                           axis=-1).astype(BF16)


def _pad_rank(w_up):
    z = jnp.zeros_like(w_up[0])
    return jnp.stack([jnp.concatenate([w_up[0], z], axis=0), jnp.concatenate([z, w_up[1]], axis=0)])


def _block_diag(w):
    _, nb, n, _ = w.shape
    eye = jnp.eye(nb, dtype=w.dtype)
    return jnp.einsum('dkij,kl->dkilj', w, eye).reshape(2, nb * n, nb * n)


def _dt_expander():
    e64 = np.zeros((2, SSD_DT_PAD, GROUP_W), np.float32)
    for d in range(2):
        for h in range(N_HEADS):
            e64[d, d * N_HEADS + h, h * HEAD_DIM:(h + 1) * HEAD_DIM] = 1.0
    return jnp.asarray(e64, BF16)


def kernel(x, norm1_w, w_in, rwkv_shift_mu, rwkv_w0, rwkv_w_up, rwkv_a0, rwkv_a_up, rwkv_g_up, rwkv_k_k, rwkv_k_a, rwkv_r_k, rwkv_gn_w, rwkv_gn_b, ssd_conv_w, ssd_conv_b, ssd_dt_bias, ssd_a_log, ssd_d, ssd_norm_w, lru_conv_w, lru_conv_b, lru_gate_a_w, lru_gate_a_b, lru_gate_x_w, lru_gate_x_b, lru_lambda, na_rel_bias, w_out, norm2_w, w_mlp1, w_mlp2, final_norm_w):
    batch, seq, _ = x.shape
    depth = w_in.shape[0]
    h = x.reshape(batch * seq, D_MODEL)

    w_in_p = _pack_w_in(w_in)
    w_out_b = w_out.astype(BF16)
    w1_b = w_mlp1.astype(BF16)
    w2_b = w_mlp2.astype(BF16)
    e64 = _dt_expander()
    row = lambda a: a.reshape(1, -1)
    rep = lambda a, n: jnp.repeat(a, n, axis=-1)

    for l in range(depth):
        u = _norm_inproj(h, row(norm1_w[l]), w_in_p, l)
        y_a = _rwkv(u, 0, batch, seq, rwkv_shift_mu[l], rwkv_w0[l], _pad_rank(rwkv_w_up[l]).astype(BF16),
                    rwkv_a0[l], _pad_rank(rwkv_a_up[l]).astype(BF16), rwkv_g_up[l].astype(BF16),
                    row(rwkv_k_k[l]), row(rwkv_k_a[l]), row(rwkv_r_k[l]), row(rwkv_gn_w[l]), row(rwkv_gn_b[l]))
        y_b = _ssd(u, 1, batch, seq, ssd_conv_w[l], row(ssd_conv_b[l]),
                   rep(ssd_dt_bias[l], HEAD_DIM), rep(ssd_a_log[l], HEAD_DIM), e64,
                   row(rep(ssd_d[l], HEAD_DIM)), row(ssd_norm_w[l]))
        y_c = _lru(u, (RWKV_COLS + SSD_COLS_PAD + NA_COLS) // LRU_COLS, batch, seq,
                   lru_conv_w[l], row(lru_conv_b[l]),
                   _block_diag(lru_gate_a_w[l]).astype(BF16), lru_gate_a_b[l],
                   _block_diag(lru_gate_x_w[l]).astype(BF16), lru_gate_x_b[l], lru_lambda[l])
        y_d = _natten(u, (RWKV_COLS + SSD_COLS_PAD) // GROUP_W, batch, seq, _natten_bias_table(na_rel_bias[l]))
        h = _outproj_mlp(h, (y_a, y_b, y_c, y_d), w_out_b, row(norm2_w[l]), w1_b, w2_b, row(final_norm_w), l,
                         final_norm=(l == depth - 1))
    return h.reshape(batch, seq, D_MODEL)
```

```python
import functools

import numpy as np
import jax
import jax.numpy as jnp
from jax import lax
from jax.experimental import pallas as pl
from jax.experimental.pallas import tpu as pltpu

F32 = jnp.float32
BF16 = jnp.bfloat16
HI = lax.Precision.HIGHEST

D_MODEL = 1024
GRID_W = 64
GROUP_W = 256
HEAD_DIM = 64
N_HEADS = GROUP_W // HEAD_DIM
D_FF = 4 * D_MODEL
NORM_EPS = 1e-5

RWKV_RANK2 = 128
RWKV_DECAY_SCALE = 0.6065306597126334
RWKV_GN_EPS = 64e-5
RWKV_COLS = 3 * GROUP_W + 3 * RWKV_RANK2

SSD_D_STATE = 128
SSD_XBC = GROUP_W + 4 * SSD_D_STATE
SSD_DT_PAD = 128
SSD_COLS_PAD = GROUP_W + SSD_XBC + SSD_DT_PAD

LRU_C = 8.0
LRU_SCAN_BLOCKS = 8
LRU_COLS = 2 * GROUP_W
NA_COLS = 3 * GROUP_W
NA_WIN_ROWS = 8
NA_WIN_COLS = 16
NA_MASKED = -1e30
NA_ROWS_PER_STEP = 8

U_COLS = RWKV_COLS + SSD_COLS_PAD + NA_COLS + LRU_COLS
CHUNK = 64
RWKV_GROUP = 4
SSD_CHUNK = 256
ROWS = 256

V7X_VMEM_LIMIT = 60 * 1024 * 1024

RWKV_PREC_GRAM = "bf16"
RWKV_PREC_APPLY = "bf16"
RWKV_PREC_INV = "bf16"


def _dot(a, b, prec=None):
    return jnp.dot(a, b, preferred_element_type=F32, precision=prec)


def _dot_t(a, b, prec=None):
    return lax.dot_general(a, b, (((1,), (1,)), ((), ())), preferred_element_type=F32, precision=prec)


def _tdot(a, b, prec=None):
    return lax.dot_general(a, b, (((0,), (0,)), ((), ())), preferred_element_type=F32, precision=prec)


_NN = (((1,), (0,)), ((), ()))
_NT = (((1,), (1,)), ((), ()))
_TN = (((0,), (0,)), ((), ()))


def _mm(a, b, mode, dims=_NN):
    dg = lambda x, y: lax.dot_general(x, y, dims, preferred_element_type=F32)
    if mode == "hi":
        return lax.dot_general(a, b, dims, preferred_element_type=F32, precision=HI)
    ah = a.astype(BF16)
    bh = b.astype(BF16)
    if mode == "bf16":
        return dg(ah, bh)
    al = (a - ah.astype(F32)).astype(BF16)
    bl = (b - bh.astype(F32)).astype(BF16)
    return dg(ah, bh) + (dg(ah, bl) + dg(al, bh))


def _split3(x):
    x1 = x.astype(BF16)
    r1 = x - x1.astype(F32)
    x2 = r1.astype(BF16)
    x3 = (r1 - x2.astype(F32)).astype(BF16)
    return x1, x2, x3


def _mm_exact_lhs(x, w01):
    x1, x2, x3 = _split3(x)
    return (_dot(x3, w01) + _dot(x2, w01)) + _dot(x1, w01)


def _mm_exact_rhs(w01, x):
    x1, x2, x3 = _split3(x)
    return (_dot(w01, x3) + _dot(w01, x2)) + _dot(w01, x1)


def _iota(shape, dim):
    return lax.broadcasted_iota(jnp.int32, shape, dim)


def _head_mask(h, width=GROUP_W):
    lane = _iota((1, width), 1)
    return ((lane >= h * HEAD_DIM) & (lane < (h + 1) * HEAD_DIM)).astype(F32)


def _sigmoid(x):
    return 0.5 * jnp.tanh(0.5 * x) + 0.5


def _softplus(x):
    return jnp.maximum(x, 0.0) + jnp.log(1.0 + jnp.exp(-jnp.abs(x)))


def _shift_down(x, prev8, k):
    rx = pltpu.roll(x, k, 0)
    row8 = _iota((8, x.shape[1]), 0)
    head = jnp.where(row8 < k, pltpu.roll(prev8, k, 0), rx[0:8])
    return jnp.concatenate([head, rx[8:]], axis=0)


def _shift_up(x, next8, k):
    n = x.shape[0]
    rx = pltpu.roll(x, n - k, 0)
    row8 = _iota((8, x.shape[1]), 0)
    tail = jnp.where(row8 >= 8 - k, pltpu.roll(next8, 8 - k, 0), rx[n - 8:])
    return jnp.concatenate([rx[:n - 8], tail], axis=0)


def _halo_rows(ref, t0, nrows, seq, cols):
    pstart = pl.multiple_of(jnp.maximum(t0 - 8, 0), 8)
    nstart = pl.multiple_of(jnp.minimum(t0 + nrows, seq - 8), 8)
    prev8 = ref[pl.ds(pstart, 8), cols] * (t0 > 0).astype(F32)
    next8 = ref[pl.ds(nstart, 8), cols] * (t0 + nrows < seq).astype(F32)
    return prev8, next8


def _tri(n, upper):
    r = _iota((n, n), 0)
    c = _iota((n, n), 1)
    return ((c >= r) if upper else (c <= r)).astype(F32)


def _rmsnorm(x, w):
    ms = jnp.mean(x * x, axis=-1, keepdims=True)
    return x * lax.rsqrt(ms + NORM_EPS) * w


def _norm_inproj_kernel(x_ref, nw_ref, w_ref, o_ref, xn_ref, *, tn):
    xn_ref[...] = _rmsnorm(x_ref[...], nw_ref[...]).astype(BF16)
    for j in range(o_ref.shape[1] // tn):
        cols = slice(j * tn, (j + 1) * tn)
        o_ref[:, cols] = _dot(xn_ref[...], w_ref[:, cols])


def _norm_inproj(h, norm_w, w_bf16, layer, tm=1024, tn=896):
    t = h.shape[0]
    n = w_bf16.shape[2]
    return pl.pallas_call(
        functools.partial(_norm_inproj_kernel, tn=tn),
        grid=(t // tm,),
        in_specs=[pl.BlockSpec((tm, D_MODEL), lambda i: (i, 0)),
                  pl.BlockSpec((1, D_MODEL), lambda i: (0, 0)),
                  pl.BlockSpec((None, D_MODEL, n), lambda i: (layer, 0, 0), pipeline_mode=pl.Buffered(1))],
        out_specs=pl.BlockSpec((tm, n), lambda i: (i, 0)),
        out_shape=jax.ShapeDtypeStruct((t, n), F32),
        scratch_shapes=[pltpu.VMEM((tm, D_MODEL), BF16)],
        compiler_params=pltpu.CompilerParams(dimension_semantics=("parallel",),
                                             vmem_limit_bytes=V7X_VMEM_LIMIT),
        name="norm_inproj",
    )(h, norm_w, w_bf16)


def _outproj_mlp_kernel(h_ref, ya_ref, yb_ref, yc_ref, yd_ref, wo_ref, nw_ref, w1_ref, w2_ref, fw_ref,
                        o_ref, xn_ref, *, final_norm, tf):
    hv = h_ref[...]
    for g, y_ref in enumerate((ya_ref, yb_ref, yc_ref, yd_ref)):
        hv = hv + _dot(y_ref[...].astype(BF16), wo_ref[g * GROUP_W:(g + 1) * GROUP_W, :])
    xn_ref[...] = _rmsnorm(hv, nw_ref[...]).astype(BF16)
    o_ref[...] = hv
    for f in range(D_FF // tf):
        cols = slice(f * tf, (f + 1) * tf)
        m = _dot(xn_ref[...], w1_ref[:, cols])
        a = jnp.square(jnp.maximum(m, 0.0))
        o_ref[...] += _dot(a.astype(BF16), w2_ref[cols, :])
    if final_norm:
        o_ref[...] = _rmsnorm(o_ref[...], fw_ref[...])


def _outproj_mlp(h, ys, wo_bf16, norm_w, w1_bf16, w2_bf16, final_w, layer, final_norm, tm=1024, tf=1024):
    t = h.shape[0]
    yspec = pl.BlockSpec((tm, GROUP_W), lambda i: (i, 0))
    resident = lambda shape: pl.BlockSpec((None,) + shape, lambda i: (layer, 0, 0), pipeline_mode=pl.Buffered(1))
    return pl.pallas_call(
        functools.partial(_outproj_mlp_kernel, final_norm=final_norm, tf=tf),
        grid=(t // tm,),
        in_specs=[pl.BlockSpec((tm, D_MODEL), lambda i: (i, 0)), yspec, yspec, yspec, yspec,
                  resident((D_MODEL, D_MODEL)),
                  pl.BlockSpec((1, D_MODEL), lambda i: (0, 0)),
                  resident((D_MODEL, D_FF)),
                  resident((D_FF, D_MODEL)),
                  pl.BlockSpec((1, D_MODEL), lambda i: (0, 0))],
        out_specs=pl.BlockSpec((tm, D_MODEL), lambda i: (i, 0)),
        out_shape=jax.ShapeDtypeStruct((t, D_MODEL), F32),
        scratch_shapes=[pltpu.VMEM((tm, D_MODEL), BF16)],
        compiler_params=pltpu.CompilerParams(dimension_semantics=("parallel",),
                                             vmem_limit_bytes=V7X_VMEM_LIMIT),
        name="outproj_mlp",
    )(h, *ys, wo_bf16, norm_w, w1_bf16, w2_bf16, final_w)


def _lru_kernel(u_ref, cw_ref, cb_ref, wa_ref, ba_ref, wx_ref, bx_ref, lam_ref, o_ref,
                gate_s, a_s, b_s, *, seq):
    gcols = slice(0, GROUP_W)
    xcols = slice(GROUP_W, 2 * GROUP_W)

    def pre(i, carry):
        t0 = pl.multiple_of(i * ROWS, ROWS)
        rows = pl.ds(t0, ROWS)
        xi = u_ref[rows, xcols]
        prev8, next8 = _halo_rows(u_ref, t0, ROWS, seq, xcols)
        xf = (cw_ref[0:1, :] * _shift_down(xi, prev8, 2) + cw_ref[1:2, :] * _shift_down(xi, prev8, 1)
              + cw_ref[2:3, :] * xi + cw_ref[3:4, :] * _shift_up(xi, next8, 1) + cb_ref[...])
        gi = u_ref[rows, gcols]
        gate_s[rows, :] = 0.5 * gi * (1.0 + jnp.tanh(0.7978845608028654 * (gi + 0.044715 * gi * gi * gi)))
        xfb = xf.astype(BF16)
        for d in range(2):
            rec = _sigmoid(_dot(xfb, wa_ref[d]) + ba_ref[d:d + 1, :])
            inp = _sigmoid(_dot(xfb, wx_ref[d]) + bx_ref[d:d + 1, :])
            a = jnp.exp(rec * (-LRU_C * _softplus(-lam_ref[d:d + 1, :])))
            a_s[d, rows, :] = a
            b_s[d, rows, :] = jnp.sqrt(1.0 - a * a) * inp * xf
        return carry

    lax.fori_loop(0, seq // ROWS, pre, 0)

    row8 = _iota((8, GROUP_W), 0)

    def scan8_step(a, b, rev, s):
        if rev:
            keep = row8 < 8 - s
            a_sh = jnp.where(keep, pltpu.roll(a, 8 - s, 0), 1.0)
            b_sh = jnp.where(keep, pltpu.roll(b, 8 - s, 0), 0.0)
        else:
            keep = row8 >= s
            a_sh = jnp.where(keep, pltpu.roll(a, s, 0), 1.0)
            b_sh = jnp.where(keep, pltpu.roll(b, s, 0), 0.0)
        return a * a_sh, a * b_sh + b

    nblk = seq // 8
    per_iter = LRU_SCAN_BLOCKS

    def scan(i, carry):
        cf, cb = carry
        rows = [(0, pl.ds(pl.multiple_of((i * per_iter + j) * 8, 8), 8)) for j in range(per_iter)]
        rows += [(1, pl.ds(pl.multiple_of((nblk - 1 - i * per_iter - j) * 8, 8), 8)) for j in range(per_iter)]
        ab = [(a_s[d, r, :], b_s[d, r, :]) for d, r in rows]
        for s in (1, 2, 4):
            ab = [scan8_step(a, b, d == 1, s) for (a, b), (d, _) in zip(ab, rows)]
        for (a, b), (d, r) in zip(ab, rows):
            if d == 0:
                h = b + a * cf
                cf = h[7:8, :]
            else:
                h = b + a * cb
                cb = h[0:1, :]
            a_s[d, r, :] = h
        return cf, cb

    zero = jnp.zeros((1, GROUP_W), F32)
    lax.fori_loop(0, nblk // per_iter, scan, (zero, zero))

    def post(i, carry):
        rows = pl.ds(pl.multiple_of(i * ROWS, ROWS), ROWS)
        o_ref[rows, :] = gate_s[rows, :] * (a_s[0, rows, :] + a_s[1, rows, :])
        return carry

    lax.fori_loop(0, seq // ROWS, post, 0)


def _lru(u, col_block, batch, seq, cw, cb, wa, ba, wx, bx, lam):
    full = lambda shape: pl.BlockSpec(shape, lambda b: (0,) * len(shape))
    return pl.pallas_call(
        functools.partial(_lru_kernel, seq=seq),
        grid=(batch,),
        in_specs=[pl.BlockSpec((seq, LRU_COLS), lambda b: (b, col_block)),
                  full((4, GROUP_W)), full((1, GROUP_W)),
                  full((2, GROUP_W, GROUP_W)), full((2, GROUP_W)),
                  full((2, GROUP_W, GROUP_W)), full((2, GROUP_W)), full((2, GROUP_W))],
        out_specs=pl.BlockSpec((seq, GROUP_W), lambda b: (b, 0)),
        out_shape=jax.ShapeDtypeStruct((batch * seq, GROUP_W), F32),
        scratch_shapes=[pltpu.VMEM((seq, GROUP_W), F32),
                        pltpu.VMEM((2, seq, GROUP_W), F32),
                        pltpu.VMEM((2, seq, GROUP_W), F32)],
        compiler_params=pltpu.CompilerParams(dimension_semantics=("parallel",),
                                             vmem_limit_bytes=V7X_VMEM_LIMIT),
        name="rglru",
    )(u, cw, cb, wa, ba, wx, bx, lam)


def _ssd_kernel(u_ref, cw_ref, cb_ref, dtb64_ref, alog64_ref, e64_ref,
                dskip_ref, nw_ref, o_ref, xc_s, y_s, st_s, *, seq):
    zcols = slice(0, GROUP_W)
    xbc_cols = slice(GROUP_W, GROUP_W + SSD_XBC)
    dt_cols = slice(GROUP_W + SSD_XBC, SSD_COLS_PAD)

    def pre(i, carry):
        t0 = pl.multiple_of(i * ROWS, ROWS)
        rows = pl.ds(t0, ROWS)
        xi = u_ref[rows, xbc_cols]
        prev8, next8 = _halo_rows(u_ref, t0, ROWS, seq, xbc_cols)
        xc = (cw_ref[0:1, :] * _shift_down(xi, prev8, 2) + cw_ref[1:2, :] * _shift_down(xi, prev8, 1)
              + cw_ref[2:3, :] * xi + cw_ref[3:4, :] * _shift_up(xi, next8, 1) + cb_ref[...])
        xc = xc * _sigmoid(xc)
        xc_s[rows, :] = xc
        y_s[rows, :] = dskip_ref[...] * xc[:, 0:GROUP_W]
        return carry

    lax.fori_loop(0, seq // ROWS, pre, 0)

    st_s[...] = jnp.zeros_like(st_s)
    n = SSD_CHUNK
    nchunk = seq // n
    rr = _iota((n, n), 0)
    cc = _iota((n, n), 1)
    incls = [cc <= rr, cc >= rr]
    tris = [m.astype(BF16) for m in incls]
    edge_rows = [n - 1, 0]
    hmasks = [_head_mask(h) for h in range(N_HEADS)]
    dirs = (0, 1)
    groups = (0, 1)
    heads = range(N_HEADS)
    gsl = [slice(g * SSD_D_STATE, (g + 1) * SSD_D_STATE) for g in groups]

    def chunk_pair(t0s):
        rows = [pl.ds(t0, n) for t0 in t0s]
        xs = [xc_s[rows[d], 0:GROUP_W] for d in dirs]
        bm = [xc_s[rows[d], GROUP_W:2 * GROUP_W].astype(BF16) for d in dirs]
        cm = [xc_s[rows[d], 2 * GROUP_W:3 * GROUP_W].astype(BF16) for d in dirs]
        dtraw = [u_ref[rows[d], dt_cols] for d in dirs]
        dt64 = [_softplus(_mm_exact_lhs(dtraw[d], e64_ref[d]) + dtb64_ref[d:d + 1, :]) for d in dirs]
        adt64 = [dt64[d] * (-jnp.exp(alog64_ref[d:d + 1, :])) for d in dirs]
        xdt = [xs[d] * dt64[d] for d in dirs]
        xdt_b = [x.astype(BF16) for x in xdt]
        cs64 = [_mm_exact_rhs(tris[d], adt64[d]) for d in dirs]
        scores = [[_dot_t(cm[d][:, gsl[g]], bm[d][:, gsl[g]]) for g in groups] for d in dirs]
        y_off = [jnp.concatenate([_dot(cm[d][:, gsl[g]], st_s[d, g].astype(BF16)) for g in groups], axis=1)
                 for d in dirs]
        y = [y_off[d] * jnp.exp(cs64[d]) for d in dirs]
        for d in dirs:
            ms = []
            for h in heads:
                cb = jnp.broadcast_to(cs64[d][:, h * HEAD_DIM:h * HEAD_DIM + 1], (n, n))
                seg = jnp.where(incls[d], cb - cb.T, NA_MASKED)
                ms.append((scores[d][h // 2] * jnp.exp(seg)).astype(BF16))
            x_heads = jnp.concatenate([(xdt[d] * hmasks[h]).astype(BF16) for h in heads], axis=0)
            y[d] = y[d] + _dot(jnp.concatenate(ms, axis=1), x_heads)
        for d in dirs:
            y_s[rows[d], :] += y[d]
            edge = cs64[d][edge_rows[d]:edge_rows[d] + 1, :]
            xd = (xdt[d] * jnp.exp(edge - cs64[d])).astype(BF16)
            egrow = jnp.exp(edge)
            for g in groups:
                st_s[d, g] = st_s[d, g] * egrow[:, gsl[g]] + _tdot(bm[d][:, gsl[g]], xd[:, gsl[g]])

    def body(c, carry):
        chunk_pair([pl.multiple_of(c * n, n), pl.multiple_of((nchunk - 1 - c) * n, n)])
        return carry

    lax.fori_loop(0, nchunk, body, 0)

    def post(i, carry):
        rows = pl.ds(pl.multiple_of(i * ROWS, ROWS), ROWS)
        z = u_ref[rows, zcols]
        y = y_s[rows, :] * (z * _sigmoid(z))
        o_ref[rows, :] = _rmsnorm(y, nw_ref[...])
        return carry

    lax.fori_loop(0, seq // ROWS, post, 0)


def _ssd(u, col_block, batch, seq, cw, cb, dtb64, alog64, e64, dskip, nw):
    full = lambda shape: pl.BlockSpec(shape, lambda b: (0,) * len(shape))
    return pl.pallas_call(
        functools.partial(_ssd_kernel, seq=seq),
        grid=(batch,),
        in_specs=[pl.BlockSpec((seq, SSD_COLS_PAD), lambda b: (b, col_block)),
                  full((4, SSD_XBC)), full((1, SSD_XBC)),
                  full((2, GROUP_W)), full((2, GROUP_W)), full((2, SSD_DT_PAD, GROUP_W)),
                  full((1, GROUP_W)), full((1, GROUP_W))],
        out_specs=pl.BlockSpec((seq, GROUP_W), lambda b: (b, 0)),
        out_shape=jax.ShapeDtypeStruct((batch * seq, GROUP_W), F32),
        scratch_shapes=[pltpu.VMEM((seq, SSD_XBC), F32),
                        pltpu.VMEM((seq, GROUP_W), F32),
                        pltpu.VMEM((2, 2, SSD_D_STATE, 128), F32)],
        compiler_params=pltpu.CompilerParams(dimension_semantics=("parallel",),
                                             vmem_limit_bytes=V7X_VMEM_LIMIT),
        name="ssd",
    )(u, cw, cb, dtb64, alog64, e64, dskip, nw)


def _pair_blockdiag(x, pmasks):
    return jnp.concatenate([x * pmasks[0], x * pmasks[1]], axis=0)


def _unit_lower_inverses(n_mats, eye, blk_mask, pmasks):
    mm = lambda a, b: _mm(a, _pair_blockdiag(b, pmasks), RWKV_PREC_INV)
    ps = [-(n * blk_mask) for n in n_mats]
    tds = [eye + p for p in ps]
    for _ in range(3):
        ps = [mm(p, p) for p in ps]
        yield None
        tds = [t + mm(t, p) for t, p in zip(tds, ps)]
        yield None
    ms = [-mm(t, n * (1.0 - blk_mask)) for t, n in zip(tds, n_mats)]
    yield None
    m2s = [mm(m, m) for m in ms]
    yield None
    tos = [eye + m for m in ms]
    tos = [t + mm(t, m2) for t, m2 in zip(tos, m2s)]
    yield None
    yield [mm(to, td) for to, td in zip(tos, tds)]


def _trace_interleaved(*gens):
    live = list(gens)
    while live:
        for gen in list(live):
            try:
                next(gen)
            except StopIteration:
                live.remove(gen)


def _rwkv_kernel(u_ref, mu_ref, w0_ref, wup_ref, a0_ref, aup_ref, gup_ref, kk_ref, ka_ref, rk_ref,
                 gnw_ref, gnb_ref, o_ref,
                 r_s, v_s, n_s, g_s, lw_s, kd_s, b_s, st_s,
                 h_lhs, h_kb, h_gl, h_t, h_a12v, h_a3, *, seq):
    allc = slice(0, RWKV_COLS)
    lane_r = _iota((GROUP_W, GROUP_W), 0) >> 6
    lane_c = _iota((GROUP_W, GROUP_W), 1) >> 6
    blockdiag = (lane_r == lane_c).astype(F32)
    blockdiag_b = blockdiag.astype(BF16)

    def pre(i, carry):
        t0 = pl.multiple_of(i * ROWS, ROWS)
        rows = pl.ds(t0, ROWS)
        x = u_ref[rows, allc]
        prev8, next8 = _halo_rows(u_ref, t0, ROWS, seq, allc)
        prev = _shift_down(x, prev8, 1)
        nxt = _shift_up(x, next8, 1)
        x = x + mu_ref[0:1, :] * (prev - x) + mu_ref[1:2, :] * (nxt - x)
        r = x[:, 0:GROUP_W]
        k = x[:, GROUP_W:2 * GROUP_W]
        v = x[:, 2 * GROUP_W:3 * GROUP_W]
        wd = jnp.tanh(x[:, 3 * GROUP_W:3 * GROUP_W + RWKV_RANK2]).astype(BF16)
        ad = x[:, 3 * GROUP_W + RWKV_RANK2:3 * GROUP_W + 2 * RWKV_RANK2].astype(BF16)
        gd = x[:, 3 * GROUP_W + 2 * RWKV_RANK2:RWKV_COLS]
        kk = k * kk_ref[...]
        kk = kk * lax.rsqrt(_mm_exact_lhs(kk * kk, blockdiag_b) + 1e-12)
        r_s[rows, :] = r
        v_s[rows, :] = v
        n_s[rows, :] = kk
        g_s[rows, :] = _dot(_sigmoid(gd).astype(BF16), gup_ref[...])
        for d in range(2):
            z_w = w0_ref[d:d + 1, :] + _dot(wd, wup_ref[d])
            lw_s[d, rows, :] = -RWKV_DECAY_SCALE * _sigmoid(z_w)
            alpha = _sigmoid(a0_ref[d:d + 1, :] + _dot(ad, aup_ref[d]))
            kd_s[d, rows, :] = k * (1.0 + (alpha - 1.0) * ka_ref[...])
            b_s[d, rows, :] = alpha * kk
        o_ref[rows, :] = jnp.zeros((ROWS, GROUP_W), F32)
        return carry

    lax.fori_loop(0, seq // ROWS, pre, 0)

    st_s[...] = jnp.zeros_like(st_s)
    nchunk = seq // CHUNK
    rr = _iota((CHUNK, CHUNK), 0)
    cc = _iota((CHUNK, CHUNK), 1)
    eye = (rr == cc).astype(F32)
    blk16 = ((rr >> 4) == (cc >> 4)).astype(F32)
    hmasks = [_head_mask(h) for h in range(N_HEADS)]

    tris = [_tri(CHUNK, False).astype(BF16), _tri(CHUNK, True).astype(BF16)]
    stricts = [(cc < rr).astype(F32), (cc > rr).astype(F32)]
    incls = [(cc <= rr).astype(F32), (cc >= rr).astype(F32)]
    edge_rows = [CHUNK - 1, 0]
    dirs = (0, 1)
    heads = range(N_HEADS)
    lane128 = _iota((1, 128), 1)
    pmasks = [(lane128 < HEAD_DIM).astype(F32), (lane128 >= HEAD_DIM).astype(F32)]
    pairs = (0, 1)
    twice = lambda m: jnp.concatenate([m, m], axis=1)
    stricts2 = [twice(m) for m in stricts]
    incls2 = [twice(m) for m in incls]
    eye2 = twice(eye)
    blk16_2 = twice(blk16)
    sel12 = [jnp.concatenate([stricts2[d], incls2[d]], axis=0) for d in dirs]

    units = [(d, k) for d in dirs for k in range(RWKV_GROUP)]
    uidx = {u: j for j, u in enumerate(units)}
    ngroup = nchunk // RWKV_GROUP
    pair = lambda a, p: a[:, p * 128:(p + 1) * 128]
    bd = lambda a: _pair_blockdiag(a, pmasks)

    def group_rows(g):
        t0 = {(0, k): (g * RWKV_GROUP + k) * CHUNK for k in range(RWKV_GROUP)}
        t0.update({(1, k): (nchunk - 1 - g * RWKV_GROUP - k) * CHUNK for k in range(RWKV_GROUP)})
        return {u: pl.ds(pl.multiple_of(t0[u], CHUNK), CHUNK) for u in units}

    def prepare(g, slot):
        rows = group_rows(g)
        lw = {u: lw_s[u[0], rows[u], :] for u in units}
        cs = {u: _mm_exact_rhs(tris[u[0]], lw[u]) for u in units}
        yield
        ginv = {u: jnp.exp(-cs[u]) for u in units}
        kkt = {u: n_s[rows[u], :] * jnp.exp(cs[u] - lw[u]) for u in units}
        rt = {u: r_s[rows[u], :] * jnp.exp(cs[u]) for u in units}
        kh = {u: kd_s[u[0], rows[u], :] * ginv[u] for u in units}
        bh = {u: b_s[u[0], rows[u], :] * ginv[u] for u in units}
        v = {u: v_s[rows[u], :] for u in units}
        lhs = {u: jnp.concatenate([kkt[u], rt[u]], axis=0) for u in units}
        uh = [(u, h) for u in units for h in heads]
        up = [(u, p) for u in units for p in pairs]
        yield
        lm = {x: pair(lhs[x[0]], x[1] // 2) * pmasks[x[1] % 2] for x in uh}
        a_k = {x: _mm(lm[x], pair(kh[x[0]], x[1] // 2), RWKV_PREC_GRAM, _NT) for x in uh}
        a_b = {x: _mm(lm[x], pair(bh[x[0]], x[1] // 2), RWKV_PREC_GRAM, _NT) for x in uh}
        side = lambda f, x: jnp.concatenate([f((x[0], 2 * x[1])), f((x[0], 2 * x[1] + 1))], axis=1)
        yield
        a12v = {x: _mm(side(lambda y: a_k[y], x) * sel12[x[0][0]], bd(pair(v[x[0]], x[1])), RWKV_PREC_APPLY)
                for x in up}
        yield
        t_list = None
        for t_list in _unit_lower_inverses(
                [side(lambda y: a_b[y][0:CHUNK], x) * stricts2[x[0][0]] for x in up], eye2, blk16_2, pmasks):
            yield
        t_inv = dict(zip(up, t_list))
        for u in units:
            j = uidx[u]
            h_lhs[slot, j] = lhs[u].astype(BF16)
            h_kb[slot, j] = jnp.concatenate([kh[u], bh[u]], axis=0).astype(BF16)
            edge = cs[u][edge_rows[u[0]]:edge_rows[u[0]] + 1, :]
            h_gl[slot, j] = jnp.broadcast_to(jnp.exp(edge), (8, GROUP_W))
            for p in pairs:
                h_t[slot, j, p] = t_inv[(u, p)].astype(BF16)
                h_a12v[slot, j, p] = a12v[(u, p)]
                h_a3[slot, j, p] = (side(lambda y: a_b[y][CHUNK:], (u, p)) * incls2[u[0]]).astype(BF16)

    def advance(g, slot):
        rows = group_rows(g)
        s_mat = [st_s[d] for d in dirs]
        for k in range(RWKV_GROUP):
            us = [(d, k) for d in dirs]
            p_all = {u: _dot_t(h_lhs[slot, uidx[u]], s_mat[u[0]].astype(BF16)) for u in us}
            yield
            u_p = {(u, p): _dot(h_t[slot, uidx[u], p],
                                bd(pair(p_all[u][0:CHUNK], p) + h_a12v[slot, uidx[u], p, 0:CHUNK, :]).astype(BF16))
                   for u in us for p in pairs}
            yield
            y_p = {(u, p): h_a12v[slot, uidx[u], p, CHUNK:, :]
                   - _dot(h_a3[slot, uidx[u], p], bd(u_p[(u, p)]).astype(BF16)) for u in us for p in pairs}
            u_all = {u: jnp.concatenate([u_p[(u, p)] for p in pairs], axis=1) for u in us}
            upd = {u: _tdot(jnp.concatenate([v_s[rows[u], :], -u_all[u]], axis=0).astype(BF16), h_kb[slot, uidx[u]])
                   for u in us}
            for u in us:
                o_ref[rows[u], :] += p_all[u][CHUNK:] + jnp.concatenate([y_p[(u, p)] for p in pairs], axis=1)
                s_mat[u[0]] = (s_mat[u[0]] + upd[u]) * h_gl[slot, uidx[u], 0:1, :] * blockdiag
            yield
        for d in dirs:
            st_s[d] = s_mat[d]

    _trace_interleaved(prepare(0, 0))

    def body(i, carry):
        _trace_interleaved(advance(2 * i, 0), prepare(2 * i + 1, 1))
        _trace_interleaved(advance(2 * i + 1, 1), prepare(jnp.minimum(2 * i + 2, ngroup - 1), 0))
        return carry

    lax.fori_loop(0, ngroup // 2, body, 0)

    def post(i, carry):
        rows = pl.ds(pl.multiple_of(i * ROWS, ROWS), ROWS)
        y = o_ref[rows, :]
        mean = _mm_exact_lhs(y, blockdiag_b) * (1.0 / HEAD_DIM)
        yc = y - mean
        var = _mm_exact_lhs(yc * yc, blockdiag_b) * (1.0 / HEAD_DIM)
        y = yc * lax.rsqrt(var + RWKV_GN_EPS) * gnw_ref[...] + gnb_ref[...]
        rk = r_s[rows, :] * (kd_s[0, rows, :] + kd_s[1, rows, :]) * rk_ref[...]
        bonus = _mm_exact_lhs(rk, blockdiag_b) * v_s[rows, :]
        o_ref[rows, :] = (y + bonus) * g_s[rows, :]
        return carry

    lax.fori_loop(0, seq // ROWS, post, 0)


def _rwkv(u, col_block, batch, seq, mu, w0, wup, a0, aup, gup, k_k, k_a, r_k, gn_w, gn_b):
    full = lambda shape: pl.BlockSpec(shape, lambda b: (0,) * len(shape))
    tok = lambda: pltpu.VMEM((seq, GROUP_W), F32)
    tok2 = lambda: pltpu.VMEM((2, seq, GROUP_W), F32)
    nunit = 2 * RWKV_GROUP
    return pl.pallas_call(
        functools.partial(_rwkv_kernel, seq=seq),
        grid=(batch,),
        in_specs=[pl.BlockSpec((seq, RWKV_COLS), lambda b: (b, col_block)),
                  full((2, RWKV_COLS)), full((2, GROUP_W)), full((2, RWKV_RANK2, GROUP_W)),
                  full((2, GROUP_W)), full((2, RWKV_RANK2, GROUP_W)), full((RWKV_RANK2, GROUP_W)),
                  full((1, GROUP_W)), full((1, GROUP_W)), full((1, GROUP_W)),
                  full((1, GROUP_W)), full((1, GROUP_W))],
        out_specs=pl.BlockSpec((seq, GROUP_W), lambda b: (b, 0)),
        out_shape=jax.ShapeDtypeStruct((batch * seq, GROUP_W), F32),
        scratch_shapes=[tok(), tok(), tok(), tok(), tok2(), tok2(), tok2(),
                        pltpu.VMEM((2, GROUP_W, GROUP_W), F32),
                        pltpu.VMEM((2, nunit, 2 * CHUNK, GROUP_W), BF16),
                        pltpu.VMEM((2, nunit, 2 * CHUNK, GROUP_W), BF16),
                        pltpu.VMEM((2, nunit, 8, GROUP_W), F32),
                        pltpu.VMEM((2, nunit, 2, CHUNK, 128), BF16),
                        pltpu.VMEM((2, nunit, 2, 2 * CHUNK, 128), F32),
                        pltpu.VMEM((2, nunit, 2, CHUNK, 128), BF16)],
        compiler_params=pltpu.CompilerParams(dimension_semantics=("parallel",),
                                             vmem_limit_bytes=V7X_VMEM_LIMIT),
        name="rwkv7",
    )(u, mu, w0, wup, a0, aup, gup, k_k, k_a, r_k, gn_w, gn_b)


def _natten_kernel(q_ref, k_ref, v_ref, bias_ref, o_ref, *, n_rows):
    steps = range(NA_ROWS_PER_STEP)
    hmasks = [_head_mask(h) for h in range(N_HEADS)]
    r = [pl.program_id(1) * NA_ROWS_PER_STEP + j for j in steps]
    start = [jnp.clip(r[j] - NA_WIN_ROWS // 2, 0, n_rows - NA_WIN_ROWS) for j in steps]
    win = [pl.ds(pl.multiple_of(start[j] * GRID_W, GRID_W), NA_WIN_ROWS * GRID_W) for j in steps]
    kw = [k_ref[win[j], :].astype(BF16) for j in steps]
    vw = [v_ref[win[j], :].astype(BF16) for j in steps]
    q = [q_ref[j * GRID_W:(j + 1) * GRID_W, :] * (HEAD_DIM ** -0.5) for j in steps]
    qs = [jnp.concatenate([q[j] * hmasks[h] for h in range(N_HEADS)], axis=0).astype(BF16) for j in steps]
    s = [_dot_t(qs[j], kw[j]) + bias_ref[start[j] - r[j] + NA_WIN_ROWS - 1] for j in steps]
    m = [jnp.max(s[j], axis=-1, keepdims=True) for j in steps]
    p = [jnp.exp(s[j] - m[j]) for j in steps]
    p = [p[j] / jnp.sum(p[j], axis=-1, keepdims=True) for j in steps]
    o = [_dot(p[j].astype(BF16), vw[j]) for j in steps]
    for j in steps:
        o_ref[j * GRID_W:(j + 1) * GRID_W, :] = sum(o[j][h * GRID_W:(h + 1) * GRID_W] * hmasks[h]
                                                    for h in range(N_HEADS))


def _natten(u, q_block, batch, seq, bias_tab):
    n_rows = seq // GRID_W
    nkeys = NA_WIN_ROWS * GRID_W
    nstep = n_rows // NA_ROWS_PER_STEP
    qrows = NA_ROWS_PER_STEP * GRID_W

    return pl.pallas_call(
        functools.partial(_natten_kernel, n_rows=n_rows),
        grid=(batch, nstep),
        in_specs=[pl.BlockSpec((qrows, GROUP_W), lambda b, i: (b * nstep + i, q_block)),
                  pl.BlockSpec((seq, GROUP_W), lambda b, i: (b, q_block + 1)),
                  pl.BlockSpec((seq, GROUP_W), lambda b, i: (b, q_block + 2)),
                  pl.BlockSpec((NA_WIN_ROWS, N_HEADS * GRID_W, nkeys), lambda b, i: (0, 0, 0))],
        out_specs=pl.BlockSpec((qrows, GROUP_W), lambda b, i: (b * nstep + i, 0)),
        out_shape=jax.ShapeDtypeStruct((batch * seq, GROUP_W), F32),
        compiler_params=pltpu.CompilerParams(dimension_semantics=("parallel", "arbitrary"),
                                             vmem_limit_bytes=V7X_VMEM_LIMIT),
        name="natten",
    )(u, u, u, bias_tab)


def _natten_bias_table(rel_bias):
    ncol = 2 * NA_WIN_COLS - 1
    qc = np.arange(GRID_W)
    kc = np.arange(GRID_W)
    ws = np.clip(qc - NA_WIN_COLS // 2, 0, GRID_W - NA_WIN_COLS)
    in_win = (kc[None, :] >= ws[:, None]) & (kc[None, :] < ws[:, None] + NA_WIN_COLS)
    col_i = np.clip(kc[None, :] - qc[:, None] + NA_WIN_COLS - 1, 0, ncol - 1)
    onehot = ((col_i[None] == np.arange(ncol)[:, None, None]) & in_win[None]).astype(np.float32)
    mask_add = np.where(in_win, 0.0, NA_MASKED).astype(np.float32)
    toep = jnp.einsum('hrc,cqk->hrqk', rel_bias.astype(F32), jnp.asarray(onehot), precision=HI) + mask_add
    tab = jnp.stack([toep[:, d:d + NA_WIN_ROWS] for d in range(NA_WIN_ROWS)])
    return tab.transpose(0, 1, 3, 2, 4).reshape(NA_WIN_ROWS, N_HEADS * GRID_W, NA_WIN_ROWS * GRID_W)


def _pack_w_in(w_in):
    o_ssd = RWKV_COLS
    o_dt_end = o_ssd + GROUP_W + SSD_XBC + 8
    o_lru_end = o_dt_end + LRU_COLS
    pad = jnp.zeros(w_in.shape[:-1] + (SSD_DT_PAD - 8,), w_in.dtype)
    return jnp.concatenate([w_in[..., :o_dt_end], pad, w_in[..., o_lru_end:], w_in[..., o_dt_end:o_lru_end]],
                           axis=-1).astype(BF16)


def _pad_rank(w_up):
    z = jnp.zeros_like(w_up[0])
    return jnp.stack([jnp.concatenate([w_up[0], z], axis=0), jnp.concatenate([z, w_up[1]], axis=0)])


def _block_diag(w):
    _, nb, n, _ = w.shape
    eye = jnp.eye(nb, dtype=w.dtype)
    return jnp.einsum('dkij,kl->dkilj', w, eye).reshape(2, nb * n, nb * n)


def _dt_expander():
    e64 = np.zeros((2, SSD_DT_PAD, GROUP_W), np.float32)
    for d in range(2):
        for h in range(N_HEADS):
            e64[d, d * N_HEADS + h, h * HEAD_DIM:(h + 1) * HEAD_DIM] = 1.0
    return jnp.asarray(e64, BF16)


def kernel(x, norm1_w, w_in, rwkv_shift_mu, rwkv_w0, rwkv_w_up, rwkv_a0, rwkv_a_up, rwkv_g_up, rwkv_k_k, rwkv_k_a, rwkv_r_k, rwkv_gn_w, rwkv_gn_b, ssd_conv_w, ssd_conv_b, ssd_dt_bias, ssd_a_log, ssd_d, ssd_norm_w, lru_conv_w, lru_conv_b, lru_gate_a_w, lru_gate_a_b, lru_gate_x_w, lru_gate_x_b, lru_lambda, na_rel_bias, w_out, norm2_w, w_mlp1, w_mlp2, final_norm_w):
    batch, seq, _ = x.shape
    depth = w_in.shape[0]
    h = x.reshape(batch * seq, D_MODEL)

    w_in_p = _pack_w_in(w_in)
    w_out_b = w_out.astype(BF16)
    w1_b = w_mlp1.astype(BF16)
    w2_b = w_mlp2.astype(BF16)
    e64 = _dt_expander()
    row = lambda a: a.reshape(1, -1)
    rep = lambda a, n: jnp.repeat(a, n, axis=-1)

    for l in range(depth):
        u = _norm_inproj(h, row(norm1_w[l]), w_in_p, l)
        y_a = _rwkv(u, 0, batch, seq, rwkv_shift_mu[l], rwkv_w0[l], _pad_rank(rwkv_w_up[l]).astype(BF16),
                    rwkv_a0[l], _pad_rank(rwkv_a_up[l]).astype(BF16), rwkv_g_up[l].astype(BF16),
                    row(rwkv_k_k[l]), row(rwkv_k_a[l]), row(rwkv_r_k[l]), row(rwkv_gn_w[l]), row(rwkv_gn_b[l]))
        y_b = _ssd(u, 1, batch, seq, ssd_conv_w[l], row(ssd_conv_b[l]),
                   rep(ssd_dt_bias[l], HEAD_DIM), rep(ssd_a_log[l], HEAD_DIM), e64,
                   row(rep(ssd_d[l], HEAD_DIM)), row(ssd_norm_w[l]))
        y_c = _lru(u, (RWKV_COLS + SSD_COLS_PAD + NA_COLS) // LRU_COLS, batch, seq,
                   lru_conv_w[l], row(lru_conv_b[l]),
                   _block_diag(lru_gate_a_w[l]).astype(BF16), lru_gate_a_b[l],
                   _block_diag(lru_gate_x_w[l]).astype(BF16), lru_gate_x_b[l], lru_lambda[l])
        y_d = _natten(u, (RWKV_COLS + SSD_COLS_PAD) // GROUP_W, batch, seq, _natten_bias_table(na_rel_bias[l]))
        h = _outproj_mlp(h, (y_a, y_b, y_c, y_d), w_out_b, row(norm2_w[l]), w1_b, w2_b, row(final_norm_w), l,
                         final_norm=(l == depth - 1))
    return h.reshape(batch, seq, D_MODEL)
```

```python
import functools

import numpy as np
import jax
import jax.numpy as jnp
from jax import lax
from jax.experimental import pallas as pl
from jax.experimental.pallas import tpu as pltpu

F32 = jnp.float32
BF16 = jnp.bfloat16
HI = lax.Precision.HIGHEST

D_MODEL = 1024
GRID_W = 64
GROUP_W = 256
HEAD_DIM = 64
N_HEADS = GROUP_W // HEAD_DIM
D_FF = 4 * D_MODEL
NORM_EPS = 1e-5

RWKV_RANK2 = 128
RWKV_DECAY_SCALE = 0.6065306597126334
RWKV_GN_EPS = 64e-5
RWKV_COLS = 3 * GROUP_W + 3 * RWKV_RANK2

SSD_D_STATE = 128
SSD_XBC = GROUP_W + 4 * SSD_D_STATE
SSD_DT_PAD = 128
SSD_COLS_PAD = GROUP_W + SSD_XBC + SSD_DT_PAD

LRU_C = 8.0
LRU_SCAN_BLOCKS = 8
LRU_COLS = 2 * GROUP_W
NA_COLS = 3 * GROUP_W
NA_WIN_ROWS = 8
NA_WIN_COLS = 16
NA_MASKED = -1e30
NA_ROWS_PER_STEP = 8

U_COLS = RWKV_COLS + SSD_COLS_PAD + NA_COLS + LRU_COLS
CHUNK = 64
RWKV_GROUP = 4
SSD_CHUNK = 256
ROWS = 256

V7X_VMEM_LIMIT = 60 * 1024 * 1024

RWKV_PREC_GRAM = "bf16"
RWKV_PREC_APPLY = "bf16"
RWKV_PREC_INV = "bf16"


def _dot(a, b, prec=None):
    return jnp.dot(a, b, preferred_element_type=F32, precision=prec)


def _dot_t(a, b, prec=None):
    return lax.dot_general(a, b, (((1,), (1,)), ((), ())), preferred_element_type=F32, precision=prec)


def _tdot(a, b, prec=None):
    return lax.dot_general(a, b, (((0,), (0,)), ((), ())), preferred_element_type=F32, precision=prec)


_NN = (((1,), (0,)), ((), ()))
_NT = (((1,), (1,)), ((), ()))
_TN = (((0,), (0,)), ((), ()))


def _mm(a, b, mode, dims=_NN):
    dg = lambda x, y: lax.dot_general(x, y, dims, preferred_element_type=F32)
    if mode == "hi":
        return lax.dot_general(a, b, dims, preferred_element_type=F32, precision=HI)
    ah = a.astype(BF16)
    bh = b.astype(BF16)
    if mode == "bf16":
        return dg(ah, bh)
    al = (a - ah.astype(F32)).astype(BF16)
    bl = (b - bh.astype(F32)).astype(BF16)
    return dg(ah, bh) + (dg(ah, bl) + dg(al, bh))


def _split3(x):
    x1 = x.astype(BF16)
    r1 = x - x1.astype(F32)
    x2 = r1.astype(BF16)
    x3 = (r1 - x2.astype(F32)).astype(BF16)
    return x1, x2, x3


def _mm_exact_lhs(x, w01):
    x1, x2, x3 = _split3(x)
    return (_dot(x3, w01) + _dot(x2, w01)) + _dot(x1, w01)


def _mm_exact_rhs(w01, x):
    x1, x2, x3 = _split3(x)
    return (_dot(w01, x3) + _dot(w01, x2)) + _dot(w01, x1)


def _iota(shape, dim):
    return lax.broadcasted_iota(jnp.int32, shape, dim)


def _head_mask(h, width=GROUP_W):
    lane = _iota((1, width), 1)
    return ((lane >= h * HEAD_DIM) & (lane < (h + 1) * HEAD_DIM)).astype(F32)


def _sigmoid(x):
    return 0.5 * jnp.tanh(0.5 * x) + 0.5


def _softplus(x):
    return jnp.maximum(x, 0.0) + jnp.log(1.0 + jnp.exp(-jnp.abs(x)))


def _shift_down(x, prev8, k):
    rx = pltpu.roll(x, k, 0)
    row8 = _iota((8, x.shape[1]), 0)
    head = jnp.where(row8 < k, pltpu.roll(prev8, k, 0), rx[0:8])
    return jnp.concatenate([head, rx[8:]], axis=0)


def _shift_up(x, next8, k):
    n = x.shape[0]
    rx = pltpu.roll(x, n - k, 0)
    row8 = _iota((8, x.shape[1]), 0)
    tail = jnp.where(row8 >= 8 - k, pltpu.roll(next8, 8 - k, 0), rx[n - 8:])
    return jnp.concatenate([rx[:n - 8], tail], axis=0)


def _halo_rows(ref, t0, nrows, seq, cols):
    pstart = pl.multiple_of(jnp.maximum(t0 - 8, 0), 8)
    nstart = pl.multiple_of(jnp.minimum(t0 + nrows, seq - 8), 8)
    prev8 = ref[pl.ds(pstart, 8), cols] * (t0 > 0).astype(F32)
    next8 = ref[pl.ds(nstart, 8), cols] * (t0 + nrows < seq).astype(F32)
    return prev8, next8


def _tri(n, upper):
    r = _iota((n, n), 0)
    c = _iota((n, n), 1)
    return ((c >= r) if upper else (c <= r)).astype(F32)


def _rmsnorm(x, w):
    ms = jnp.mean(x * x, axis=-1, keepdims=True)
    return x * lax.rsqrt(ms + NORM_EPS) * w


def _norm_inproj_kernel(x_ref, nw_ref, w_ref, o_ref, xn_ref, *, tn):
    xn_ref[...] = _rmsnorm(x_ref[...], nw_ref[...]).astype(BF16)
    for j in range(o_ref.shape[1] // tn):
        cols = slice(j * tn, (j + 1) * tn)
        o_ref[:, cols] = _dot(xn_ref[...], w_ref[:, cols])


def _norm_inproj(h, norm_w, w_bf16, layer, tm=1024, tn=896):
    t = h.shape[0]
    n = w_bf16.shape[2]
    return pl.pallas_call(
        functools.partial(_norm_inproj_kernel, tn=tn),
        grid=(t // tm,),
        in_specs=[pl.BlockSpec((tm, D_MODEL), lambda i: (i, 0)),
                  pl.BlockSpec((1, D_MODEL), lambda i: (0, 0)),
                  pl.BlockSpec((None, D_MODEL, n), lambda i: (layer, 0, 0), pipeline_mode=pl.Buffered(1))],
        out_specs=pl.BlockSpec((tm, n), lambda i: (i, 0)),
        out_shape=jax.ShapeDtypeStruct((t, n), F32),
        scratch_shapes=[pltpu.VMEM((tm, D_MODEL), BF16)],
        compiler_params=pltpu.CompilerParams(dimension_semantics=("parallel",),
                                             vmem_limit_bytes=V7X_VMEM_LIMIT),
        name="norm_inproj",
    )(h, norm_w, w_bf16)


def _outproj_mlp_kernel(h_ref, ya_ref, yb_ref, yc_ref, yd_ref, wo_ref, nw_ref, w1_ref, w2_ref, fw_ref,
                        o_ref, xn_ref, *, final_norm, tf):
    hv = h_ref[...]
    for g, y_ref in enumerate((ya_ref, yb_ref, yc_ref, yd_ref)):
        hv = hv + _dot(y_ref[...].astype(BF16), wo_ref[g * GROUP_W:(g + 1) * GROUP_W, :])
    xn_ref[...] = _rmsnorm(hv, nw_ref[...]).astype(BF16)
    o_ref[...] = hv
    for f in range(D_FF // tf):
        cols = slice(f * tf, (f + 1) * tf)
        m = _dot(xn_ref[...], w1_ref[:, cols])
        a = jnp.square(jnp.maximum(m, 0.0))
        o_ref[...] += _dot(a.astype(BF16), w2_ref[cols, :])
    if final_norm:
        o_ref[...] = _rmsnorm(o_ref[...], fw_ref[...])


def _outproj_mlp(h, ys, wo_bf16, norm_w, w1_bf16, w2_bf16, final_w, layer, final_norm, tm=1024, tf=1024):
    t = h.shape[0]
    yspec = pl.BlockSpec((tm, GROUP_W), lambda i: (i, 0))
    resident = lambda shape: pl.BlockSpec((None,) + shape, lambda i: (layer, 0, 0), pipeline_mode=pl.Buffered(1))
    return pl.pallas_call(
        functools.partial(_outproj_mlp_kernel, final_norm=final_norm, tf=tf),
        grid=(t // tm,),
        in_specs=[pl.BlockSpec((tm, D_MODEL), lambda i: (i, 0)), yspec, yspec, yspec, yspec,
                  resident((D_MODEL, D_MODEL)),
                  pl.BlockSpec((1, D_MODEL), lambda i: (0, 0)),
                  resident((D_MODEL, D_FF)),
                  resident((D_FF, D_MODEL)),
                  pl.BlockSpec((1, D_MODEL), lambda i: (0, 0))],
        out_specs=pl.BlockSpec((tm, D_MODEL), lambda i: (i, 0)),
        out_shape=jax.ShapeDtypeStruct((t, D_MODEL), F32),
        scratch_shapes=[pltpu.VMEM((tm, D_MODEL), BF16)],
        compiler_params=pltpu.CompilerParams(
            dimension_semantics=("parallel",), vmem_limit_bytes=V7X_VMEM_LIMIT,
            allow_input_fusion=[False, False, False, False, False, True, False, True, True, False]),
        name="outproj_mlp",
    )(h, *ys, wo_bf16, norm_w, w1_bf16, w2_bf16, final_w)


def _lru_kernel(u_ref, cw_ref, cb_ref, wa_ref, ba_ref, wx_ref, bx_ref, lam_ref, o_ref,
                gate_s, a_s, b_s, *, seq):
    gcols = slice(0, GROUP_W)
    xcols = slice(GROUP_W, 2 * GROUP_W)

    def pre(i, carry):
        t0 = pl.multiple_of(i * ROWS, ROWS)
        rows = pl.ds(t0, ROWS)
        xi = u_ref[rows, xcols]
        prev8, next8 = _halo_rows(u_ref, t0, ROWS, seq, xcols)
        xf = (cw_ref[0:1, :] * _shift_down(xi, prev8, 2) + cw_ref[1:2, :] * _shift_down(xi, prev8, 1)
              + cw_ref[2:3, :] * xi + cw_ref[3:4, :] * _shift_up(xi, next8, 1) + cb_ref[...])
        gi = u_ref[rows, gcols]
        gate_s[rows, :] = 0.5 * gi * (1.0 + jnp.tanh(0.7978845608028654 * (gi + 0.044715 * gi * gi * gi)))
        xfb = xf.astype(BF16)
        for d in range(2):
            rec = _sigmoid(_dot(xfb, wa_ref[d]) + ba_ref[d:d + 1, :])
            inp = _sigmoid(_dot(xfb, wx_ref[d]) + bx_ref[d:d + 1, :])
            a = jnp.exp(rec * (-LRU_C * _softplus(-lam_ref[d:d + 1, :])))
            a_s[d, rows, :] = a
            b_s[d, rows, :] = jnp.sqrt(1.0 - a * a) * inp * xf
        return carry

    lax.fori_loop(0, seq // ROWS, pre, 0)

    row8 = _iota((8, GROUP_W), 0)

    def scan8_step(a, b, rev, s):
        if rev:
            keep = row8 < 8 - s
            a_sh = jnp.where(keep, pltpu.roll(a, 8 - s, 0), 1.0)
            b_sh = jnp.where(keep, pltpu.roll(b, 8 - s, 0), 0.0)
        else:
            keep = row8 >= s
            a_sh = jnp.where(keep, pltpu.roll(a, s, 0), 1.0)
            b_sh = jnp.where(keep, pltpu.roll(b, s, 0), 0.0)
        return a * a_sh, a * b_sh + b

    nblk = seq // 8
    per_iter = LRU_SCAN_BLOCKS

    def scan(i, carry):
        cf, cb = carry
        rows = [(0, pl.ds(pl.multiple_of((i * per_iter + j) * 8, 8), 8)) for j in range(per_iter)]
        rows += [(1, pl.ds(pl.multiple_of((nblk - 1 - i * per_iter - j) * 8, 8), 8)) for j in range(per_iter)]
        ab = [(a_s[d, r, :], b_s[d, r, :]) for d, r in rows]
        for s in (1, 2, 4):
            ab = [scan8_step(a, b, d == 1, s) for (a, b), (d, _) in zip(ab, rows)]
        for (a, b), (d, r) in zip(ab, rows):
            if d == 0:
                h = b + a * cf
                cf = h[7:8, :]
            else:
                h = b + a * cb
                cb = h[0:1, :]
            a_s[d, r, :] = h
        return cf, cb

    zero = jnp.zeros((1, GROUP_W), F32)
    lax.fori_loop(0, nblk // per_iter, scan, (zero, zero))

    def post(i, carry):
        rows = pl.ds(pl.multiple_of(i * ROWS, ROWS), ROWS)
        o_ref[rows, :] = gate_s[rows, :] * (a_s[0, rows, :] + a_s[1, rows, :])
        return carry

    lax.fori_loop(0, seq // ROWS, post, 0)


def _lru(u, col_block, batch, seq, cw, cb, wa, ba, wx, bx, lam):
    full = lambda shape: pl.BlockSpec(shape, lambda b: (0,) * len(shape))
    return pl.pallas_call(
        functools.partial(_lru_kernel, seq=seq),
        grid=(batch,),
        in_specs=[pl.BlockSpec((seq, LRU_COLS), lambda b: (b, col_block)),
                  full((4, GROUP_W)), full((1, GROUP_W)),
                  full((2, GROUP_W, GROUP_W)), full((2, GROUP_W)),
                  full((2, GROUP_W, GROUP_W)), full((2, GROUP_W)), full((2, GROUP_W))],
        out_specs=pl.BlockSpec((seq, GROUP_W), lambda b: (b, 0)),
        out_shape=jax.ShapeDtypeStruct((batch * seq, GROUP_W), F32),
        scratch_shapes=[pltpu.VMEM((seq, GROUP_W), F32),
                        pltpu.VMEM((2, seq, GROUP_W), F32),
                        pltpu.VMEM((2, seq, GROUP_W), F32)],
        compiler_params=pltpu.CompilerParams(dimension_semantics=("parallel",),
                                             vmem_limit_bytes=V7X_VMEM_LIMIT),
        name="rglru",
    )(u, cw, cb, wa, ba, wx, bx, lam)


def _ssd_kernel(u_ref, cw_ref, cb_ref, dtb64_ref, alog64_ref, e64_ref,
                dskip_ref, nw_ref, o_ref, xc_s, y_s, st_s, *, seq):
    zcols = slice(0, GROUP_W)
    xbc_cols = slice(GROUP_W, GROUP_W + SSD_XBC)
    dt_cols = slice(GROUP_W + SSD_XBC, SSD_COLS_PAD)

    def pre(i, carry):
        t0 = pl.multiple_of(i * ROWS, ROWS)
        rows = pl.ds(t0, ROWS)
        xi = u_ref[rows, xbc_cols]
        prev8, next8 = _halo_rows(u_ref, t0, ROWS, seq, xbc_cols)
        xc = (cw_ref[0:1, :] * _shift_down(xi, prev8, 2) + cw_ref[1:2, :] * _shift_down(xi, prev8, 1)
              + cw_ref[2:3, :] * xi + cw_ref[3:4, :] * _shift_up(xi, next8, 1) + cb_ref[...])
        xc = xc * _sigmoid(xc)
        xc_s[rows, :] = xc
        y_s[rows, :] = dskip_ref[...] * xc[:, 0:GROUP_W]
        return carry

    lax.fori_loop(0, seq // ROWS, pre, 0)

    st_s[...] = jnp.zeros_like(st_s)
    n = SSD_CHUNK
    nchunk = seq // n
    rr = _iota((n, n), 0)
    cc = _iota((n, n), 1)
    incls = [cc <= rr, cc >= rr]
    tris = [m.astype(BF16) for m in incls]
    edge_rows = [n - 1, 0]
    hmasks = [_head_mask(h) for h in range(N_HEADS)]
    dirs = (0, 1)
    groups = (0, 1)
    heads = range(N_HEADS)
    gsl = [slice(g * SSD_D_STATE, (g + 1) * SSD_D_STATE) for g in groups]

    def chunk_pair(t0s):
        rows = [pl.ds(t0, n) for t0 in t0s]
        xs = [xc_s[rows[d], 0:GROUP_W] for d in dirs]
        bm = [xc_s[rows[d], GROUP_W:2 * GROUP_W].astype(BF16) for d in dirs]
        cm = [xc_s[rows[d], 2 * GROUP_W:3 * GROUP_W].astype(BF16) for d in dirs]
        dtraw = [u_ref[rows[d], dt_cols] for d in dirs]
        dt64 = [_softplus(_mm_exact_lhs(dtraw[d], e64_ref[d]) + dtb64_ref[d:d + 1, :]) for d in dirs]
        adt64 = [dt64[d] * (-jnp.exp(alog64_ref[d:d + 1, :])) for d in dirs]
        xdt = [xs[d] * dt64[d] for d in dirs]
        xdt_b = [x.astype(BF16) for x in xdt]
        cs64 = [_mm_exact_rhs(tris[d], adt64[d]) for d in dirs]
        scores = [[_dot_t(cm[d][:, gsl[g]], bm[d][:, gsl[g]]) for g in groups] for d in dirs]
        y_off = [jnp.concatenate([_dot(cm[d][:, gsl[g]], st_s[d, g].astype(BF16)) for g in groups], axis=1)
                 for d in dirs]
        y = [y_off[d] * jnp.exp(cs64[d]) for d in dirs]
        for d in dirs:
            ms = []
            for h in heads:
                cb = jnp.broadcast_to(cs64[d][:, h * HEAD_DIM:h * HEAD_DIM + 1], (n, n))
                seg = jnp.where(incls[d], cb - cb.T, NA_MASKED)
                ms.append((scores[d][h // 2] * jnp.exp(seg)).astype(BF16))
            x_heads = jnp.concatenate([(xdt[d] * hmasks[h]).astype(BF16) for h in heads], axis=0)
            y[d] = y[d] + _dot(jnp.concatenate(ms, axis=1), x_heads)
        for d in dirs:
            y_s[rows[d], :] += y[d]
            edge = cs64[d][edge_rows[d]:edge_rows[d] + 1, :]
            xd = (xdt[d] * jnp.exp(edge - cs64[d])).astype(BF16)
            egrow = jnp.exp(edge)
            for g in groups:
                st_s[d, g] = st_s[d, g] * egrow[:, gsl[g]] + _tdot(bm[d][:, gsl[g]], xd[:, gsl[g]])

    def body(c, carry):
        chunk_pair([pl.multiple_of(c * n, n), pl.multiple_of((nchunk - 1 - c) * n, n)])
        return carry

    lax.fori_loop(0, nchunk, body, 0)

    def post(i, carry):
        rows = pl.ds(pl.multiple_of(i * ROWS, ROWS), ROWS)
        z = u_ref[rows, zcols]
        y = y_s[rows, :] * (z * _sigmoid(z))
        o_ref[rows, :] = _rmsnorm(y, nw_ref[...])
        return carry

    lax.fori_loop(0, seq // ROWS, post, 0)


def _ssd(u, col_block, batch, seq, cw, cb, dtb64, alog64, e64, dskip, nw):
    full = lambda shape: pl.BlockSpec(shape, lambda b: (0,) * len(shape))
    return pl.pallas_call(
        functools.partial(_ssd_kernel, seq=seq),
        grid=(batch,),
        in_specs=[pl.BlockSpec((seq, SSD_COLS_PAD), lambda b: (b, col_block)),
                  full((4, SSD_XBC)), full((1, SSD_XBC)),
                  full((2, GROUP_W)), full((2, GROUP_W)), full((2, SSD_DT_PAD, GROUP_W)),
                  full((1, GROUP_W)), full((1, GROUP_W))],
        out_specs=pl.BlockSpec((seq, GROUP_W), lambda b: (b, 0)),
        out_shape=jax.ShapeDtypeStruct((batch * seq, GROUP_W), F32),
        scratch_shapes=[pltpu.VMEM((seq, SSD_XBC), F32),
                        pltpu.VMEM((seq, GROUP_W), F32),
                        pltpu.VMEM((2, 2, SSD_D_STATE, 128), F32)],
        compiler_params=pltpu.CompilerParams(dimension_semantics=("parallel",),
                                             vmem_limit_bytes=V7X_VMEM_LIMIT),
        name="ssd",
    )(u, cw, cb, dtb64, alog64, e64, dskip, nw)


def _pair_blockdiag(x, pmasks):
    return jnp.concatenate([x * pmasks[0], x * pmasks[1]], axis=0)


def _unit_lower_inverses(n_mats, eye, blk_mask, pmasks):
    mm = lambda a, b: _mm(a, _pair_blockdiag(b, pmasks), RWKV_PREC_INV)
    ps = [-(n * blk_mask) for n in n_mats]
    tds = [eye + p for p in ps]
    for _ in range(3):
        ps = [mm(p, p) for p in ps]
        yield None
        tds = [t + mm(t, p) for t, p in zip(tds, ps)]
        yield None
    ms = [-mm(t, n * (1.0 - blk_mask)) for t, n in zip(tds, n_mats)]
    yield None
    m2s = [mm(m, m) for m in ms]
    yield None
    tos = [eye + m for m in ms]
    tos = [t + mm(t, m2) for t, m2 in zip(tos, m2s)]
    yield None
    yield [mm(to, td) for to, td in zip(tos, tds)]


def _trace_interleaved(*gens):
    live = list(gens)
    while live:
        for gen in list(live):
            try:
                next(gen)
            except StopIteration:
                live.remove(gen)


def _rwkv_kernel(u_ref, mu_ref, w0_ref, wup_ref, a0_ref, aup_ref, gup_ref, kk_ref, ka_ref, rk_ref,
                 gnw_ref, gnb_ref, o_ref,
                 r_s, v_s, n_s, g_s, lw_s, kd_s, b_s, st_s,
                 h_lhs, h_kb, h_gl, h_t, h_a12v, h_a3, *, seq):
    allc = slice(0, RWKV_COLS)
    lane_r = _iota((GROUP_W, GROUP_W), 0) >> 6
    lane_c = _iota((GROUP_W, GROUP_W), 1) >> 6
    blockdiag = (lane_r == lane_c).astype(F32)
    blockdiag_b = blockdiag.astype(BF16)

    def pre(i, carry):
        t0 = pl.multiple_of(i * ROWS, ROWS)
        rows = pl.ds(t0, ROWS)
        x = u_ref[rows, allc]
        prev8, next8 = _halo_rows(u_ref, t0, ROWS, seq, allc)
        prev = _shift_down(x, prev8, 1)
        nxt = _shift_up(x, next8, 1)
        x = x + mu_ref[0:1, :] * (prev - x) + mu_ref[1:2, :] * (nxt - x)
        r = x[:, 0:GROUP_W]
        k = x[:, GROUP_W:2 * GROUP_W]
        v = x[:, 2 * GROUP_W:3 * GROUP_W]
        wd = jnp.tanh(x[:, 3 * GROUP_W:3 * GROUP_W + RWKV_RANK2]).astype(BF16)
        ad = x[:, 3 * GROUP_W + RWKV_RANK2:3 * GROUP_W + 2 * RWKV_RANK2].astype(BF16)
        gd = x[:, 3 * GROUP_W + 2 * RWKV_RANK2:RWKV_COLS]
        kk = k * kk_ref[...]
        kk = kk * lax.rsqrt(_mm_exact_lhs(kk * kk, blockdiag_b) + 1e-12)
        r_s[rows, :] = r
        v_s[rows, :] = v
        n_s[rows, :] = kk
        g_s[rows, :] = _dot(_sigmoid(gd).astype(BF16), gup_ref[...])
        for d in range(2):
            z_w = w0_ref[d:d + 1, :] + _dot(wd, wup_ref[d])
            lw_s[d, rows, :] = -RWKV_DECAY_SCALE * _sigmoid(z_w)
            alpha = _sigmoid(a0_ref[d:d + 1, :] + _dot(ad, aup_ref[d]))
            kd_s[d, rows, :] = k * (1.0 + (alpha - 1.0) * ka_ref[...])
            b_s[d, rows, :] = alpha * kk
        o_ref[rows, :] = jnp.zeros((ROWS, GROUP_W), F32)
        return carry

    lax.fori_loop(0, seq // ROWS, pre, 0)

    st_s[...] = jnp.zeros_like(st_s)
    nchunk = seq // CHUNK
    rr = _iota((CHUNK, CHUNK), 0)
    cc = _iota((CHUNK, CHUNK), 1)
    eye = (rr == cc).astype(F32)
    blk16 = ((rr >> 4) == (cc >> 4)).astype(F32)
    hmasks = [_head_mask(h) for h in range(N_HEADS)]

    tris = [_tri(CHUNK, False).astype(BF16), _tri(CHUNK, True).astype(BF16)]
    stricts = [(cc < rr).astype(F32), (cc > rr).astype(F32)]
    incls = [(cc <= rr).astype(F32), (cc >= rr).astype(F32)]
    edge_rows = [CHUNK - 1, 0]
    dirs = (0, 1)
    heads = range(N_HEADS)
    lane128 = _iota((1, 128), 1)
    pmasks = [(lane128 < HEAD_DIM).astype(F32), (lane128 >= HEAD_DIM).astype(F32)]
    pairs = (0, 1)
    twice = lambda m: jnp.concatenate([m, m], axis=1)
    stricts2 = [twice(m) for m in stricts]
    incls2 = [twice(m) for m in incls]
    eye2 = twice(eye)
    blk16_2 = twice(blk16)
    sel12 = [jnp.concatenate([stricts2[d], incls2[d]], axis=0) for d in dirs]

    units = [(d, k) for d in dirs for k in range(RWKV_GROUP)]
    uidx = {u: j for j, u in enumerate(units)}
    ngroup = nchunk // RWKV_GROUP
    pair = lambda a, p: a[:, p * 128:(p + 1) * 128]
    bd = lambda a: _pair_blockdiag(a, pmasks)

    def group_rows(g):
        t0 = {(0, k): (g * RWKV_GROUP + k) * CHUNK for k in range(RWKV_GROUP)}
        t0.update({(1, k): (nchunk - 1 - g * RWKV_GROUP - k) * CHUNK for k in range(RWKV_GROUP)})
        return {u: pl.ds(pl.multiple_of(t0[u], CHUNK), CHUNK) for u in units}

    def prepare(g, slot):
        rows = group_rows(g)
        lw = {u: lw_s[u[0], rows[u], :] for u in units}
        cs = {u: _mm_exact_rhs(tris[u[0]], lw[u]) for u in units}
        yield
        ginv = {u: jnp.exp(-cs[u]) for u in units}
        kkt = {u: n_s[rows[u], :] * jnp.exp(cs[u] - lw[u]) for u in units}
        rt = {u: r_s[rows[u], :] * jnp.exp(cs[u]) for u in units}
        kh = {u: kd_s[u[0], rows[u], :] * ginv[u] for u in units}
        bh = {u: b_s[u[0], rows[u], :] * ginv[u] for u in units}
        v = {u: v_s[rows[u], :] for u in units}
        lhs = {u: jnp.concatenate([kkt[u], rt[u]], axis=0) for u in units}
        uh = [(u, h) for u in units for h in heads]
        up = [(u, p) for u in units for p in pairs]
        yield
        lm = {x: pair(lhs[x[0]], x[1] // 2) * pmasks[x[1] % 2] for x in uh}
        a_k = {x: _mm(lm[x], pair(kh[x[0]], x[1] // 2), RWKV_PREC_GRAM, _NT) for x in uh}
        a_b = {x: _mm(lm[x], pair(bh[x[0]], x[1] // 2), RWKV_PREC_GRAM, _NT) for x in uh}
        side = lambda f, x: jnp.concatenate([f((x[0], 2 * x[1])), f((x[0], 2 * x[1] + 1))], axis=1)
        yield
        a12v = {x: _mm(side(lambda y: a_k[y], x) * sel12[x[0][0]], bd(pair(v[x[0]], x[1])), RWKV_PREC_APPLY)
                for x in up}
        yield
        t_list = None
        for t_list in _unit_lower_inverses(
                [side(lambda y: a_b[y][0:CHUNK], x) * stricts2[x[0][0]] for x in up], eye2, blk16_2, pmasks):
            yield
        t_inv = dict(zip(up, t_list))
        for u in units:
            j = uidx[u]
            h_lhs[slot, j] = lhs[u].astype(BF16)
            h_kb[slot, j] = jnp.concatenate([kh[u], bh[u]], axis=0).astype(BF16)
            edge = cs[u][edge_rows[u[0]]:edge_rows[u[0]] + 1, :]
            h_gl[slot, j] = jnp.broadcast_to(jnp.exp(edge), (8, GROUP_W))
            for p in pairs:
                h_t[slot, j, p] = t_inv[(u, p)].astype(BF16)
                h_a12v[slot, j, p] = a12v[(u, p)]
                h_a3[slot, j, p] = (side(lambda y: a_b[y][CHUNK:], (u, p)) * incls2[u[0]]).astype(BF16)

    def advance(g, slot):
        rows = group_rows(g)
        s_mat = [st_s[d] for d in dirs]
        for k in range(RWKV_GROUP):
            us = [(d, k) for d in dirs]
            p_all = {u: _dot_t(h_lhs[slot, uidx[u]], s_mat[u[0]].astype(BF16)) for u in us}
            yield
            u_p = {(u, p): _dot(h_t[slot, uidx[u], p],
                                bd(pair(p_all[u][0:CHUNK], p) + h_a12v[slot, uidx[u], p, 0:CHUNK, :]).astype(BF16))
                   for u in us for p in pairs}
            yield
            y_p = {(u, p): h_a12v[slot, uidx[u], p, CHUNK:, :]
                   - _dot(h_a3[slot, uidx[u], p], bd(u_p[(u, p)]).astype(BF16)) for u in us for p in pairs}
            u_all = {u: jnp.concatenate([u_p[(u, p)] for p in pairs], axis=1) for u in us}
            upd = {u: _tdot(jnp.concatenate([v_s[rows[u], :], -u_all[u]], axis=0).astype(BF16), h_kb[slot, uidx[u]])
                   for u in us}
            for u in us:
                o_ref[rows[u], :] += p_all[u][CHUNK:] + jnp.concatenate([y_p[(u, p)] for p in pairs], axis=1)
                s_mat[u[0]] = (s_mat[u[0]] + upd[u]) * h_gl[slot, uidx[u], 0:1, :] * blockdiag
            yield
        for d in dirs:
            st_s[d] = s_mat[d]

    _trace_interleaved(prepare(0, 0))

    def body(i, carry):
        _trace_interleaved(advance(2 * i, 0), prepare(2 * i + 1, 1))
        _trace_interleaved(advance(2 * i + 1, 1), prepare(jnp.minimum(2 * i + 2, ngroup - 1), 0))
        return carry

    lax.fori_loop(0, ngroup // 2, body, 0)

    def post(i, carry):
        rows = pl.ds(pl.multiple_of(i * ROWS, ROWS), ROWS)
        y = o_ref[rows, :]
        mean = _mm_exact_lhs(y, blockdiag_b) * (1.0 / HEAD_DIM)
        yc = y - mean
        var = _mm_exact_lhs(yc * yc, blockdiag_b) * (1.0 / HEAD_DIM)
        y = yc * lax.rsqrt(var + RWKV_GN_EPS) * gnw_ref[...] + gnb_ref[...]
        rk = r_s[rows, :] * (kd_s[0, rows, :] + kd_s[1, rows, :]) * rk_ref[...]
        bonus = _mm_exact_lhs(rk, blockdiag_b) * v_s[rows, :]
        o_ref[rows, :] = (y + bonus) * g_s[rows, :]
        return carry

    lax.fori_loop(0, seq // ROWS, post, 0)


def _rwkv(u, col_block, batch, seq, mu, w0, wup, a0, aup, gup, k_k, k_a, r_k, gn_w, gn_b):
    full = lambda shape: pl.BlockSpec(shape, lambda b: (0,) * len(shape))
    tok = lambda: pltpu.VMEM((seq, GROUP_W), F32)
    tok2 = lambda: pltpu.VMEM((2, seq, GROUP_W), F32)
    nunit = 2 * RWKV_GROUP
    return pl.pallas_call(
        functools.partial(_rwkv_kernel, seq=seq),
        grid=(batch,),
        in_specs=[pl.BlockSpec((seq, RWKV_COLS), lambda b: (b, col_block)),
                  full((2, RWKV_COLS)), full((2, GROUP_W)), full((2, RWKV_RANK2, GROUP_W)),
                  full((2, GROUP_W)), full((2, RWKV_RANK2, GROUP_W)), full((RWKV_RANK2, GROUP_W)),
                  full((1, GROUP_W)), full((1, GROUP_W)), full((1, GROUP_W)),
                  full((1, GROUP_W)), full((1, GROUP_W))],
        out_specs=pl.BlockSpec((seq, GROUP_W), lambda b: (b, 0)),
        out_shape=jax.ShapeDtypeStruct((batch * seq, GROUP_W), F32),
        scratch_shapes=[tok(), tok(), tok(), tok(), tok2(), tok2(), tok2(),
                        pltpu.VMEM((2, GROUP_W, GROUP_W), F32),
                        pltpu.VMEM((2, nunit, 2 * CHUNK, GROUP_W), BF16),
                        pltpu.VMEM((2, nunit, 2 * CHUNK, GROUP_W), BF16),
                        pltpu.VMEM((2, nunit, 8, GROUP_W), F32),
                        pltpu.VMEM((2, nunit, 2, CHUNK, 128), BF16),
                        pltpu.VMEM((2, nunit, 2, 2 * CHUNK, 128), F32),
                        pltpu.VMEM((2, nunit, 2, CHUNK, 128), BF16)],
        compiler_params=pltpu.CompilerParams(dimension_semantics=("parallel",),
                                             vmem_limit_bytes=V7X_VMEM_LIMIT),
        name="rwkv7",
    )(u, mu, w0, wup, a0, aup, gup, k_k, k_a, r_k, gn_w, gn_b)


def _natten_kernel(q_ref, k_ref, v_ref, bias_ref, o_ref, *, n_rows):
    steps = range(NA_ROWS_PER_STEP)
    hmasks = [_head_mask(h) for h in range(N_HEADS)]
    r = [pl.program_id(1) * NA_ROWS_PER_STEP + j for j in steps]
    start = [jnp.clip(r[j] - NA_WIN_ROWS // 2, 0, n_rows - NA_WIN_ROWS) for j in steps]
    win = [pl.ds(pl.multiple_of(start[j] * GRID_W, GRID_W), NA_WIN_ROWS * GRID_W) for j in steps]
    kw = [k_ref[win[j], :].astype(BF16) for j in steps]
    vw = [v_ref[win[j], :].astype(BF16) for j in steps]
    q = [q_ref[j * GRID_W:(j + 1) * GRID_W, :] * (HEAD_DIM ** -0.5) for j in steps]
    qs = [jnp.concatenate([q[j] * hmasks[h] for h in range(N_HEADS)], axis=0).astype(BF16) for j in steps]
    s = [_dot_t(qs[j], kw[j]) + bias_ref[start[j] - r[j] + NA_WIN_ROWS - 1] for j in steps]
    m = [jnp.max(s[j], axis=-1, keepdims=True) for j in steps]
    p = [jnp.exp(s[j] - m[j]) for j in steps]
    p = [p[j] / jnp.sum(p[j], axis=-1, keepdims=True) for j in steps]
    o = [_dot(p[j].astype(BF16), vw[j]) for j in steps]
    for j in steps:
        o_ref[j * GRID_W:(j + 1) * GRID_W, :] = sum(o[j][h * GRID_W:(h + 1) * GRID_W] * hmasks[h]
                                                    for h in range(N_HEADS))


def _natten(u, q_block, batch, seq, bias_tab):
    n_rows = seq // GRID_W
    nkeys = NA_WIN_ROWS * GRID_W
    nstep = n_rows // NA_ROWS_PER_STEP
    qrows = NA_ROWS_PER_STEP * GRID_W

    return pl.pallas_call(
        functools.partial(_natten_kernel, n_rows=n_rows),
        grid=(batch, nstep),
        in_specs=[pl.BlockSpec((qrows, GROUP_W), lambda b, i: (b * nstep + i, q_block)),
                  pl.BlockSpec((seq, GROUP_W), lambda b, i: (b, q_block + 1)),
                  pl.BlockSpec((seq, GROUP_W), lambda b, i: (b, q_block + 2)),
                  pl.BlockSpec((NA_WIN_ROWS, N_HEADS * GRID_W, nkeys), lambda b, i: (0, 0, 0))],
        out_specs=pl.BlockSpec((qrows, GROUP_W), lambda b, i: (b * nstep + i, 0)),
        out_shape=jax.ShapeDtypeStruct((batch * seq, GROUP_W), F32),
        compiler_params=pltpu.CompilerParams(dimension_semantics=("parallel", "arbitrary"),
                                             vmem_limit_bytes=V7X_VMEM_LIMIT),
        name="natten",
    )(u, u, u, bias_tab)


def _natten_bias_table(rel_bias):
    ncol = 2 * NA_WIN_COLS - 1
    qc = np.arange(GRID_W)
    kc = np.arange(GRID_W)
    ws = np.clip(qc - NA_WIN_COLS // 2, 0, GRID_W - NA_WIN_COLS)
    in_win = (kc[None, :] >= ws[:, None]) & (kc[None, :] < ws[:, None] + NA_WIN_COLS)
    col_i = np.clip(kc[None, :] - qc[:, None] + NA_WIN_COLS - 1, 0, ncol - 1)
    onehot = ((col_i[None] == np.arange(ncol)[:, None, None]) & in_win[None]).astype(np.float32)
    mask_add = np.where(in_win, 0.0, NA_MASKED).astype(np.float32)
    toep = jnp.einsum('hrc,cqk->hrqk', rel_bias.astype(F32), jnp.asarray(onehot), precision=HI) + mask_add
    tab = jnp.stack([toep[:, d:d + NA_WIN_ROWS] for d in range(NA_WIN_ROWS)])
    return tab.transpose(0, 1, 3, 2, 4).reshape(NA_WIN_ROWS, N_HEADS * GRID_W, NA_WIN_ROWS * GRID_W)


def _pack_w_in(w_in):
    o_ssd = RWKV_COLS
    o_dt_end = o_ssd + GROUP_W + SSD_XBC + 8
    o_lru_end = o_dt_end + LRU_COLS
    pad = jnp.zeros(w_in.shape[:-1] + (SSD_DT_PAD - 8,), w_in.dtype)
    return jnp.concatenate([w_in[..., :o_dt_end], pad, w_in[..., o_lru_end:], w_in[..., o_dt_end:o_lru_end]],
                           axis=-1).astype(BF16)


def _pad_rank(w_up):
    z = jnp.zeros_like(w_up[0])
    return jnp.stack([jnp.concatenate([w_up[0], z], axis=0), jnp.concatenate([z, w_up[1]], axis=0)])


def _block_diag(w):
    _, nb, n, _ = w.shape
    eye = jnp.eye(nb, dtype=w.dtype)
    return jnp.einsum('dkij,kl->dkilj', w, eye).reshape(2, nb * n, nb * n)


def _dt_expander():
    e64 = np.zeros((2, SSD_DT_PAD, GROUP_W), np.float32)
    for d in range(2):
        for h in range(N_HEADS):
            e64[d, d * N_HEADS + h, h * HEAD_DIM:(h + 1) * HEAD_DIM] = 1.0
    return jnp.asarray(e64, BF16)


def kernel(x, norm1_w, w_in, rwkv_shift_mu, rwkv_w0, rwkv_w_up, rwkv_a0, rwkv_a_up, rwkv_g_up, rwkv_k_k, rwkv_k_a, rwkv_r_k, rwkv_gn_w, rwkv_gn_b, ssd_conv_w, ssd_conv_b, ssd_dt_bias, ssd_a_log, ssd_d, ssd_norm_w, lru_conv_w, lru_conv_b, lru_gate_a_w, lru_gate_a_b, lru_gate_x_w, lru_gate_x_b, lru_lambda, na_rel_bias, w_out, norm2_w, w_mlp1, w_mlp2, final_norm_w):
    batch, seq, _ = x.shape
    depth = w_in.shape[0]
    h = x.reshape(batch * seq, D_MODEL)

    w_in_p = _pack_w_in(w_in)
    w_out_b = w_out.astype(BF16)
    w1_b = w_mlp1.astype(BF16)
    w2_b = w_mlp2.astype(BF16)
    e64 = _dt_expander()
    row = lambda a: a.reshape(1, -1)
    rep = lambda a, n: jnp.repeat(a, n, axis=-1)

    for l in range(depth):
        u = _norm_inproj(h, row(norm1_w[l]), w_in_p, l)
        y_a = _rwkv(u, 0, batch, seq, rwkv_shift_mu[l], rwkv_w0[l], _pad_rank(rwkv_w_up[l]).astype(BF16),
                    rwkv_a0[l], _pad_rank(rwkv_a_up[l]).astype(BF16), rwkv_g_up[l].astype(BF16),
                    row(rwkv_k_k[l]), row(rwkv_k_a[l]), row(rwkv_r_k[l]), row(rwkv_gn_w[l]), row(rwkv_gn_b[l]))
        y_b = _ssd(u, 1, batch, seq, ssd_conv_w[l], row(ssd_conv_b[l]),
                   rep(ssd_dt_bias[l], HEAD_DIM), rep(ssd_a_log[l], HEAD_DIM), e64,
                   row(rep(ssd_d[l], HEAD_DIM)), row(ssd_norm_w[l]))
        y_c = _lru(u, (RWKV_COLS + SSD_COLS_PAD + NA_COLS) // LRU_COLS, batch, seq,
                   lru_conv_w[l], row(lru_conv_b[l]),
                   _block_diag(lru_gate_a_w[l]).astype(BF16), lru_gate_a_b[l],
                   _block_diag(lru_gate_x_w[l]).astype(BF16), lru_gate_x_b[l], lru_lambda[l])
        y_d = _natten(u, (RWKV_COLS + SSD_COLS_PAD) // GROUP_W, batch, seq, _natten_bias_table(na_rel_bias[l]))
        h = _outproj_mlp(h, (y_a, y_b, y_c, y_d), w_out_b, row(norm2_w[l]), w1_b, w2_b, row(final_norm_w), l,
                         final_norm=(l == depth - 1))
    return h.reshape(batch, seq, D_MODEL)
```
